```python
import jax, jax.numpy as jnp
from jax import lax
import numpy as np

D_MODEL = 1024
BATCH = 32
SEQ = 2048
DEPTH = 2
DEC_BATCH = 16
DEC_SEQ = 2048
PAST_LEN = 128

NORM_EPS = 1e-6
W_LRU = 256
W_MLA = 512
W_GLA = 256
LRU_BLOCKS = 8
LRU_BLOCK = W_LRU // LRU_BLOCKS
CONV_W = 4
CONV_LEFT = 2
CONV_RIGHT = 1
LRU_C = 8.0
MLA_HEADS = 4
QK_NOPE = 128
QK_ROPE = 64
QK_HEAD = QK_NOPE + QK_ROPE
V_HEAD = W_MLA // MLA_HEADS
Q_RANK = 256
KV_RANK = 128
ROPE_THETA = 10000.0
Q_BLOCK = 128
GLA_HEADS = 4
GLA_DK = 32
GLA_DV = W_GLA // GLA_HEADS
GLA_GATE_RANK = 16
GLA_TAU = 16.0
GLA_CHUNK = 64
D_FF = -(-8 * D_MODEL // (3 * 256)) * 256
IN_SPLITS = (W_LRU, W_LRU,
             Q_RANK, KV_RANK, QK_ROPE,
             GLA_HEADS * GLA_DK, GLA_HEADS * GLA_DK, W_GLA,
             GLA_GATE_RANK, GLA_GATE_RANK,
             W_GLA)
D_IN = sum(IN_SPLITS)

kernel_name = 'hymba_style_lru_mla_gla_encoder'


def rms_norm(x, g):
    xf = x.astype(jnp.float32)
    y = xf * lax.rsqrt(jnp.mean(xf * xf, axis=-1, keepdims=True) + NORM_EPS)
    return (y * g.astype(jnp.float32)).astype(x.dtype)


def _split_points():
    return [int(v) for v in np.cumsum(IN_SPLITS)[:-1]]


def _rotary_tables(seq_len, dtype):
    half = QK_ROPE // 2
    inv = 1.0 / (ROPE_THETA ** (jnp.arange(half, dtype=jnp.float32) * 2.0 / QK_ROPE))
    ang = jnp.arange(seq_len, dtype=jnp.float32)[:, None] * inv[None, :]
    return jnp.cos(ang).astype(dtype), jnp.sin(ang).astype(dtype)


def _rotary(x, cos, sin):
    x1, x2 = jnp.split(x, 2, axis=-1)
    return jnp.concatenate([x1 * cos - x2 * sin, x2 * cos + x1 * sin], axis=-1)


def _centred_dwconv(x, w, b):
    S = x.shape[1]
    xp = jnp.pad(x, ((0, 0), (CONV_LEFT, CONV_RIGHT), (0, 0)))
    y = xp[:, 0:S] * w[0]
    for j in range(1, CONV_W):
        y = y + xp[:, j:j + S] * w[j]
    return y + b


def _block_diag(x, w, b):
    xb = x.reshape(x.shape[:-1] + (LRU_BLOCKS, LRU_BLOCK))
    return jnp.einsum('bsnc,ncd->bsnd', xb, w).reshape(x.shape) + b


def _linear_combine(left, right):
    a_l, b_l = left
    a_r, b_r = right
    return a_l * a_r, a_r * b_l + b_r


def _rg_lru(x, w_a, b_a, w_x, b_x, lam, reverse):
    f32 = jnp.float32
    r = jax.nn.sigmoid(_block_diag(x, w_a, b_a).astype(f32))
    i = jax.nn.sigmoid(_block_diag(x, w_x, b_x).astype(f32))
    log_a = -LRU_C * r * jax.nn.softplus(-lam.astype(f32))
    a = jnp.exp(log_a)
    u = jnp.sqrt(-jnp.expm1(2.0 * log_a)) * (i * x.astype(f32))
    _, h = lax.associative_scan(_linear_combine, (a, u), axis=1, reverse=reverse)
    return h.astype(x.dtype)


def _mla(c_q, c_kv, k_rope, g_cq, w_uq, g_ckv, w_ukv, g_q, g_k, cos, sin):
    B, S, _ = c_q.shape
    q = jnp.einsum('bsr,re->bse', rms_norm(c_q, g_cq), w_uq).reshape(B, S, MLA_HEADS, QK_HEAD)
    kv = jnp.einsum('bsr,re->bse', rms_norm(c_kv, g_ckv), w_ukv).reshape(B, S, MLA_HEADS, QK_NOPE + V_HEAD)
    k_nope, v = kv[..., :QK_NOPE], kv[..., QK_NOPE:]
    k_r = jnp.broadcast_to(k_rope[:, :, None, :], (B, S, MLA_HEADS, QK_ROPE))
    k = jnp.concatenate([k_nope, k_r], axis=-1)
    q = rms_norm(q, g_q)
    k = rms_norm(k, g_k)
    cs, sn = cos[:, None, :], sin[:, None, :]
    q = jnp.concatenate([q[..., :QK_NOPE], _rotary(q[..., QK_NOPE:], cs, sn)], axis=-1)
    k = jnp.concatenate([k[..., :QK_NOPE], _rotary(k[..., QK_NOPE:], cs, sn)], axis=-1)
    q = q.transpose(0, 2, 1, 3)
    k = k.transpose(0, 2, 1, 3)
    v = v.transpose(0, 2, 1, 3)
    n_blk = S // Q_BLOCK
    qb = q.reshape(B, MLA_HEADS, n_blk, Q_BLOCK, QK_HEAD).transpose(2, 0, 1, 3, 4)
    scale = QK_HEAD ** -0.5

    def attend(q_blk):
        s = jnp.einsum('bhqd,bhkd->bhqk', q_blk, k).astype(jnp.float32) * scale
        p = jax.nn.softmax(s, axis=-1)
        return jnp.einsum('bhqk,bhkd->bhqd', p.astype(v.dtype), v)

    o = lax.map(attend, qb)
    return o.transpose(1, 0, 3, 2, 4).reshape(B, S, W_MLA)


def _gla_chunked(q, k, v, log_a):
    B, H, S, dk = q.shape
    dv = v.shape[-1]
    n = S // GLA_CHUNK

    def chunks(t):
        return t.reshape(B, H, n, GLA_CHUNK, t.shape[-1])

    q, k, v, log_a = chunks(q), chunks(k), chunks(v), chunks(log_a)
    b = jnp.cumsum(log_a, axis=3)
    b_last = b[:, :, :, -1:, :]
    q_dec = q * jnp.exp(b)
    k_inv = k * jnp.exp(-b)
    k_end = k * jnp.exp(b_last - b)
    lower_tri = jnp.tril(jnp.ones((GLA_CHUNK, GLA_CHUNK), q.dtype))
    scores = jnp.einsum('bhnqd,bhnkd->bhnqk', q_dec, k_inv) * lower_tri
    o_intra = jnp.einsum('bhnqk,bhnkv->bhnqv', scores, v)
    chunk_update = jnp.einsum('bhnkd,bhnkv->nbhdv', k_end, v)
    chunk_decay = jnp.exp(jnp.moveaxis(b_last[:, :, :, 0, :], 2, 0))

    def step(state, inp):
        decay, upd = inp
        return decay[..., None] * state + upd, state

    init = jnp.zeros((B, H, dk, dv), q.dtype)
    _, prev_state = lax.scan(step, init, (chunk_decay, chunk_update))
    o_inter = jnp.einsum('bhnqd,nbhdv->bhnqv', q_dec, prev_state)
    return (o_intra + o_inter).reshape(B, H, S, dv)


def _gla(q, k, v, gf, gb, og, wa2_f, ba2_f, wa2_b, ba2_b, g_o):
    B, S, _ = q.shape
    f32 = jnp.float32

    def heads(t, d):
        return t.reshape(B, S, GLA_HEADS, d).transpose(0, 2, 1, 3).astype(f32)

    qh = heads(q, GLA_DK) * GLA_DK ** -0.5
    kh = heads(k, GLA_DK)
    vh = heads(v, GLA_DV)
    la_f = heads(jax.nn.log_sigmoid((jnp.einsum('bsr,re->bse', gf, wa2_f) + ba2_f).astype(f32)) / GLA_TAU, GLA_DK)
    la_b = heads(jax.nn.log_sigmoid((jnp.einsum('bsr,re->bse', gb, wa2_b) + ba2_b).astype(f32)) / GLA_TAU, GLA_DK)
    o_f = _gla_chunked(qh, kh, vh, la_f)

    def flip(t):
        return jnp.flip(t, axis=2)

    o_b = flip(_gla_chunked(flip(qh), flip(kh), flip(vh), flip(la_b)))
    o = rms_norm(o_f + o_b, g_o)
    o = o.transpose(0, 2, 1, 3).reshape(B, S, W_GLA).astype(og.dtype)
    return o * jax.nn.silu(og)


def hybrid_layer(x, cos, sin, g_mix, w_in, conv_w, conv_b,
                 lru_wa_f, lru_ba_f, lru_wx_f, lru_bx_f, lru_lam_f,
                 lru_wa_b, lru_ba_b, lru_wx_b, lru_bx_b, lru_lam_b,
                 mla_g_cq, mla_w_uq, mla_g_ckv, mla_w_ukv, mla_g_q, mla_g_k,
                 gla_wa2_f, gla_ba2_f, gla_wa2_b, gla_ba2_b, gla_g_o,
                 w_out, g_ffn, w_ffn_in, w_ffn_out):
    h = rms_norm(x, g_mix)
    z = jnp.einsum('bsd,de->bse', h, w_in)
    (lru_in, lru_gate, c_q, c_kv, k_rope, gla_q, gla_k, gla_v,
     gla_gf, gla_gb, gla_og) = jnp.split(z, _split_points(), axis=-1)
    u = _centred_dwconv(lru_in, conv_w, conv_b)
    h_lru = (_rg_lru(u, lru_wa_f, lru_ba_f, lru_wx_f, lru_bx_f, lru_lam_f, reverse=False)
             + _rg_lru(u, lru_wa_b, lru_ba_b, lru_wx_b, lru_bx_b, lru_lam_b, reverse=True))
    o_lru = h_lru * jax.nn.gelu(lru_gate)
    o_mla = _mla(c_q, c_kv, k_rope, mla_g_cq, mla_w_uq, mla_g_ckv, mla_w_ukv, mla_g_q, mla_g_k, cos, sin)
    o_gla = _gla(gla_q, gla_k, gla_v, gla_gf, gla_gb, gla_og, gla_wa2_f, gla_ba2_f, gla_wa2_b, gla_ba2_b, gla_g_o)
    mixed = jnp.concatenate([o_lru, o_mla, o_gla], axis=-1)
    x = x + jnp.einsum('bse,ed->bsd', mixed, w_out)
    h = rms_norm(x, g_ffn)
    gate, up = jnp.split(jnp.einsum('bsd,df->bsf', h, w_ffn_in), 2, axis=-1)
    return x + jnp.einsum('bsf,fd->bsd', jax.nn.silu(gate) * up, w_ffn_out)


def setup_inputs(seed: int = 0) -> dict:
    key = jax.random.key(seed)
    keys = iter(jax.random.split(key, 48))
    f32 = jnp.float32
    L = DEPTH

    def normal(shape, scale):
        return scale * jax.random.normal(next(keys), shape, f32)

    def gain(shape):
        return 1.0 + 0.02 * jax.random.normal(next(keys), shape, f32)

    def lru_lambda():
        a = jax.random.uniform(next(keys), (L, W_LRU), f32, 0.9, 0.999)
        s = a ** (1.0 / LRU_C)
        return jnp.log(s) - jnp.log1p(-s)

    return {
        'x_prompt': normal((BATCH, SEQ, D_MODEL), 1.0),
        'x_sample': normal((DEC_BATCH, DEC_SEQ, D_MODEL), 1.0),
        'g_mix': gain((L, D_MODEL)),
        'w_in': normal((L, D_MODEL, D_IN), D_MODEL ** -0.5),
        'conv_w': normal((L, CONV_W, W_LRU), CONV_W ** -0.5),
        'conv_b': normal((L, W_LRU), 0.02),
        'lru_wa_f': normal((L, LRU_BLOCKS, LRU_BLOCK, LRU_BLOCK), LRU_BLOCK ** -0.5),
        'lru_ba_f': normal((L, W_LRU), 0.02),
        'lru_wx_f': normal((L, LRU_BLOCKS, LRU_BLOCK, LRU_BLOCK), LRU_BLOCK ** -0.5),
        'lru_bx_f': normal((L, W_LRU), 0.02),
        'lru_lam_f': lru_lambda(),
        'lru_wa_b': normal((L, LRU_BLOCKS, LRU_BLOCK, LRU_BLOCK), LRU_BLOCK ** -0.5),
        'lru_ba_b': normal((L, W_LRU), 0.02),
        'lru_wx_b': normal((L, LRU_BLOCKS, LRU_BLOCK, LRU_BLOCK), LRU_BLOCK ** -0.5),
        'lru_bx_b': normal((L, W_LRU), 0.02),
        'lru_lam_b': lru_lambda(),
        'mla_g_cq': gain((L, Q_RANK)),
        'mla_w_uq': normal((L, Q_RANK, MLA_HEADS * QK_HEAD), Q_RANK ** -0.5),
        'mla_g_ckv': gain((L, KV_RANK)),
        'mla_w_ukv': normal((L, KV_RANK, MLA_HEADS * (QK_NOPE + V_HEAD)), KV_RANK ** -0.5),
        'mla_g_q': gain((L, QK_HEAD)),
        'mla_g_k': gain((L, QK_HEAD)),
        'gla_wa2_f': normal((L, GLA_GATE_RANK, GLA_HEADS * GLA_DK), GLA_GATE_RANK ** -0.5),
        'gla_ba2_f': normal((L, GLA_HEADS * GLA_DK), 0.02),
        'gla_wa2_b': normal((L, GLA_GATE_RANK, GLA_HEADS * GLA_DK), GLA_GATE_RANK ** -0.5),
        'gla_ba2_b': normal((L, GLA_HEADS * GLA_DK), 0.02),
        'gla_g_o': gain((L, GLA_DV)),
        'w_out': normal((L, D_MODEL, D_MODEL), D_MODEL ** -0.5),
        'g_ffn': gain((L, D_MODEL)),
        'w_ffn_in': normal((L, D_MODEL, 2 * D_FF), D_MODEL ** -0.5),
        'w_ffn_out': normal((L, D_FF, D_MODEL), D_FF ** -0.5),
    }


def reference(x_prompt, x_sample, g_mix, w_in, conv_w, conv_b,
              lru_wa_f, lru_ba_f, lru_wx_f, lru_bx_f, lru_lam_f,
              lru_wa_b, lru_ba_b, lru_wx_b, lru_bx_b, lru_lam_b,
              mla_g_cq, mla_w_uq, mla_g_ckv, mla_w_ukv, mla_g_q, mla_g_k,
              gla_wa2_f, gla_ba2_f, gla_wa2_b, gla_ba2_b, gla_g_o,
              w_out, g_ffn, w_ffn_in, w_ffn_out):
    params = (g_mix, w_in, conv_w, conv_b,
              lru_wa_f, lru_ba_f, lru_wx_f, lru_bx_f, lru_lam_f,
              lru_wa_b, lru_ba_b, lru_wx_b, lru_bx_b, lru_lam_b,
              mla_g_cq, mla_w_uq, mla_g_ckv, mla_w_ukv, mla_g_q, mla_g_k,
              gla_wa2_f, gla_ba2_f, gla_wa2_b, gla_ba2_b, gla_g_o,
              w_out, g_ffn, w_ffn_in, w_ffn_out)

    def trunk(x):
        cos, sin = _rotary_tables(x.shape[1], x.dtype)
        for l in range(DEPTH):
            x = hybrid_layer(x, cos, sin, *[p[l] for p in params])
        return x

    y_prompt = trunk(x_prompt)
    y_sample = trunk(x_sample)
    return (y_prompt, y_sample)
```

```python
import functools

import jax
import jax.numpy as jnp
import numpy as np
from jax import lax
from jax.experimental import pallas as pl
from jax.experimental.pallas import tpu as pltpu

F32 = jnp.float32
BF16 = jnp.bfloat16

NORM_EPS = 1e-6
W_LRU = 256
W_MLA = 512
W_GLA = 256
LRU_BLOCKS = 8
LRU_C = 8.0
CONV_W = 4
CONV_LEFT = 2
MLA_HEADS = 4
QK_NOPE = 128
QK_ROPE = 64
QK_HEAD = QK_NOPE + QK_ROPE
V_HEAD = 128
Q_RANK = 256
KV_RANK = 128
ROPE_THETA = 10000.0
GLA_HEADS = 4
GLA_DK = 32
GLA_DV = 64
GLA_GATE_RANK = 16
GLA_TAU = 16.0
GLA_CHUNK = 64

LANES = 128
SUBLANES = 8
VMEM_LIMIT_BYTES = 56 * 1024 * 1024

ZL_W = 2 * W_LRU
ZM_W = Q_RANK + KV_RANK + 2 * QK_ROPE
ZG_W = 2 * GLA_HEADS * GLA_DK + W_GLA + LANES + W_GLA

IN_ROWS = 1024
FFN_ROWS = 1024
FFN_COLS = 256
MLA_Q_ROWS = 256
MLA_KV_ROWS = 512
GLA_ROWS = 256
LRU_SEGS = SUBLANES


def _dot(a, b):
    return jnp.dot(a, b, preferred_element_type=F32)


def _dot_nt(a, b):
    return lax.dot_general(a, b, (((1,), (1,)), ((), ())), preferred_element_type=F32)


def _cparams(sem):
    return pltpu.CompilerParams(dimension_semantics=sem, vmem_limit_bytes=VMEM_LIMIT_BYTES)


def _full(shape):
    return pl.BlockSpec(shape, lambda *_: (0,) * len(shape))


def _in_proj_kernel(x_ref, g_ref, w_ref, zl_ref, zm_ref, zg_ref):
    x = x_ref[...]
    ms = jnp.mean(x * x, axis=-1, keepdims=True)
    h = (x * lax.rsqrt(ms + NORM_EPS) * g_ref[...]).astype(BF16)
    zl_ref[...] = _dot(h, w_ref[:, 0:ZL_W]).astype(BF16)
    zm_ref[...] = _dot(h, w_ref[:, ZL_W:ZL_W + ZM_W]).astype(BF16)
    zg_ref[...] = _dot(h, w_ref[:, ZL_W + ZM_W:]).astype(BF16)


def _in_proj(x2, g, w):
    n, d = x2.shape
    tm = min(IN_ROWS, n)
    row = lambda i: (i, 0)
    return pl.pallas_call(
        _in_proj_kernel,
        grid=(n // tm,),
        in_specs=[pl.BlockSpec((tm, d), row), _full(g.shape), _full(w.shape)],
        out_specs=[pl.BlockSpec((tm, ZL_W), row), pl.BlockSpec((tm, ZM_W), row),
                   pl.BlockSpec((tm, ZG_W), row)],
        out_shape=[jax.ShapeDtypeStruct((n, ZL_W), BF16), jax.ShapeDtypeStruct((n, ZM_W), BF16),
                   jax.ShapeDtypeStruct((n, ZG_W), BF16)],
        compiler_params=_cparams(("arbitrary",)),
        name="in_proj",
    )(x2, g, w)


def _softplus(x):
    return jnp.maximum(x, 0.0) + jnp.log1p(jnp.exp(-jnp.abs(x)))


def _lru_kernel(z_ref, cw_ref, cb_ref, wcat_ref, bcat_ref, lam_ref, o_ref,
                u_ref, af_ref, vf_ref, ab_ref, vb_ref):
    seq = z_ref.shape[1]
    seg = seq // LRU_SEGS
    pitch = seg + SUBLANES
    halves = W_LRU // LANES

    xin = z_ref[0, :, 0:W_LRU].astype(F32)
    row = lax.broadcasted_iota(jnp.int32, (seq, W_LRU), 0)
    cw = cw_ref[...]
    u = xin * cw[CONV_LEFT:CONV_LEFT + 1]
    for j in range(CONV_W):
        off = j - CONV_LEFT
        if off == 0:
            continue
        shifted = pltpu.roll(xin, (-off) % seq, 0)
        valid = (row >= -off) if off < 0 else (row < seq - off)
        u = u + jnp.where(valid, shifted, 0.0) * cw[j:j + 1]
    u_ref[...] = u + cb_ref[...]

    sp = _softplus(-lam_ref[...])

    def gates(s, _):
        r0 = pl.multiple_of(s * seg, seg)
        p0 = pl.multiple_of(s * pitch, SUBLANES)
        uc = u_ref[pl.ds(r0, seg), :]
        g = _dot(uc.astype(BF16), wcat_ref[...]) + bcat_ref[...]
        for d, (a_ref, v_ref) in enumerate(((af_ref, vf_ref), (ab_ref, vb_ref))):
            base = d * 2 * W_LRU
            r = jax.nn.sigmoid(g[:, base:base + W_LRU])
            i = jax.nn.sigmoid(g[:, base + W_LRU:base + 2 * W_LRU])
            log_a = -LRU_C * r * sp[d:d + 1]
            a = jnp.exp(log_a)
            v = jnp.sqrt(-jnp.tanh(log_a) * (a * a + 1.0)) * (i * uc)
            for hh in range(halves):
                a_ref[hh, pl.ds(p0, seg), :] = a[:, hh * LANES:(hh + 1) * LANES]
                v_ref[hh, pl.ds(p0, seg), :] = v[:, hh * LANES:(hh + 1) * LANES]
        return 0

    lax.fori_loop(0, LRU_SEGS, gates, 0)

    def rows_at(i):
        return pl.ds(i, LRU_SEGS, stride=pitch)

    def pass1(i, carry):
        hf, pf, hb, pb = carry
        ib = seg - 1 - i
        hf_n, pf_n, hb_n, pb_n = [], [], [], []
        for hh in range(halves):
            a = af_ref[hh, rows_at(i), :]
            hf_n.append(a * hf[hh] + vf_ref[hh, rows_at(i), :])
            pf_n.append(a * pf[hh])
            a = ab_ref[hh, rows_at(ib), :]
            hb_n.append(a * hb[hh] + vb_ref[hh, rows_at(ib), :])
            pb_n.append(a * pb[hh])
        return tuple(hf_n), tuple(pf_n), tuple(hb_n), tuple(pb_n)

    zeros = tuple(jnp.zeros((LRU_SEGS, LANES), F32) for _ in range(halves))
    ones = tuple(jnp.ones((LRU_SEGS, LANES), F32) for _ in range(halves))
    hf_e, pf_e, hb_e, pb_e = lax.fori_loop(0, seg, pass1, (zeros, ones, zeros, ones))

    srow = lax.broadcasted_iota(jnp.int32, (LRU_SEGS, LANES), 0)
    cf, cb = [], []
    for hh in range(halves):
        c = jnp.zeros((LRU_SEGS, LANES), F32)
        for _ in range(LRU_SEGS - 1):
            c = jnp.where(srow == 0, 0.0, pltpu.roll(hf_e[hh] + pf_e[hh] * c, 1, 0))
        cf.append(c)
        c = jnp.zeros((LRU_SEGS, LANES), F32)
        for _ in range(LRU_SEGS - 1):
            c = jnp.where(srow == LRU_SEGS - 1, 0.0,
                          pltpu.roll(hb_e[hh] + pb_e[hh] * c, LRU_SEGS - 1, 0))
        cb.append(c)

    def pass2(i, carry):
        hf, hb = carry
        ib = seg - 1 - i
        hf_n, hb_n = [], []
        for hh in range(halves):
            h = af_ref[hh, rows_at(i), :] * hf[hh] + vf_ref[hh, rows_at(i), :]
            vf_ref[hh, rows_at(i), :] = h
            hf_n.append(h)
            h = ab_ref[hh, rows_at(ib), :] * hb[hh] + vb_ref[hh, rows_at(ib), :]
            vb_ref[hh, rows_at(ib), :] = h
            hb_n.append(h)
        return tuple(hf_n), tuple(hb_n)

    lax.fori_loop(0, seg, pass2, (tuple(cf), tuple(cb)))

    def finish(s, _):
        r0 = pl.multiple_of(s * seg, seg)
        p0 = pl.multiple_of(s * pitch, SUBLANES)
        for hh in range(halves):
            h = vf_ref[hh, pl.ds(p0, seg), :] + vb_ref[hh, pl.ds(p0, seg), :]
            gate = z_ref[0, pl.ds(r0, seg), W_LRU + hh * LANES:W_LRU + (hh + 1) * LANES].astype(F32)
            o_ref[0, pl.ds(r0, seg), hh * LANES:(hh + 1) * LANES] = (h * jax.nn.gelu(gate)).astype(BF16)
        return 0

    lax.fori_loop(0, LRU_SEGS, finish, 0)


def _lru_mixer(zl, cw, cb, wcat, bcat, lam):
    b, seq, _ = zl.shape
    pitch = seq // LRU_SEGS + SUBLANES
    scan = pltpu.VMEM((W_LRU // LANES, LRU_SEGS * pitch, LANES), F32)
    return pl.pallas_call(
        _lru_kernel,
        grid=(b,),
        in_specs=[pl.BlockSpec((1, seq, ZL_W), lambda i: (i, 0, 0)), _full(cw.shape), _full(cb.shape),
                  _full(wcat.shape), _full(bcat.shape), _full(lam.shape)],
        out_specs=pl.BlockSpec((1, seq, W_LRU), lambda i: (i, 0, 0)),
        out_shape=jax.ShapeDtypeStruct((b, seq, W_LRU), BF16),
        scratch_shapes=[pltpu.VMEM((seq, W_LRU), F32), scan, scan, scan, scan],
        compiler_params=_cparams(("arbitrary",)),
        name="lru_mixer",
    )(zl, cw, cb, wcat, bcat, lam)


def _rms(x, g):
    ms = jnp.mean(x * x, axis=-1, keepdims=True)
    return x * lax.rsqrt(ms + NORM_EPS) * g


def _rope_part(t, cos, sin, gains):
    return t * cos * gains[1:2] + pltpu.roll(t, QK_ROPE, 1) * sin * gains[2:3]


def _mla_kernel(z_ref, cos_ref, sin_ref, gcq_ref, wq_ref, gckv_ref, wkv_ref, gq_ref, gk_ref,
                o_ref, k_scr, v_scr):
    seq = z_ref.shape[1]
    qi = pl.program_id(1)
    c_kv0 = Q_RANK
    c_rope0 = Q_RANK + KV_RANK
    head_w = QK_NOPE + 2 * QK_ROPE

    @pl.when(qi == 0)
    def _():
        gk = gk_ref[...]
        for c in range(seq // MLA_KV_ROWS):
            rows = slice(c * MLA_KV_ROWS, (c + 1) * MLA_KV_ROWS)
            ckv = _rms(z_ref[0, rows, c_kv0:c_rope0].astype(F32), gckv_ref[...])
            kv = _dot(ckv.astype(BF16), wkv_ref[...])
            t = z_ref[0, rows, c_rope0:c_rope0 + 2 * QK_ROPE].astype(F32)
            t_ss = 0.5 * jnp.sum(t * t, axis=-1, keepdims=True)
            rope = _rope_part(t, cos_ref[rows, :], sin_ref[rows, :], gk)
            for h in range(MLA_HEADS):
                kn = kv[:, h * QK_NOPE:(h + 1) * QK_NOPE]
                ss = jnp.sum(kn * kn, axis=-1, keepdims=True) + t_ss
                inv = lax.rsqrt(ss * (1.0 / QK_HEAD) + NORM_EPS)
                k_scr[h, rows, 0:QK_NOPE] = (kn * inv * gk[0:1]).astype(BF16)
                k_scr[h, rows, QK_NOPE:head_w] = (rope * inv).astype(BF16)
                v0 = MLA_HEADS * QK_NOPE + h * V_HEAD
                v_scr[h, rows, :] = kv[:, v0:v0 + V_HEAD].astype(BF16)

    r0 = pl.multiple_of(qi * MLA_Q_ROWS, MLA_Q_ROWS)
    qrows = pl.ds(r0, MLA_Q_ROWS)
    cq = _rms(z_ref[0, qrows, 0:Q_RANK].astype(F32), gcq_ref[...])
    qall = _dot(cq.astype(BF16), wq_ref[...])
    cos = cos_ref[qrows, :]
    sin = sin_ref[qrows, :]
    gq = gq_ref[...]
    scale = QK_HEAD ** -0.5
    for h in range(MLA_HEADS):
        qn = qall[:, h * head_w:h * head_w + QK_NOPE]
        t = qall[:, h * head_w + QK_NOPE:(h + 1) * head_w]
        ss = jnp.sum(qn * qn, axis=-1, keepdims=True) + 0.5 * jnp.sum(t * t, axis=-1, keepdims=True)
        inv = lax.rsqrt(ss * (1.0 / QK_HEAD) + NORM_EPS) * scale
        qh = jnp.concatenate([qn * inv * gq[0:1], _rope_part(t, cos, sin, gq) * inv], axis=-1)
        s = _dot_nt(qh.astype(BF16), k_scr[h])
        m = jnp.max(s, axis=-1, keepdims=True)
        e = jnp.exp(s - m)
        l = jnp.sum(e, axis=-1, keepdims=True)
        o = _dot(e.astype(BF16), v_scr[h]) / l
        o_ref[0, :, h * V_HEAD:(h + 1) * V_HEAD] = o.astype(BF16)


def _mla_mixer(zm, cos, sin, gcq, wq, gckv, wkv, gq, gk):
    b, seq, _ = zm.shape
    head_w = QK_NOPE + 2 * QK_ROPE
    return pl.pallas_call(
        _mla_kernel,
        grid=(b, seq // MLA_Q_ROWS),
        in_specs=[pl.BlockSpec((1, seq, ZM_W), lambda i, j: (i, 0, 0)), _full(cos.shape), _full(sin.shape),
                  _full(gcq.shape), _full(wq.shape), _full(gckv.shape), _full(wkv.shape),
                  _full(gq.shape), _full(gk.shape)],
        out_specs=pl.BlockSpec((1, MLA_Q_ROWS, W_MLA), lambda i, j: (i, j, 0)),
        out_shape=jax.ShapeDtypeStruct((b, seq, W_MLA), BF16),
        scratch_shapes=[pltpu.VMEM((MLA_HEADS, seq, head_w), BF16),
                        pltpu.VMEM((MLA_HEADS, seq, V_HEAD), BF16)],
        compiler_params=_cparams(("arbitrary", "arbitrary")),
        name="mla_mixer",
    )(zm, cos, sin, gcq, wq, gckv, wkv, gq, gk)


def _split_bf16(x):
    hi = x.astype(BF16)
    lo = (x - hi.astype(F32)).astype(BF16)
    return hi, lo


def _log_sigmoid(x):
    return -_softplus(-x)


def _gla_kernel(z_ref, wg_ref, bg_ref, go_ref, o_ref, la_scr, of_scr, st_scr):
    seq = z_ref.shape[1]
    n_groups = seq // GLA_ROWS
    chunks = GLA_ROWS // GLA_CHUNK
    qk_w = GLA_HEADS * GLA_DK
    k0, v0, g0, og0 = qk_w, 2 * qk_w, 2 * qk_w + W_GLA, 2 * qk_w + W_GLA + LANES

    def gate_rows(g, _):
        rows = pl.ds(pl.multiple_of(g * GLA_ROWS, GLA_ROWS), GLA_ROWS)
        x = _dot(z_ref[0, rows, g0:g0 + LANES], wg_ref[...]) + bg_ref[...]
        la_scr[rows, :] = _log_sigmoid(x) * (1.0 / GLA_TAU)
        return 0

    lax.fori_loop(0, n_groups, gate_rows, 0)

    ri = lax.broadcasted_iota(jnp.int32, (GLA_ROWS, GLA_ROWS), 0)
    ci = lax.broadcasted_iota(jnp.int32, (GLA_ROWS, GLA_ROWS), 1)
    same_chunk = (ri // GLA_CHUNK) == (ci // GLA_CHUNK)
    causal = (jnp.where(same_chunk, ci, GLA_ROWS) <= ri, jnp.where(same_chunk, ci, -1) >= ri)
    cum_mat = tuple(jnp.where(m, 1.0, 0.0).astype(BF16) for m in causal)
    tot_mat = jnp.where(same_chunk, 1.0, 0.0).astype(BF16)
    avg_mat = jnp.where((ri // GLA_DV) == (ci // GLA_DV), 1.0 / GLA_DV, 0.0).astype(BF16)
    qk_head = lax.broadcasted_iota(jnp.int32, (GLA_ROWS, qk_w), 1) // GLA_DK
    v_head = lax.broadcasted_iota(jnp.int32, (GLA_ROWS, W_GLA), 1) // GLA_DV
    st_rows = lax.broadcasted_iota(jnp.int32, (W_GLA, qk_w), 0) // GLA_DV
    st_cols = lax.broadcasted_iota(jnp.int32, (W_GLA, qk_w), 1) // GLA_DK
    st_mask = st_rows == st_cols
    col_chunk = ci // GLA_CHUNK

    def group(g, d):
        rows = pl.ds(pl.multiple_of(g * GLA_ROWS, GLA_ROWS), GLA_ROWS)
        la_hi, la_lo = _split_bf16(la_scr[rows, d * qk_w:(d + 1) * qk_w])
        b = _dot(cum_mat[d], la_hi) + _dot(cum_mat[d], la_lo)
        bt = _dot(tot_mat, la_hi) + _dot(tot_mat, la_lo)
        q = z_ref[0, rows, 0:qk_w].astype(F32) * (GLA_DK ** -0.5)
        k = z_ref[0, rows, k0:k0 + qk_w].astype(F32)
        v = z_ref[0, rows, v0:v0 + W_GLA]
        q_dec = (q * jnp.exp(b)).astype(BF16)
        k_inv = (k * jnp.exp(-b)).astype(BF16)
        k_end = (k * jnp.exp(bt - b)).astype(BF16)
        decay = jnp.exp(bt)

        probs, vals = [], []
        for h in range(GLA_HEADS):
            s = _dot_nt(jnp.where(qk_head == h, q_dec, jnp.zeros_like(q_dec)), k_inv)
            probs.append(jnp.where(causal[d], s, 0.0).astype(BF16))
            vals.append(jnp.where(v_head == h, v, jnp.zeros_like(v)))
        o = _dot(jnp.concatenate(probs, axis=1), jnp.concatenate(vals, axis=0))

        v_t = v.astype(F32).T
        st = st_scr[...]
        inter = [None] * chunks
        for c in (range(chunks) if d == 0 else reversed(range(chunks))):
            crow = slice(c * GLA_CHUNK, (c + 1) * GLA_CHUNK)
            inter[c] = _dot_nt(q_dec[crow], st.astype(BF16))
            upd = _dot(jnp.where(col_chunk == c, v_t, 0.0).astype(BF16), k_end)
            st = st * decay[c * GLA_CHUNK:c * GLA_CHUNK + 1] + jnp.where(st_mask, upd, 0.0)
        st_scr[...] = st
        return rows, o + jnp.concatenate(inter, axis=0)

    def fwd(g, _):
        rows, o = group(g, 0)
        of_scr[rows, :] = o
        return 0

    def bwd(i, _):
        rows, o = group(n_groups - 1 - i, 1)
        o = o + of_scr[rows, :]
        sq_hi, sq_lo = _split_bf16(o * o)
        ms = _dot(sq_hi, avg_mat) + _dot(sq_lo, avg_mat)
        og = z_ref[0, rows, og0:og0 + W_GLA].astype(F32)
        y = (o * lax.rsqrt(ms + NORM_EPS) * go_ref[...]) * (og * jax.nn.sigmoid(og))
        o_ref[0, rows, :] = y.astype(BF16)
        return 0

    st_scr[...] = jnp.zeros_like(st_scr)
    lax.fori_loop(0, n_groups, fwd, 0)
    st_scr[...] = jnp.zeros_like(st_scr)
    lax.fori_loop(0, n_groups, bwd, 0)


def _gla_mixer(zg, wg, bg, go):
    b, seq, _ = zg.shape
    qk_w = GLA_HEADS * GLA_DK
    return pl.pallas_call(
        _gla_kernel,
        grid=(b,),
        in_specs=[pl.BlockSpec((1, seq, ZG_W), lambda i: (i, 0, 0)), _full(wg.shape), _full(bg.shape),
                  _full(go.shape)],
        out_specs=pl.BlockSpec((1, seq, W_GLA), lambda i: (i, 0, 0)),
        out_shape=jax.ShapeDtypeStruct((b, seq, W_GLA), BF16),
        scratch_shapes=[pltpu.VMEM((seq, 2 * qk_w), F32), pltpu.VMEM((seq, W_GLA), F32),
                        pltpu.VMEM((W_GLA, qk_w), F32)],
        compiler_params=_cparams(("arbitrary",)),
        name="gla_mixer",
    )(zg, wg, bg, go)


def _out_ffn_kernel(x_ref, ol_ref, om_ref, og_ref, wol_ref, wom_ref, wog_ref, g_ref,
                    wgate_ref, wup_ref, wdown_ref, o_ref, h_ref):
    @pl.when(pl.program_id(1) == 0)
    def _():
        x1 = (x_ref[...] + _dot(ol_ref[...], wol_ref[...]) + _dot(om_ref[...], wom_ref[...])
              + _dot(og_ref[...], wog_ref[...]))
        o_ref[...] = x1
        h_ref[...] = _rms(x1, g_ref[...]).astype(BF16)

    h = h_ref[...]
    gate = _dot(h, wgate_ref[...])
    up = _dot(h, wup_ref[...])
    act = (gate * jax.nn.sigmoid(gate) * up).astype(BF16)
    o_ref[...] += _dot(act, wdown_ref[...])


def _out_ffn(x2, ol, om, og, wol, wom, wog, g, w_in, w_down):
    n, d = x2.shape
    d_ff = w_down.shape[0]
    tm = min(FFN_ROWS, n)
    n_f = d_ff // FFN_COLS
    row = lambda i, j: (i, 0)
    return pl.pallas_call(
        _out_ffn_kernel,
        grid=(n // tm, n_f),
        in_specs=[pl.BlockSpec((tm, d), row), pl.BlockSpec((tm, W_LRU), row),
                  pl.BlockSpec((tm, W_MLA), row), pl.BlockSpec((tm, W_GLA), row),
                  _full(wol.shape), _full(wom.shape), _full(wog.shape), _full(g.shape),
                  pl.BlockSpec((d, FFN_COLS), lambda i, j: (0, j)),
                  pl.BlockSpec((d, FFN_COLS), lambda i, j: (0, j + n_f)),
                  pl.BlockSpec((FFN_COLS, d), lambda i, j: (j, 0))],
        out_specs=pl.BlockSpec((tm, d), row),
        out_shape=jax.ShapeDtypeStruct((n, d), F32),
        scratch_shapes=[pltpu.VMEM((tm, d), BF16)],
        compiler_params=_cparams(("arbitrary", "arbitrary")),
        name="out_ffn",
    )(x2, ol, om, og, wol, wom, wog, g, w_in, w_in, w_down)


def _block_diag(w):
    nb, bi, bo = w.shape
    eye = jnp.eye(nb, dtype=w.dtype)
    return (w[:, :, None, :] * eye[:, None, :, None]).reshape(nb * bi, nb * bo)


def _swap_halves(n):
    return np.concatenate([np.arange(n // 2, n), np.arange(0, n // 2)])


def _prep_layer(p):
    row = lambda v: v.reshape(1, -1).astype(F32)
    d_model = p["w_in"].shape[0]
    qk_w = GLA_HEADS * GLA_DK
    o_cq = 2 * W_LRU
    o_ckv = o_cq + Q_RANK
    o_rope = o_ckv + KV_RANK
    o_glaq = o_rope + QK_ROPE
    o_gf = o_glaq + 2 * qk_w + W_GLA
    o_og = o_gf + 2 * GLA_GATE_RANK
    sw = _swap_halves(QK_ROPE)
    w = p["w_in"]
    w_in = jnp.concatenate([
        w[:, :o_glaq], w[:, o_rope + sw],
        w[:, o_glaq:o_og], jnp.zeros((d_model, LANES - 2 * GLA_GATE_RANK), w.dtype), w[:, o_og:],
    ], axis=1).astype(BF16)

    wcat = jnp.concatenate([_block_diag(p[k]) for k in ("lru_wa_f", "lru_wx_f", "lru_wa_b", "lru_wx_b")],
                           axis=1).astype(BF16)
    bcat = jnp.concatenate([p[k] for k in ("lru_ba_f", "lru_bx_f", "lru_ba_b", "lru_bx_b")]).reshape(1, -1)
    lam = jnp.stack([p["lru_lam_f"], p["lru_lam_b"]])

    wq = p["mla_w_uq"].reshape(Q_RANK, MLA_HEADS, QK_HEAD)
    wq = jnp.concatenate([wq, wq[:, :, QK_NOPE + sw]], axis=2).reshape(Q_RANK, -1).astype(BF16)
    wkv = p["mla_w_ukv"].reshape(KV_RANK, MLA_HEADS, QK_NOPE + V_HEAD)
    wkv = jnp.concatenate([wkv[:, :, :QK_NOPE].reshape(KV_RANK, -1),
                           wkv[:, :, QK_NOPE:].reshape(KV_RANK, -1)], axis=1).astype(BF16)

    def qk_gains(g):
        pad = jnp.zeros((LANES - QK_ROPE,), F32)
        return jnp.stack([g[:QK_NOPE], jnp.concatenate([g[QK_NOPE:], pad]),
                          jnp.concatenate([g[QK_NOPE + sw], pad])]).astype(F32)

    wg = jnp.zeros((LANES, 2 * qk_w), F32)
    wg = wg.at[:GLA_GATE_RANK, :qk_w].set(p["gla_wa2_f"])
    wg = wg.at[GLA_GATE_RANK:2 * GLA_GATE_RANK, qk_w:].set(p["gla_wa2_b"])
    bg = jnp.concatenate([p["gla_ba2_f"], p["gla_ba2_b"]]).reshape(1, -1)

    w_out = p["w_out"].astype(BF16)
    return dict(
        g_mix=row(p["g_mix"]), w_in=w_in,
        conv_w=p["conv_w"], conv_b=row(p["conv_b"]), wcat=wcat, bcat=bcat, lam=lam,
        gcq=row(p["mla_g_cq"]), wq=wq, gckv=row(p["mla_g_ckv"]), wkv=wkv,
        gq=qk_gains(p["mla_g_q"]), gk=qk_gains(p["mla_g_k"]),
        wg=wg.astype(BF16), bg=bg, go=row(jnp.tile(p["gla_g_o"], GLA_HEADS)),
        wol=w_out[:W_LRU], wom=w_out[W_LRU:W_LRU + W_MLA], wog=w_out[W_LRU + W_MLA:],
        g_ffn=row(p["g_ffn"]), w_ffn_in=p["w_ffn_in"].astype(BF16), w_ffn_out=p["w_ffn_out"].astype(BF16),
    )


def _rotary_tables(seq):
    half = QK_ROPE // 2
    inv = 1.0 / (ROPE_THETA ** (jnp.arange(half, dtype=F32) * 2.0 / QK_ROPE))
    ang = jnp.arange(seq, dtype=F32)[:, None] * inv[None, :]
    cos, sin = jnp.cos(ang), jnp.sin(ang)
    pad = jnp.zeros((seq, LANES - QK_ROPE), F32)
    return jnp.concatenate([cos, cos, pad], axis=1), jnp.concatenate([-sin, sin, pad], axis=1)


def _layer(x, cos, sin, lp):
    b, seq, d = x.shape
    x2 = x.reshape(b * seq, d)
    zl, zm, zg = _in_proj(x2, lp["g_mix"], lp["w_in"])
    o_lru = _lru_mixer(zl.reshape(b, seq, ZL_W), lp["conv_w"], lp["conv_b"], lp["wcat"], lp["bcat"], lp["lam"])
    o_mla = _mla_mixer(zm.reshape(b, seq, ZM_W), cos, sin, lp["gcq"], lp["wq"], lp["gckv"], lp["wkv"],
                       lp["gq"], lp["gk"])
    o_gla = _gla_mixer(zg.reshape(b, seq, ZG_W), lp["wg"], lp["bg"], lp["go"])
    y = _out_ffn(x2, o_lru.reshape(b * seq, W_LRU), o_mla.reshape(b * seq, W_MLA),
                 o_gla.reshape(b * seq, W_GLA), lp["wol"], lp["wom"], lp["wog"], lp["g_ffn"],
                 lp["w_ffn_in"], lp["w_ffn_out"])
    return y.reshape(b, seq, d)


_PARAM_NAMES = ("g_mix", "w_in", "conv_w", "conv_b",
                "lru_wa_f", "lru_ba_f", "lru_wx_f", "lru_bx_f", "lru_lam_f",
                "lru_wa_b", "lru_ba_b", "lru_wx_b", "lru_bx_b", "lru_lam_b",
                "mla_g_cq", "mla_w_uq", "mla_g_ckv", "mla_w_ukv", "mla_g_q", "mla_g_k",
                "gla_wa2_f", "gla_ba2_f", "gla_wa2_b", "gla_ba2_b", "gla_g_o",
                "w_out", "g_ffn", "w_ffn_in", "w_ffn_out")


def kernel(x_prompt, x_sample, g_mix, w_in, conv_w, conv_b, lru_wa_f, lru_ba_f, lru_wx_f, lru_bx_f, lru_lam_f, lru_wa_b, lru_ba_b, lru_wx_b, lru_bx_b, lru_lam_b, mla_g_cq, mla_w_uq, mla_g_ckv, mla_w_ukv, mla_g_q, mla_g_k, gla_wa2_f, gla_ba2_f, gla_wa2_b, gla_ba2_b, gla_g_o, w_out, g_ffn, w_ffn_in, w_ffn_out):
    stacked = dict(zip(_PARAM_NAMES, (
        g_mix, w_in, conv_w, conv_b, lru_wa_f, lru_ba_f, lru_wx_f, lru_bx_f, lru_lam_f,
        lru_wa_b, lru_ba_b, lru_wx_b, lru_bx_b, lru_lam_b,
        mla_g_cq, mla_w_uq, mla_g_ckv, mla_w_ukv, mla_g_q, mla_g_k,
        gla_wa2_f, gla_ba2_f, gla_wa2_b, gla_ba2_b, gla_g_o, w_out, g_ffn, w_ffn_in, w_ffn_out)))
    depth = g_mix.shape[0]
    layers = [_prep_layer({k: v[l] for k, v in stacked.items()}) for l in range(depth)]

    def trunk(x):
        cos, sin = _rotary_tables(x.shape[1])
        for lp in layers:
            x = _layer(x, cos, sin, lp)
        return x

    return trunk(x_prompt), trunk(x_sample)
```

```python
import functools

import jax
import jax.numpy as jnp
import numpy as np
from jax import lax
from jax.experimental import pallas as pl
from jax.experimental.pallas import tpu as pltpu

F32 = jnp.float32
BF16 = jnp.bfloat16

NORM_EPS = 1e-6
W_LRU = 256
W_MLA = 512
W_GLA = 256
LRU_BLOCKS = 8
LRU_C = 8.0
CONV_W = 4
CONV_LEFT = 2
MLA_HEADS = 4
QK_NOPE = 128
QK_ROPE = 64
QK_HEAD = QK_NOPE + QK_ROPE
V_HEAD = 128
Q_RANK = 256
KV_RANK = 128
ROPE_THETA = 10000.0
GLA_HEADS = 4
GLA_DK = 32
GLA_DV = 64
GLA_GATE_RANK = 16
GLA_TAU = 16.0
GLA_CHUNK = 64

LANES = 128
SUBLANES = 8
VMEM_LIMIT_BYTES = 56 * 1024 * 1024

ZL_W = 2 * W_LRU
ZM_W = Q_RANK + KV_RANK + 2 * QK_ROPE
ZG_W = 2 * GLA_HEADS * GLA_DK + W_GLA + LANES + W_GLA

IN_ROWS = 1024
FFN_ROWS = 1024
FFN_COLS = 256
MLA_Q_ROWS = 512
MLA_KV_ROWS = 512
MLA_KEY_BLOCK = 512
MLA_SAFE_SHIFT = 60.0
GLA_ROWS = 256
LRU_SEGS = SUBLANES


def _dot(a, b):
    return jnp.dot(a, b, preferred_element_type=F32)


def _dot_nt(a, b):
    return lax.dot_general(a, b, (((1,), (1,)), ((), ())), preferred_element_type=F32)


def _cparams(sem):
    return pltpu.CompilerParams(dimension_semantics=sem, vmem_limit_bytes=VMEM_LIMIT_BYTES)


def _full(shape):
    return pl.BlockSpec(shape, lambda *_: (0,) * len(shape))


def _in_proj_kernel(x_ref, g_ref, w_ref, zl_ref, zm_ref, zg_ref):
    x = x_ref[...]
    ms = jnp.mean(x * x, axis=-1, keepdims=True)
    h = (x * lax.rsqrt(ms + NORM_EPS) * g_ref[...]).astype(BF16)
    zl_ref[...] = _dot(h, w_ref[:, 0:ZL_W]).astype(BF16)
    zm_ref[...] = _dot(h, w_ref[:, ZL_W:ZL_W + ZM_W]).astype(BF16)
    zg_ref[...] = _dot(h, w_ref[:, ZL_W + ZM_W:]).astype(BF16)


def _in_proj(x2, g, w):
    n, d = x2.shape
    tm = min(IN_ROWS, n)
    row = lambda i: (i, 0)
    return pl.pallas_call(
        _in_proj_kernel,
        grid=(n // tm,),
        in_specs=[pl.BlockSpec((tm, d), row), _full(g.shape), _full(w.shape)],
        out_specs=[pl.BlockSpec((tm, ZL_W), row), pl.BlockSpec((tm, ZM_W), row),
                   pl.BlockSpec((tm, ZG_W), row)],
        out_shape=[jax.ShapeDtypeStruct((n, ZL_W), BF16), jax.ShapeDtypeStruct((n, ZM_W), BF16),
                   jax.ShapeDtypeStruct((n, ZG_W), BF16)],
        compiler_params=_cparams(("arbitrary",)),
        name="in_proj",
    )(x2, g, w)


def _softplus(x):
    return jnp.maximum(x, 0.0) + jnp.log1p(jnp.exp(-jnp.abs(x)))


def _lru_kernel(z_ref, cw_ref, cb_ref, wcat_ref, bcat_ref, lam_ref, o_ref,
                u_ref, af_ref, vf_ref, ab_ref, vb_ref, c_ref):
    seq = z_ref.shape[1]
    seg = seq // LRU_SEGS
    pitch = seg + SUBLANES
    halves = W_LRU // LANES

    xin = z_ref[0, :, 0:W_LRU].astype(F32)
    row = lax.broadcasted_iota(jnp.int32, (seq, W_LRU), 0)
    cw = cw_ref[...]
    u = xin * cw[CONV_LEFT:CONV_LEFT + 1]
    for j in range(CONV_W):
        off = j - CONV_LEFT
        if off == 0:
            continue
        shifted = pltpu.roll(xin, (-off) % seq, 0)
        valid = (row >= -off) if off < 0 else (row < seq - off)
        u = u + jnp.where(valid, shifted, 0.0) * cw[j:j + 1]
    u_ref[...] = u + cb_ref[...]

    sp = _softplus(-lam_ref[...])

    def gates(s, _):
        r0 = pl.multiple_of(s * seg, seg)
        p0 = pl.multiple_of(s * pitch, SUBLANES)
        uc = u_ref[pl.ds(r0, seg), :]
        g = _dot(uc.astype(BF16), wcat_ref[...]) + bcat_ref[...]
        for d, (a_ref, v_ref) in enumerate(((af_ref, vf_ref), (ab_ref, vb_ref))):
            base = d * 2 * W_LRU
            r = jax.nn.sigmoid(g[:, base:base + W_LRU])
            i = jax.nn.sigmoid(g[:, base + W_LRU:base + 2 * W_LRU])
            log_a = -LRU_C * r * sp[d:d + 1]
            a = jnp.exp(log_a)
            v = jnp.sqrt(-jnp.tanh(log_a) * (a * a + 1.0)) * (i * uc)
            for hh in range(halves):
                a_ref[hh, pl.ds(p0, seg), :] = a[:, hh * LANES:(hh + 1) * LANES]
                v_ref[hh, pl.ds(p0, seg), :] = v[:, hh * LANES:(hh + 1) * LANES]
        return 0

    lax.fori_loop(0, LRU_SEGS, gates, 0)

    def rows_at(i):
        return pl.ds(i, LRU_SEGS, stride=pitch)

    def scan(i, carry):
        hf, pf, hb, pb = carry
        ib = seg - 1 - i
        hf_n, pf_n, hb_n, pb_n = [], [], [], []
        for hh in range(halves):
            a = af_ref[hh, rows_at(i), :]
            h = a * hf[hh] + vf_ref[hh, rows_at(i), :]
            p = a * pf[hh]
            vf_ref[hh, rows_at(i), :] = h
            af_ref[hh, rows_at(i), :] = p
            hf_n.append(h)
            pf_n.append(p)
            a = ab_ref[hh, rows_at(ib), :]
            h = a * hb[hh] + vb_ref[hh, rows_at(ib), :]
            p = a * pb[hh]
            vb_ref[hh, rows_at(ib), :] = h
            ab_ref[hh, rows_at(ib), :] = p
            hb_n.append(h)
            pb_n.append(p)
        return tuple(hf_n), tuple(pf_n), tuple(hb_n), tuple(pb_n)

    zeros = tuple(jnp.zeros((LRU_SEGS, LANES), F32) for _ in range(halves))
    ones = tuple(jnp.ones((LRU_SEGS, LANES), F32) for _ in range(halves))
    hf_e, pf_e, hb_e, pb_e = lax.fori_loop(0, seg, scan, (zeros, ones, zeros, ones), unroll=2)

    srow = lax.broadcasted_iota(jnp.int32, (LRU_SEGS, LANES), 0)
    for hh in range(halves):
        c = jnp.zeros((LRU_SEGS, LANES), F32)
        for _ in range(LRU_SEGS - 1):
            c = jnp.where(srow == 0, 0.0, pltpu.roll(hf_e[hh] + pf_e[hh] * c, 1, 0))
        c_ref[0, hh] = c
        c = jnp.zeros((LRU_SEGS, LANES), F32)
        for _ in range(LRU_SEGS - 1):
            c = jnp.where(srow == LRU_SEGS - 1, 0.0,
                          pltpu.roll(hb_e[hh] + pb_e[hh] * c, LRU_SEGS - 1, 0))
        c_ref[1, hh] = c

    def finish(s, _):
        r0 = pl.multiple_of(s * seg, seg)
        p0 = pl.multiple_of(s * pitch, SUBLANES)
        for hh in range(halves):
            h = (vf_ref[hh, pl.ds(p0, seg), :] + af_ref[hh, pl.ds(p0, seg), :] * c_ref[0, hh, pl.ds(s, 1), :]
                 + vb_ref[hh, pl.ds(p0, seg), :] + ab_ref[hh, pl.ds(p0, seg), :] * c_ref[1, hh, pl.ds(s, 1), :])
            gate = z_ref[0, pl.ds(r0, seg), W_LRU + hh * LANES:W_LRU + (hh + 1) * LANES].astype(F32)
            o_ref[0, pl.ds(r0, seg), hh * LANES:(hh + 1) * LANES] = (h * jax.nn.gelu(gate)).astype(BF16)
        return 0

    lax.fori_loop(0, LRU_SEGS, finish, 0)


def _lru_mixer(zl, cw, cb, wcat, bcat, lam):
    b, seq, _ = zl.shape
    pitch = seq // LRU_SEGS + SUBLANES
    scan = pltpu.VMEM((W_LRU // LANES, LRU_SEGS * pitch, LANES), F32)
    return pl.pallas_call(
        _lru_kernel,
        grid=(b,),
        in_specs=[pl.BlockSpec((1, seq, ZL_W), lambda i: (i, 0, 0)), _full(cw.shape), _full(cb.shape),
                  _full(wcat.shape), _full(bcat.shape), _full(lam.shape)],
        out_specs=pl.BlockSpec((1, seq, W_LRU), lambda i: (i, 0, 0)),
        out_shape=jax.ShapeDtypeStruct((b, seq, W_LRU), BF16),
        scratch_shapes=[pltpu.VMEM((seq, W_LRU), F32), scan, scan, scan, scan,
                        pltpu.VMEM((2, W_LRU // LANES, LRU_SEGS, LANES), F32)],
        compiler_params=_cparams(("arbitrary",)),
        name="lru_mixer",
    )(zl, cw, cb, wcat, bcat, lam)


def _rms(x, g):
    ms = jnp.mean(x * x, axis=-1, keepdims=True)
    return x * lax.rsqrt(ms + NORM_EPS) * g


def _split_bf16(x):
    hi = x.astype(BF16)
    lo = (x - hi.astype(F32)).astype(BF16)
    return hi, lo


def _group_sums(x, ones):
    hi, lo = _split_bf16(x)
    return _dot(hi, ones) + _dot(lo, ones)


def _block_ones(n, block):
    ri = lax.broadcasted_iota(jnp.int32, (n, n), 0) // block
    ci = lax.broadcasted_iota(jnp.int32, (n, n), 1) // block
    return jnp.where(ri == ci, 1.0, 0.0).astype(BF16)


def _rope_part(t, cos, sin, gains):
    return t * cos * gains[1:2] + pltpu.roll(t, QK_ROPE, 1) * sin * gains[2:3]


def _mla_kernel(z_ref, cos_ref, sin_ref, gcq_ref, wq_ref, gckv_ref, wkv_ref, gq_ref, gk_ref,
                o_ref, k_scr, v_scr):
    seq = z_ref.shape[1]
    qi = pl.program_id(1)
    c_kv0 = Q_RANK
    c_rope0 = Q_RANK + KV_RANK
    head_w = QK_NOPE + 2 * QK_ROPE
    ones_pair = _block_ones(2 * LANES, LANES)

    @pl.when(qi == 0)
    def _():
        gk = gk_ref[...]
        for c in range(seq // MLA_KV_ROWS):
            rows = slice(c * MLA_KV_ROWS, (c + 1) * MLA_KV_ROWS)
            ckv = _rms(z_ref[0, rows, c_kv0:c_rope0].astype(F32), gckv_ref[...])
            kv = _dot(ckv.astype(BF16), wkv_ref[...])
            t = z_ref[0, rows, c_rope0:c_rope0 + 2 * QK_ROPE].astype(F32)
            t_sq = 0.5 * t * t
            rope = _rope_part(t, cos_ref[rows, :], sin_ref[rows, :], gk)
            for h in range(MLA_HEADS):
                kn = kv[:, h * QK_NOPE:(h + 1) * QK_NOPE]
                ss = jnp.sum(kn * kn + t_sq, axis=-1, keepdims=True)
                inv = lax.rsqrt(ss * (1.0 / QK_HEAD) + NORM_EPS)
                k_scr[h, rows, 0:QK_NOPE] = (kn * inv * gk[0:1]).astype(BF16)
                k_scr[h, rows, QK_NOPE:head_w] = (rope * inv).astype(BF16)
                v0 = MLA_HEADS * QK_NOPE + h * V_HEAD
                v_scr[h, rows, 0:V_HEAD] = kv[:, v0:v0 + V_HEAD].astype(BF16)
                v_scr[h, rows, V_HEAD:2 * V_HEAD] = jnp.ones((MLA_KV_ROWS, V_HEAD), BF16)

    r0 = pl.multiple_of(qi * MLA_Q_ROWS, MLA_Q_ROWS)
    qrows = pl.ds(r0, MLA_Q_ROWS)
    cq = z_ref[0, qrows, 0:Q_RANK].astype(F32)
    ms = _group_sums(cq * cq, jnp.ones((Q_RANK, Q_RANK), BF16)) * (1.0 / Q_RANK)
    cq = cq * lax.rsqrt(ms + NORM_EPS) * gcq_ref[...]
    qall = _dot(cq.astype(BF16), wq_ref[...])
    cos = cos_ref[qrows, :]
    sin = sin_ref[qrows, :]
    gq = gq_ref[...]
    scale = QK_HEAD ** -0.5 * np.log2(np.e)
    q_heads = []
    sq = lambda h: (qall[:, h * head_w:h * head_w + QK_NOPE] ** 2
                    + 0.5 * qall[:, h * head_w + QK_NOPE:(h + 1) * head_w] ** 2)
    for h in range(MLA_HEADS):
        qn = qall[:, h * head_w:h * head_w + QK_NOPE]
        t = qall[:, h * head_w + QK_NOPE:(h + 1) * head_w]
        if h % 2 == 0:
            ss2 = _group_sums(jnp.concatenate([sq(h), sq(h + 1)], axis=-1), ones_pair)
        ss = ss2[:, (h % 2) * LANES:(h % 2 + 1) * LANES]
        inv = lax.rsqrt(ss * (1.0 / QK_HEAD) + NORM_EPS) * scale
        qh = jnp.concatenate([qn * inv * gq[0:1], _rope_part(t, cos, sin, gq) * inv], axis=-1)
        q_heads.append(qh.astype(BF16))

    def write_head(h, ov):
        o_ref[0, :, h * V_HEAD:(h + 1) * V_HEAD] = (ov[:, 0:V_HEAD] / ov[:, V_HEAD:]).astype(BF16)

    g_max = lambda g: jnp.max(jnp.abs(g[0:2]))
    bound = scale * QK_HEAD * g_max(gq) * g_max(gk_ref[...])
    shift_is_safe = bound <= MLA_SAFE_SHIFT

    @pl.when(shift_is_safe)
    def _():
        for h in range(MLA_HEADS):
            ov = jnp.zeros((MLA_Q_ROWS, 2 * V_HEAD), F32)
            for kb in range(seq // MLA_KEY_BLOCK):
                keys = slice(kb * MLA_KEY_BLOCK, (kb + 1) * MLA_KEY_BLOCK)
                s = _dot_nt(q_heads[h], k_scr[h, keys, :])
                ov = ov + _dot(jnp.exp2(s - bound).astype(BF16), v_scr[h, keys, :])
            write_head(h, ov)

    @pl.when(jnp.logical_not(shift_is_safe))
    def _():
        for h in range(MLA_HEADS):
            s = _dot_nt(q_heads[h], k_scr[h])
            m = jnp.max(s, axis=-1, keepdims=True)
            write_head(h, _dot(jnp.exp2(s - m).astype(BF16), v_scr[h]))


def _mla_mixer(zm, cos, sin, gcq, wq, gckv, wkv, gq, gk):
    b, seq, _ = zm.shape
    head_w = QK_NOPE + 2 * QK_ROPE
    return pl.pallas_call(
        _mla_kernel,
        grid=(b, seq // MLA_Q_ROWS),
        in_specs=[pl.BlockSpec((1, seq, ZM_W), lambda i, j: (i, 0, 0)), _full(cos.shape), _full(sin.shape),
                  _full(gcq.shape), _full(wq.shape), _full(gckv.shape), _full(wkv.shape),
                  _full(gq.shape), _full(gk.shape)],
        out_specs=pl.BlockSpec((1, MLA_Q_ROWS, W_MLA), lambda i, j: (i, j, 0)),
        out_shape=jax.ShapeDtypeStruct((b, seq, W_MLA), BF16),
        scratch_shapes=[pltpu.VMEM((MLA_HEADS, seq, head_w), BF16),
                        pltpu.VMEM((MLA_HEADS, seq, 2 * V_HEAD), BF16)],
        compiler_params=_cparams(("arbitrary", "arbitrary")),
        name="mla_mixer",
    )(zm, cos, sin, gcq, wq, gckv, wkv, gq, gk)


def _log_sigmoid(x):
    return -_softplus(-x)


def _gla_kernel(z_ref, wg_ref, bg_ref, go_ref, o_ref, la_scr, of_scr, st_scr):
    seq = z_ref.shape[1]
    n_groups = seq // GLA_ROWS
    chunks = GLA_ROWS // GLA_CHUNK
    qk_w = GLA_HEADS * GLA_DK
    k0, v0, g0, og0 = qk_w, 2 * qk_w, 2 * qk_w + W_GLA, 2 * qk_w + W_GLA + LANES

    def gate_rows(g, _):
        rows = pl.ds(pl.multiple_of(g * GLA_ROWS, GLA_ROWS), GLA_ROWS)
        x = _dot(z_ref[0, rows, g0:g0 + LANES], wg_ref[...]) + bg_ref[...]
        la_scr[rows, :] = _log_sigmoid(x) * (1.0 / GLA_TAU)
        return 0

    lax.fori_loop(0, n_groups, gate_rows, 0)

    ri = lax.broadcasted_iota(jnp.int32, (GLA_ROWS, GLA_ROWS), 0)
    ci = lax.broadcasted_iota(jnp.int32, (GLA_ROWS, GLA_ROWS), 1)
    same_chunk = (ri // GLA_CHUNK) == (ci // GLA_CHUNK)
    causal = (jnp.where(same_chunk, ci, GLA_ROWS) <= ri, jnp.where(same_chunk, ci, -1) >= ri)
    cum_mat = tuple(jnp.where(m, 1.0, 0.0).astype(BF16) for m in causal)
    tot_mat = jnp.where(same_chunk, 1.0, 0.0).astype(BF16)
    avg_mat = jnp.where((ri // GLA_DV) == (ci // GLA_DV), 1.0 / GLA_DV, 0.0).astype(BF16)
    qk_head = lax.broadcasted_iota(jnp.int32, (GLA_ROWS, qk_w), 1) // GLA_DK
    v_head = lax.broadcasted_iota(jnp.int32, (GLA_ROWS, W_GLA), 1) // GLA_DV
    st_rows = lax.broadcasted_iota(jnp.int32, (W_GLA, qk_w), 0) // GLA_DV
    st_cols = lax.broadcasted_iota(jnp.int32, (W_GLA, qk_w), 1) // GLA_DK
    st_mask = st_rows == st_cols
    col_chunk = ci // GLA_CHUNK

    def group(g, d):
        rows = pl.ds(pl.multiple_of(g * GLA_ROWS, GLA_ROWS), GLA_ROWS)
        la_hi, la_lo = _split_bf16(la_scr[rows, d * qk_w:(d + 1) * qk_w])
        b = _dot(cum_mat[d], la_hi) + _dot(cum_mat[d], la_lo)
        bt = _dot(tot_mat, la_hi) + _dot(tot_mat, la_lo)
        q = z_ref[0, rows, 0:qk_w].astype(F32) * (GLA_DK ** -0.5)
        k = z_ref[0, rows, k0:k0 + qk_w].astype(F32)
        v = z_ref[0, rows, v0:v0 + W_GLA]
        q_dec = (q * jnp.exp(b)).astype(BF16)
        k_inv = (k * jnp.exp(-b)).astype(BF16)
        k_end = (k * jnp.exp(bt - b)).astype(BF16)
        decay = jnp.exp(bt)

        probs, vals = [], []
        for h in range(GLA_HEADS):
            s = _dot_nt(jnp.where(qk_head == h, q_dec, jnp.zeros_like(q_dec)), k_inv)
            probs.append(jnp.where(causal[d], s, 0.0).astype(BF16))
            vals.append(jnp.where(v_head == h, v, jnp.zeros_like(v)))
        o = _dot(jnp.concatenate(probs, axis=1), jnp.concatenate(vals, axis=0))

        v_t = v.astype(F32).T
        st = st_scr[...]
        inter = [None] * chunks
        for c in (range(chunks) if d == 0 else reversed(range(chunks))):
            crow = slice(c * GLA_CHUNK, (c + 1) * GLA_CHUNK)
            inter[c] = _dot_nt(q_dec[crow], st.astype(BF16))
            upd = _dot(jnp.where(col_chunk == c, v_t, 0.0).astype(BF16), k_end)
            st = st * decay[c * GLA_CHUNK:c * GLA_CHUNK + 1] + jnp.where(st_mask, upd, 0.0)
        st_scr[...] = st
        return rows, o + jnp.concatenate(inter, axis=0)

    def fwd(g, _):
        rows, o = group(g, 0)
        of_scr[rows, :] = o
        return 0

    def bwd(i, _):
        rows, o = group(n_groups - 1 - i, 1)
        o = o + of_scr[rows, :]
        sq_hi, sq_lo = _split_bf16(o * o)
        ms = _dot(sq_hi, avg_mat) + _dot(sq_lo, avg_mat)
        og = z_ref[0, rows, og0:og0 + W_GLA].astype(F32)
        y = (o * lax.rsqrt(ms + NORM_EPS) * go_ref[...]) * (og * jax.nn.sigmoid(og))
        o_ref[0, rows, :] = y.astype(BF16)
        return 0

    st_scr[...] = jnp.zeros_like(st_scr)
    lax.fori_loop(0, n_groups, fwd, 0)
    st_scr[...] = jnp.zeros_like(st_scr)
    lax.fori_loop(0, n_groups, bwd, 0)


def _gla_mixer(zg, wg, bg, go):
    b, seq, _ = zg.shape
    qk_w = GLA_HEADS * GLA_DK
    return pl.pallas_call(
        _gla_kernel,
        grid=(b,),
        in_specs=[pl.BlockSpec((1, seq, ZG_W), lambda i: (i, 0, 0)), _full(wg.shape), _full(bg.shape),
                  _full(go.shape)],
        out_specs=pl.BlockSpec((1, seq, W_GLA), lambda i: (i, 0, 0)),
        out_shape=jax.ShapeDtypeStruct((b, seq, W_GLA), BF16),
        scratch_shapes=[pltpu.VMEM((seq, 2 * qk_w), F32), pltpu.VMEM((seq, W_GLA), F32),
                        pltpu.VMEM((W_GLA, qk_w), F32)],
        compiler_params=_cparams(("arbitrary",)),
        name="gla_mixer",
    )(zg, wg, bg, go)


def _out_ffn_kernel(x_ref, ol_ref, om_ref, og_ref, wol_ref, wom_ref, wog_ref, g_ref,
                    wgu_ref, wdown_ref, o_ref, h_ref):
    x1 = (x_ref[...] + _dot(ol_ref[...], wol_ref[...]) + _dot(om_ref[...], wom_ref[...])
          + _dot(og_ref[...], wog_ref[...]))
    o_ref[...] = x1
    h_ref[...] = _rms(x1, g_ref[...]).astype(BF16)

    def hidden_chunk(j, _):
        gu = _dot(h_ref[...], wgu_ref[j])
        gate = gu[:, 0:FFN_COLS]
        act = (gate * jax.nn.sigmoid(gate) * gu[:, FFN_COLS:]).astype(BF16)
        o_ref[...] += _dot(act, wdown_ref[j])
        return 0

    lax.fori_loop(0, wgu_ref.shape[0], hidden_chunk, 0)


def _out_ffn(x2, ol, om, og, wol, wom, wog, g, wgu, wdown):
    n, d = x2.shape
    tm = min(FFN_ROWS, n)
    row = lambda i: (i, 0)
    resident = lambda a: pl.BlockSpec(a.shape, lambda i: (0,) * a.ndim, pipeline_mode=pl.Buffered(1))
    return pl.pallas_call(
        _out_ffn_kernel,
        grid=(n // tm,),
        in_specs=[pl.BlockSpec((tm, d), row), pl.BlockSpec((tm, W_LRU), row),
                  pl.BlockSpec((tm, W_MLA), row), pl.BlockSpec((tm, W_GLA), row),
                  resident(wol), resident(wom), resident(wog), resident(g),
                  resident(wgu), resident(wdown)],
        out_specs=pl.BlockSpec((tm, d), row),
        out_shape=jax.ShapeDtypeStruct((n, d), F32),
        scratch_shapes=[pltpu.VMEM((tm, d), BF16)],
        compiler_params=_cparams(("arbitrary",)),
        name="out_ffn",
    )(x2, ol, om, og, wol, wom, wog, g, wgu, wdown)


def _block_diag(w):
    nb, bi, bo = w.shape
    eye = jnp.eye(nb, dtype=w.dtype)
    return (w[:, :, None, :] * eye[:, None, :, None]).reshape(nb * bi, nb * bo)


def _swap_halves(n):
    return np.concatenate([np.arange(n // 2, n), np.arange(0, n // 2)])


def _prep_layer(p):
    row = lambda v: v.reshape(1, -1).astype(F32)
    d_model = p["w_in"].shape[0]
    qk_w = GLA_HEADS * GLA_DK
    o_cq = 2 * W_LRU
    o_ckv = o_cq + Q_RANK
    o_rope = o_ckv + KV_RANK
    o_glaq = o_rope + QK_ROPE
    o_gf = o_glaq + 2 * qk_w + W_GLA
    o_og = o_gf + 2 * GLA_GATE_RANK
    sw = _swap_halves(QK_ROPE)
    w = p["w_in"]
    w_in = jnp.concatenate([
        w[:, :o_glaq], w[:, o_rope + sw],
        w[:, o_glaq:o_og], jnp.zeros((d_model, LANES - 2 * GLA_GATE_RANK), w.dtype), w[:, o_og:],
    ], axis=1).astype(BF16)

    wcat = jnp.concatenate([_block_diag(p[k]) for k in ("lru_wa_f", "lru_wx_f", "lru_wa_b", "lru_wx_b")],
                           axis=1).astype(BF16)
    bcat = jnp.concatenate([p[k] for k in ("lru_ba_f", "lru_bx_f", "lru_ba_b", "lru_bx_b")]).reshape(1, -1)
    lam = jnp.stack([p["lru_lam_f"], p["lru_lam_b"]])

    wq = p["mla_w_uq"].reshape(Q_RANK, MLA_HEADS, QK_HEAD)
    wq = jnp.concatenate([wq, wq[:, :, QK_NOPE + sw]], axis=2).reshape(Q_RANK, -1).astype(BF16)
    wkv = p["mla_w_ukv"].reshape(KV_RANK, MLA_HEADS, QK_NOPE + V_HEAD)
    wkv = jnp.concatenate([wkv[:, :, :QK_NOPE].reshape(KV_RANK, -1),
                           wkv[:, :, QK_NOPE:].reshape(KV_RANK, -1)], axis=1).astype(BF16)

    def qk_gains(g):
        pad = jnp.zeros((LANES - QK_ROPE,), F32)
        return jnp.stack([g[:QK_NOPE], jnp.concatenate([g[QK_NOPE:], pad]),
                          jnp.concatenate([g[QK_NOPE + sw], pad])]).astype(F32)

    wg = jnp.zeros((LANES, 2 * qk_w), F32)
    wg = wg.at[:GLA_GATE_RANK, :qk_w].set(p["gla_wa2_f"])
    wg = wg.at[GLA_GATE_RANK:2 * GLA_GATE_RANK, qk_w:].set(p["gla_wa2_b"])
    bg = jnp.concatenate([p["gla_ba2_f"], p["gla_ba2_b"]]).reshape(1, -1)

    w_out = p["w_out"].astype(BF16)
    d_ff = p["w_ffn_out"].shape[0]
    n_f = d_ff // FFN_COLS
    wgu = p["w_ffn_in"].astype(BF16).reshape(d_model, 2, n_f, FFN_COLS)
    wgu = wgu.transpose(2, 0, 1, 3).reshape(n_f, d_model, 2 * FFN_COLS)
    wdown = p["w_ffn_out"].astype(BF16).reshape(n_f, FFN_COLS, d_model)
    return dict(
        g_mix=row(p["g_mix"]), w_in=w_in,
        conv_w=p["conv_w"], conv_b=row(p["conv_b"]), wcat=wcat, bcat=bcat, lam=lam,
        gcq=row(p["mla_g_cq"]), wq=wq, gckv=row(p["mla_g_ckv"]), wkv=wkv,
        gq=qk_gains(p["mla_g_q"]), gk=qk_gains(p["mla_g_k"]),
        wg=wg.astype(BF16), bg=bg, go=row(jnp.tile(p["gla_g_o"], GLA_HEADS)),
        wol=w_out[:W_LRU], wom=w_out[W_LRU:W_LRU + W_MLA], wog=w_out[W_LRU + W_MLA:],
        g_ffn=row(p["g_ffn"]), wgu=wgu, wdown=wdown,
    )


def _rotary_tables(seq):
    half = QK_ROPE // 2
    inv = 1.0 / (ROPE_THETA ** (jnp.arange(half, dtype=F32) * 2.0 / QK_ROPE))
    ang = jnp.arange(seq, dtype=F32)[:, None] * inv[None, :]
    cos, sin = jnp.cos(ang), jnp.sin(ang)
    pad = jnp.zeros((seq, LANES - QK_ROPE), F32)
    return jnp.concatenate([cos, cos, pad], axis=1), jnp.concatenate([-sin, sin, pad], axis=1)


def _layer(x, cos, sin, lp):
    b, seq, d = x.shape
    x2 = x.reshape(b * seq, d)
    zl, zm, zg = _in_proj(x2, lp["g_mix"], lp["w_in"])
    o_lru = _lru_mixer(zl.reshape(b, seq, ZL_W), lp["conv_w"], lp["conv_b"], lp["wcat"], lp["bcat"], lp["lam"])
    o_mla = _mla_mixer(zm.reshape(b, seq, ZM_W), cos, sin, lp["gcq"], lp["wq"], lp["gckv"], lp["wkv"],
                       lp["gq"], lp["gk"])
    o_gla = _gla_mixer(zg.reshape(b, seq, ZG_W), lp["wg"], lp["bg"], lp["go"])
    y = _out_ffn(x2, o_lru.reshape(b * seq, W_LRU), o_mla.reshape(b * seq, W_MLA),
                 o_gla.reshape(b * seq, W_GLA), lp["wol"], lp["wom"], lp["wog"], lp["g_ffn"],
                 lp["wgu"], lp["wdown"])
    return y.reshape(b, seq, d)


_PARAM_NAMES = ("g_mix", "w_in", "conv_w", "conv_b",
                "lru_wa_f", "lru_ba_f", "lru_wx_f", "lru_bx_f", "lru_lam_f",
                "lru_wa_b", "lru_ba_b", "lru_wx_b", "lru_bx_b", "lru_lam_b",
                "mla_g_cq", "mla_w_uq", "mla_g_ckv", "mla_w_ukv", "mla_g_q", "mla_g_k",
                "gla_wa2_f", "gla_ba2_f", "gla_wa2_b", "gla_ba2_b", "gla_g_o",
                "w_out", "g_ffn", "w_ffn_in", "w_ffn_out")


def kernel(x_prompt, x_sample, g_mix, w_in, conv_w, conv_b, lru_wa_f, lru_ba_f, lru_wx_f, lru_bx_f, lru_lam_f, lru_wa_b, lru_ba_b, lru_wx_b, lru_bx_b, lru_lam_b, mla_g_cq, mla_w_uq, mla_g_ckv, mla_w_ukv, mla_g_q, mla_g_k, gla_wa2_f, gla_ba2_f, gla_wa2_b, gla_ba2_b, gla_g_o, w_out, g_ffn, w_ffn_in, w_ffn_out):
    stacked = dict(zip(_PARAM_NAMES, (
        g_mix, w_in, conv_w, conv_b, lru_wa_f, lru_ba_f, lru_wx_f, lru_bx_f, lru_lam_f,
        lru_wa_b, lru_ba_b, lru_wx_b, lru_bx_b, lru_lam_b,
        mla_g_cq, mla_w_uq, mla_g_ckv, mla_w_ukv, mla_g_q, mla_g_k,
        gla_wa2_f, gla_ba2_f, gla_wa2_b, gla_ba2_b, gla_g_o, w_out, g_ffn, w_ffn_in, w_ffn_out)))
    depth = g_mix.shape[0]
    layers = [_prep_layer({k: v[l] for k, v in stacked.items()}) for l in range(depth)]

    def trunk(x):
        cos, sin = _rotary_tables(x.shape[1])
        for lp in layers:
            x = _layer(x, cos, sin, lp)
        return x

    return trunk(x_prompt), trunk(x_sample)
```

```python
import functools

import jax
import jax.numpy as jnp
import numpy as np
from jax import lax
from jax.experimental import pallas as pl
from jax.experimental.pallas import tpu as pltpu

F32 = jnp.float32
BF16 = jnp.bfloat16

NORM_EPS = 1e-6
W_LRU = 256
W_MLA = 512
W_GLA = 256
LRU_BLOCKS = 8
LRU_C = 8.0
CONV_W = 4
CONV_LEFT = 2
MLA_HEADS = 4
QK_NOPE = 128
QK_ROPE = 64
QK_HEAD = QK_NOPE + QK_ROPE
V_HEAD = 128
Q_RANK = 256
KV_RANK = 128
ROPE_THETA = 10000.0
GLA_HEADS = 4
GLA_DK = 32
GLA_DV = 64
GLA_GATE_RANK = 16
GLA_TAU = 16.0
GLA_CHUNK = 64

LANES = 128
SUBLANES = 8
VMEM_LIMIT_BYTES = 56 * 1024 * 1024

ZL_W = 2 * W_LRU
ZM_W = Q_RANK + KV_RANK + 2 * QK_ROPE
ZG_W = 2 * GLA_HEADS * GLA_DK + W_GLA + LANES + W_GLA

IN_ROWS = 1024
FFN_ROWS = 1024
FFN_COLS = 256
MLA_Q_ROWS = 512
MLA_KV_ROWS = 512
MLA_KEY_BLOCK = 512
MLA_SAFE_SHIFT = 60.0
GLA_ROWS = 256
LRU_SEGS = SUBLANES


def _dot(a, b):
    return jnp.dot(a, b, preferred_element_type=F32)


def _dot_nt(a, b):
    return lax.dot_general(a, b, (((1,), (1,)), ((), ())), preferred_element_type=F32)


def _cparams(sem):
    return pltpu.CompilerParams(dimension_semantics=sem, vmem_limit_bytes=VMEM_LIMIT_BYTES)


def _full(shape):
    return pl.BlockSpec(shape, lambda *_: (0,) * len(shape))


def _in_proj_kernel(x_ref, g_ref, w_ref, zl_ref, zm_ref, zg_ref):
    x = x_ref[...]
    ms = jnp.mean(x * x, axis=-1, keepdims=True)
    h = (x * lax.rsqrt(ms + NORM_EPS) * g_ref[...]).astype(BF16)
    zl_ref[...] = _dot(h, w_ref[:, 0:ZL_W]).astype(BF16)
    zm_ref[...] = _dot(h, w_ref[:, ZL_W:ZL_W + ZM_W]).astype(BF16)
    zg_ref[...] = _dot(h, w_ref[:, ZL_W + ZM_W:]).astype(BF16)


def _in_proj(x2, g, w):
    n, d = x2.shape
    tm = min(IN_ROWS, n)
    row = lambda i: (i, 0)
    return pl.pallas_call(
        _in_proj_kernel,
        grid=(n // tm,),
        in_specs=[pl.BlockSpec((tm, d), row), _full(g.shape), _full(w.shape)],
        out_specs=[pl.BlockSpec((tm, ZL_W), row), pl.BlockSpec((tm, ZM_W), row),
                   pl.BlockSpec((tm, ZG_W), row)],
        out_shape=[jax.ShapeDtypeStruct((n, ZL_W), BF16), jax.ShapeDtypeStruct((n, ZM_W), BF16),
                   jax.ShapeDtypeStruct((n, ZG_W), BF16)],
        compiler_params=_cparams(("arbitrary",)),
        name="in_proj",
    )(x2, g, w)


def _softplus(x):
    return jnp.maximum(x, 0.0) + jnp.log1p(jnp.exp(-jnp.abs(x)))


def _lru_kernel(z_ref, cw_ref, cb_ref, wcat_ref, bcat_ref, lam_ref, o_ref,
                u_ref, af_ref, vf_ref, ab_ref, vb_ref, c_ref):
    seq = z_ref.shape[1]
    seg = seq // LRU_SEGS
    pitch = seg + SUBLANES
    halves = W_LRU // LANES

    xin = z_ref[0, :, 0:W_LRU].astype(F32)
    row = lax.broadcasted_iota(jnp.int32, (seq, W_LRU), 0)
    cw = cw_ref[...]
    u = xin * cw[CONV_LEFT:CONV_LEFT + 1]
    for j in range(CONV_W):
        off = j - CONV_LEFT
        if off == 0:
            continue
        shifted = pltpu.roll(xin, (-off) % seq, 0)
        valid = (row >= -off) if off < 0 else (row < seq - off)
        u = u + jnp.where(valid, shifted, 0.0) * cw[j:j + 1]
    u_ref[...] = u + cb_ref[...]

    sp = _softplus(-lam_ref[...])

    def gates(s, _):
        r0 = pl.multiple_of(s * seg, seg)
        p0 = pl.multiple_of(s * pitch, SUBLANES)
        uc = u_ref[pl.ds(r0, seg), :]
        g = _dot(uc.astype(BF16), wcat_ref[...]) + bcat_ref[...]
        for d, (a_ref, v_ref) in enumerate(((af_ref, vf_ref), (ab_ref, vb_ref))):
            base = d * 2 * W_LRU
            r = jax.nn.sigmoid(g[:, base:base + W_LRU])
            i = jax.nn.sigmoid(g[:, base + W_LRU:base + 2 * W_LRU])
            log_a = -LRU_C * r * sp[d:d + 1]
            a = jnp.exp(log_a)
            v = jnp.sqrt(-jnp.tanh(log_a) * (a * a + 1.0)) * (i * uc)
            for hh in range(halves):
                a_ref[hh, pl.ds(p0, seg), :] = a[:, hh * LANES:(hh + 1) * LANES]
                v_ref[hh, pl.ds(p0, seg), :] = v[:, hh * LANES:(hh + 1) * LANES]
        return 0

    lax.fori_loop(0, LRU_SEGS, gates, 0)

    def rows_at(i):
        return pl.ds(i, LRU_SEGS, stride=pitch)

    def scan(i, carry):
        hf, pf, hb, pb = carry
        ib = seg - 1 - i
        hf_n, pf_n, hb_n, pb_n = [], [], [], []
        for hh in range(halves):
            a = af_ref[hh, rows_at(i), :]
            h = a * hf[hh] + vf_ref[hh, rows_at(i), :]
            p = a * pf[hh]
            vf_ref[hh, rows_at(i), :] = h
            af_ref[hh, rows_at(i), :] = p
            hf_n.append(h)
            pf_n.append(p)
            a = ab_ref[hh, rows_at(ib), :]
            h = a * hb[hh] + vb_ref[hh, rows_at(ib), :]
            p = a * pb[hh]
            vb_ref[hh, rows_at(ib), :] = h
            ab_ref[hh, rows_at(ib), :] = p
            hb_n.append(h)
            pb_n.append(p)
        return tuple(hf_n), tuple(pf_n), tuple(hb_n), tuple(pb_n)

    zeros = tuple(jnp.zeros((LRU_SEGS, LANES), F32) for _ in range(halves))
    ones = tuple(jnp.ones((LRU_SEGS, LANES), F32) for _ in range(halves))
    hf_e, pf_e, hb_e, pb_e = lax.fori_loop(0, seg, scan, (zeros, ones, zeros, ones), unroll=2)

    srow = lax.broadcasted_iota(jnp.int32, (LRU_SEGS, LANES), 0)
    for hh in range(halves):
        c = jnp.zeros((LRU_SEGS, LANES), F32)
        for _ in range(LRU_SEGS - 1):
            c = jnp.where(srow == 0, 0.0, pltpu.roll(hf_e[hh] + pf_e[hh] * c, 1, 0))
        c_ref[0, hh] = c
        c = jnp.zeros((LRU_SEGS, LANES), F32)
        for _ in range(LRU_SEGS - 1):
            c = jnp.where(srow == LRU_SEGS - 1, 0.0,
                          pltpu.roll(hb_e[hh] + pb_e[hh] * c, LRU_SEGS - 1, 0))
        c_ref[1, hh] = c

    def finish(s, _):
        r0 = pl.multiple_of(s * seg, seg)
        p0 = pl.multiple_of(s * pitch, SUBLANES)
        for hh in range(halves):
            h = (vf_ref[hh, pl.ds(p0, seg), :] + af_ref[hh, pl.ds(p0, seg), :] * c_ref[0, hh, pl.ds(s, 1), :]
                 + vb_ref[hh, pl.ds(p0, seg), :] + ab_ref[hh, pl.ds(p0, seg), :] * c_ref[1, hh, pl.ds(s, 1), :])
            gate = z_ref[0, pl.ds(r0, seg), W_LRU + hh * LANES:W_LRU + (hh + 1) * LANES].astype(F32)
            o_ref[0, pl.ds(r0, seg), hh * LANES:(hh + 1) * LANES] = (h * jax.nn.gelu(gate)).astype(BF16)
        return 0

    lax.fori_loop(0, LRU_SEGS, finish, 0)


def _lru_mixer(zl, cw, cb, wcat, bcat, lam):
    b, seq, _ = zl.shape
    pitch = seq // LRU_SEGS + SUBLANES
    scan = pltpu.VMEM((W_LRU // LANES, LRU_SEGS * pitch, LANES), F32)
    return pl.pallas_call(
        _lru_kernel,
        grid=(b,),
        in_specs=[pl.BlockSpec((1, seq, ZL_W), lambda i: (i, 0, 0)), _full(cw.shape), _full(cb.shape),
                  _full(wcat.shape), _full(bcat.shape), _full(lam.shape)],
        out_specs=pl.BlockSpec((1, seq, W_LRU), lambda i: (i, 0, 0)),
        out_shape=jax.ShapeDtypeStruct((b, seq, W_LRU), BF16),
        scratch_shapes=[pltpu.VMEM((seq, W_LRU), F32), scan, scan, scan, scan,
                        pltpu.VMEM((2, W_LRU // LANES, LRU_SEGS, LANES), F32)],
        compiler_params=_cparams(("arbitrary",)),
        name="lru_mixer",
    )(zl, cw, cb, wcat, bcat, lam)


def _rms(x, g):
    ms = jnp.mean(x * x, axis=-1, keepdims=True)
    return x * lax.rsqrt(ms + NORM_EPS) * g


def _split_bf16(x):
    hi = x.astype(BF16)
    lo = (x - hi.astype(F32)).astype(BF16)
    return hi, lo


def _group_sums(x, ones):
    hi, lo = _split_bf16(x)
    return _dot(hi, ones) + _dot(lo, ones)


def _block_ones(n, block):
    ri = lax.broadcasted_iota(jnp.int32, (n, n), 0) // block
    ci = lax.broadcasted_iota(jnp.int32, (n, n), 1) // block
    return jnp.where(ri == ci, 1.0, 0.0).astype(BF16)


def _rope_part(t, cos, sin, gains):
    return t * cos * gains[1:2] + pltpu.roll(t, QK_ROPE, 1) * sin * gains[2:3]


def _mla_kernel(z_ref, cos_ref, sin_ref, gcq_ref, wq_ref, gckv_ref, wkv_ref, gq_ref, gk_ref,
                o_ref, k_scr, v_scr):
    seq = z_ref.shape[1]
    qi = pl.program_id(1)
    c_kv0 = Q_RANK
    c_rope0 = Q_RANK + KV_RANK
    head_w = QK_NOPE + 2 * QK_ROPE
    ones_pair = _block_ones(2 * LANES, LANES)

    @pl.when(qi == 0)
    def _():
        gk = gk_ref[...]
        for c in range(seq // MLA_KV_ROWS):
            rows = slice(c * MLA_KV_ROWS, (c + 1) * MLA_KV_ROWS)
            ckv = _rms(z_ref[0, rows, c_kv0:c_rope0].astype(F32), gckv_ref[...])
            kv = _dot(ckv.astype(BF16), wkv_ref[...])
            t = z_ref[0, rows, c_rope0:c_rope0 + 2 * QK_ROPE].astype(F32)
            t_sq = 0.5 * t * t
            rope = _rope_part(t, cos_ref[rows, :], sin_ref[rows, :], gk)
            for h in range(MLA_HEADS):
                kn = kv[:, h * QK_NOPE:(h + 1) * QK_NOPE]
                ss = jnp.sum(kn * kn + t_sq, axis=-1, keepdims=True)
                inv = lax.rsqrt(ss * (1.0 / QK_HEAD) + NORM_EPS)
                k_scr[h, rows, 0:QK_NOPE] = (kn * inv * gk[0:1]).astype(BF16)
                k_scr[h, rows, QK_NOPE:head_w] = (rope * inv).astype(BF16)
                v0 = MLA_HEADS * QK_NOPE + h * V_HEAD
                v_scr[h, rows, 0:V_HEAD] = kv[:, v0:v0 + V_HEAD].astype(BF16)
                v_scr[h, rows, V_HEAD:2 * V_HEAD] = jnp.ones((MLA_KV_ROWS, V_HEAD), BF16)

    r0 = pl.multiple_of(qi * MLA_Q_ROWS, MLA_Q_ROWS)
    qrows = pl.ds(r0, MLA_Q_ROWS)
    cq = z_ref[0, qrows, 0:Q_RANK].astype(F32)
    ms = _group_sums(cq * cq, jnp.ones((Q_RANK, Q_RANK), BF16)) * (1.0 / Q_RANK)
    cq = cq * lax.rsqrt(ms + NORM_EPS) * gcq_ref[...]
    qall = _dot(cq.astype(BF16), wq_ref[...])
    cos = cos_ref[qrows, :]
    sin = sin_ref[qrows, :]
    gq = gq_ref[...]
    scale = QK_HEAD ** -0.5 * np.log2(np.e)
    q_heads = []
    sq = lambda h: (qall[:, h * head_w:h * head_w + QK_NOPE] ** 2
                    + 0.5 * qall[:, h * head_w + QK_NOPE:(h + 1) * head_w] ** 2)
    for h in range(MLA_HEADS):
        qn = qall[:, h * head_w:h * head_w + QK_NOPE]
        t = qall[:, h * head_w + QK_NOPE:(h + 1) * head_w]
        if h % 2 == 0:
            ss2 = _group_sums(jnp.concatenate([sq(h), sq(h + 1)], axis=-1), ones_pair)
        ss = ss2[:, (h % 2) * LANES:(h % 2 + 1) * LANES]
        inv = lax.rsqrt(ss * (1.0 / QK_HEAD) + NORM_EPS) * scale
        qh = jnp.concatenate([qn * inv * gq[0:1], _rope_part(t, cos, sin, gq) * inv], axis=-1)
        q_heads.append(qh.astype(BF16))

    def write_head(h, ov):
        o_ref[0, :, h * V_HEAD:(h + 1) * V_HEAD] = (ov[:, 0:V_HEAD] / ov[:, V_HEAD:]).astype(BF16)

    g_max = lambda g: jnp.max(jnp.abs(g[0:2]))
    bound = scale * QK_HEAD * g_max(gq) * g_max(gk_ref[...])
    shift_is_safe = bound <= MLA_SAFE_SHIFT

    @pl.when(shift_is_safe)
    def _():
        for h in range(MLA_HEADS):
            ov = jnp.zeros((MLA_Q_ROWS, 2 * V_HEAD), F32)
            for kb in range(seq // MLA_KEY_BLOCK):
                keys = slice(kb * MLA_KEY_BLOCK, (kb + 1) * MLA_KEY_BLOCK)
                s = _dot_nt(q_heads[h], k_scr[h, keys, :])
                ov = ov + _dot(jnp.exp2(s - bound).astype(BF16), v_scr[h, keys, :])
            write_head(h, ov)

    @pl.when(jnp.logical_not(shift_is_safe))
    def _():
        for h in range(MLA_HEADS):
            s = _dot_nt(q_heads[h], k_scr[h])
            m = jnp.max(s, axis=-1, keepdims=True)
            write_head(h, _dot(jnp.exp2(s - m).astype(BF16), v_scr[h]))


def _mla_mixer(zm, cos, sin, gcq, wq, gckv, wkv, gq, gk):
    b, seq, _ = zm.shape
    head_w = QK_NOPE + 2 * QK_ROPE
    return pl.pallas_call(
        _mla_kernel,
        grid=(b, seq // MLA_Q_ROWS),
        in_specs=[pl.BlockSpec((1, seq, ZM_W), lambda i, j: (i, 0, 0)), _full(cos.shape), _full(sin.shape),
                  _full(gcq.shape), _full(wq.shape), _full(gckv.shape), _full(wkv.shape),
                  _full(gq.shape), _full(gk.shape)],
        out_specs=pl.BlockSpec((1, MLA_Q_ROWS, W_MLA), lambda i, j: (i, j, 0)),
        out_shape=jax.ShapeDtypeStruct((b, seq, W_MLA), BF16),
        scratch_shapes=[pltpu.VMEM((MLA_HEADS, seq, head_w), BF16),
                        pltpu.VMEM((MLA_HEADS, seq, 2 * V_HEAD), BF16)],
        compiler_params=_cparams(("arbitrary", "arbitrary")),
        name="mla_mixer",
    )(zm, cos, sin, gcq, wq, gckv, wkv, gq, gk)


def _log_sigmoid(x):
    return jnp.minimum(x, 0.0) - jnp.log(1.0 + jnp.exp(-jnp.abs(x)))


def _gla_kernel(z_ref, wg_ref, bg_ref, go_ref, o_ref, oi_scr, qd_scr, ki_scr, ke_scr, u_scr, dec_scr, s_scr):
    seq = z_ref.shape[1]
    n_groups = seq // GLA_ROWS
    chunks = GLA_ROWS // GLA_CHUNK
    qk_w = GLA_HEADS * GLA_DK
    k0, v0, g0, og0 = qk_w, 2 * qk_w, 2 * qk_w + W_GLA, 2 * qk_w + W_GLA + LANES

    ri = lax.broadcasted_iota(jnp.int32, (GLA_ROWS, GLA_ROWS), 0)
    ci = lax.broadcasted_iota(jnp.int32, (GLA_ROWS, GLA_ROWS), 1)
    same_chunk = (ri // GLA_CHUNK) == (ci // GLA_CHUNK)
    causal = (jnp.where(same_chunk, ci, GLA_ROWS) <= ri, jnp.where(same_chunk, ci, -1) >= ri)
    cum_mat = jnp.where(causal[0], 1.0, 0.0).astype(BF16)
    avg_mat = jnp.where((ri // GLA_DV) == (ci // GLA_DV), 1.0 / GLA_DV, 0.0).astype(BF16)
    qk_head = lax.broadcasted_iota(jnp.int32, (GLA_ROWS, qk_w), 1) // GLA_DK
    qk_chunk = lax.broadcasted_iota(jnp.int32, (GLA_ROWS, qk_w), 0) // GLA_CHUNK
    v_head = lax.broadcasted_iota(jnp.int32, (GLA_ROWS, W_GLA), 1) // GLA_DV
    st_mask = (lax.broadcasted_iota(jnp.int32, (W_GLA, qk_w), 0) // GLA_DV
               == lax.broadcasted_iota(jnp.int32, (W_GLA, qk_w), 1) // GLA_DK)
    col_chunk = ci // GLA_CHUNK

    def rows_of(g):
        return pl.ds(pl.multiple_of(g * GLA_ROWS, GLA_ROWS), GLA_ROWS)

    def decays(g, _):
        rows = rows_of(g)
        x = _dot(z_ref[0, rows, g0:g0 + LANES], wg_ref[...]) + bg_ref[...]
        la = _log_sigmoid(x) * (1.0 / GLA_TAU)
        la_hi, la_lo = _split_bf16(la)
        b_pre = _dot(cum_mat, la_hi) + _dot(cum_mat, la_lo)
        ends = [b_pre[(c + 1) * GLA_CHUNK - 1:(c + 1) * GLA_CHUNK] for c in range(chunks)]
        bt = jnp.concatenate([jnp.broadcast_to(e, (GLA_CHUNK, 2 * qk_w)) for e in ends], axis=0)
        b_dir = (b_pre[:, 0:qk_w], bt[:, qk_w:] - b_pre[:, qk_w:] + la[:, qk_w:])
        q = z_ref[0, rows, 0:qk_w].astype(F32) * (GLA_DK ** -0.5)
        k = z_ref[0, rows, k0:k0 + qk_w].astype(F32)
        for d in range(2):
            b = b_dir[d]
            cols = slice(d * qk_w, (d + 1) * qk_w)
            qd_scr[rows, cols] = (q * jnp.exp(b)).astype(BF16)
            ki_scr[rows, cols] = (k * jnp.exp(-b)).astype(BF16)
            ke_scr[rows, cols] = (k * jnp.exp(bt[:, cols] - b)).astype(BF16)
        dec_scr[g] = jnp.exp(jnp.concatenate(ends + ends, axis=0))
        return 0

    lax.fori_loop(0, n_groups, decays, 0, unroll=True)

    def local(g, _):
        rows = rows_of(g)
        v = z_ref[0, rows, v0:v0 + W_GLA]
        probs = None
        for d in range(2):
            cols = slice(d * qk_w, (d + 1) * qk_w)
            q_dec = qd_scr[rows, cols]
            q_stack = jnp.concatenate([jnp.where(qk_head == h, q_dec, jnp.zeros_like(q_dec))
                                       for h in range(GLA_HEADS)], axis=0)
            s = _dot_nt(q_stack, ki_scr[rows, cols])
            p = [jnp.where(causal[d], s[h * GLA_ROWS:(h + 1) * GLA_ROWS], 0.0) for h in range(GLA_HEADS)]
            probs = p if probs is None else [a + c for a, c in zip(probs, p)]
        v_stack = jnp.concatenate([jnp.where(v_head == h, v, jnp.zeros_like(v)) for h in range(GLA_HEADS)],
                                  axis=0)
        oi_scr[rows, :] = _dot(jnp.concatenate([p.astype(BF16) for p in probs], axis=1), v_stack)
        v_t = v.astype(F32).T
        vt_stack = jnp.concatenate([jnp.where(col_chunk == c, v_t, 0.0) for c in range(chunks)], axis=0)
        u_scr[g] = _dot(vt_stack.astype(BF16), ke_scr[rows, :])
        return 0

    lax.fori_loop(0, n_groups, local, 0)

    for d in range(2):
        def step(i, st, d=d):
            g = i if d == 0 else n_groups - 1 - i
            dec = dec_scr[g]
            for c in (range(chunks) if d == 0 else reversed(range(chunks))):
                blk = (d * chunks + c) * qk_w
                s_scr[g, :, blk:blk + qk_w] = st.astype(BF16)
                upd = u_scr[g, c * W_GLA:(c + 1) * W_GLA, d * qk_w:(d + 1) * qk_w]
                st = st * dec[c:c + 1, d * qk_w:(d + 1) * qk_w] + jnp.where(st_mask, upd, 0.0)
            return st

        lax.fori_loop(0, n_groups, step, jnp.zeros((W_GLA, qk_w), F32))

    def finish(g, _):
        rows = rows_of(g)
        qd = qd_scr[rows, :]
        lhs = jnp.concatenate([jnp.where(qk_chunk == c, qd[:, d * qk_w:(d + 1) * qk_w],
                                         jnp.zeros((GLA_ROWS, qk_w), BF16))
                               for d in range(2) for c in range(chunks)], axis=1)
        o = oi_scr[rows, :] + _dot_nt(lhs, s_scr[g])
        sq_hi, sq_lo = _split_bf16(o * o)
        ms = _dot(sq_hi, avg_mat) + _dot(sq_lo, avg_mat)
        og = z_ref[0, rows, og0:og0 + W_GLA].astype(F32)
        y = (o * lax.rsqrt(ms + NORM_EPS) * go_ref[...]) * (og * jax.nn.sigmoid(og))
        o_ref[0, rows, :] = y.astype(BF16)
        return 0

    lax.fori_loop(0, n_groups, finish, 0, unroll=True)


def _gla_mixer(zg, wg, bg, go):
    b, seq, _ = zg.shape
    qk_w = GLA_HEADS * GLA_DK
    n_groups = seq // GLA_ROWS
    chunks = GLA_ROWS // GLA_CHUNK
    return pl.pallas_call(
        _gla_kernel,
        grid=(b,),
        in_specs=[pl.BlockSpec((1, seq, ZG_W), lambda i: (i, 0, 0)), _full(wg.shape), _full(bg.shape),
                  _full(go.shape)],
        out_specs=pl.BlockSpec((1, seq, W_GLA), lambda i: (i, 0, 0)),
        out_shape=jax.ShapeDtypeStruct((b, seq, W_GLA), BF16),
        scratch_shapes=[pltpu.VMEM((seq, W_GLA), F32),
                        pltpu.VMEM((seq, 2 * qk_w), BF16),
                        pltpu.VMEM((seq, 2 * qk_w), BF16),
                        pltpu.VMEM((seq, 2 * qk_w), BF16),
                        pltpu.VMEM((n_groups, chunks * W_GLA, 2 * qk_w), F32),
                        pltpu.VMEM((n_groups, 2 * chunks, 2 * qk_w), F32),
                        pltpu.VMEM((n_groups, W_GLA, 2 * chunks * qk_w), BF16)],
        compiler_params=_cparams(("arbitrary",)),
        name="gla_mixer",
    )(zg, wg, bg, go)


def _out_ffn_kernel(x_ref, ol_ref, om_ref, og_ref, wol_ref, wom_ref, wog_ref, g_ref,
                    wgu_ref, wdown_ref, o_ref, h_ref):
    x1 = (x_ref[...] + _dot(ol_ref[...], wol_ref[...]) + _dot(om_ref[...], wom_ref[...])
          + _dot(og_ref[...], wog_ref[...]))
    o_ref[...] = x1
    h_ref[...] = _rms(x1, g_ref[...]).astype(BF16)

    def hidden_chunk(j, _):
        gu = _dot(h_ref[...], wgu_ref[j])
        gate = gu[:, 0:FFN_COLS]
        act = (gate * jax.nn.sigmoid(gate) * gu[:, FFN_COLS:]).astype(BF16)
        o_ref[...] += _dot(act, wdown_ref[j])
        return 0

    lax.fori_loop(0, wgu_ref.shape[0], hidden_chunk, 0)


def _out_ffn(x2, ol, om, og, wol, wom, wog, g, wgu, wdown):
    n, d = x2.shape
    tm = min(FFN_ROWS, n)
    row = lambda i: (i, 0)
    resident = lambda a: pl.BlockSpec(a.shape, lambda i: (0,) * a.ndim, pipeline_mode=pl.Buffered(1))
    return pl.pallas_call(
        _out_ffn_kernel,
        grid=(n // tm,),
        in_specs=[pl.BlockSpec((tm, d), row), pl.BlockSpec((tm, W_LRU), row),
                  pl.BlockSpec((tm, W_MLA), row), pl.BlockSpec((tm, W_GLA), row),
                  resident(wol), resident(wom), resident(wog), resident(g),
                  resident(wgu), resident(wdown)],
        out_specs=pl.BlockSpec((tm, d), row),
        out_shape=jax.ShapeDtypeStruct((n, d), F32),
        scratch_shapes=[pltpu.VMEM((tm, d), BF16)],
        compiler_params=_cparams(("arbitrary",)),
        name="out_ffn",
    )(x2, ol, om, og, wol, wom, wog, g, wgu, wdown)


def _block_diag(w):
    nb, bi, bo = w.shape
    eye = jnp.eye(nb, dtype=w.dtype)
    return (w[:, :, None, :] * eye[:, None, :, None]).reshape(nb * bi, nb * bo)


def _swap_halves(n):
    return np.concatenate([np.arange(n // 2, n), np.arange(0, n // 2)])


def _prep_layer(p):
    row = lambda v: v.reshape(1, -1).astype(F32)
    d_model = p["w_in"].shape[0]
    qk_w = GLA_HEADS * GLA_DK
    o_cq = 2 * W_LRU
    o_ckv = o_cq + Q_RANK
    o_rope = o_ckv + KV_RANK
    o_glaq = o_rope + QK_ROPE
    o_gf = o_glaq + 2 * qk_w + W_GLA
    o_og = o_gf + 2 * GLA_GATE_RANK
    sw = _swap_halves(QK_ROPE)
    w = p["w_in"]
    w_in = jnp.concatenate([
        w[:, :o_glaq], w[:, o_rope + sw],
        w[:, o_glaq:o_og], jnp.zeros((d_model, LANES - 2 * GLA_GATE_RANK), w.dtype), w[:, o_og:],
    ], axis=1).astype(BF16)

    wcat = jnp.concatenate([_block_diag(p[k]) for k in ("lru_wa_f", "lru_wx_f", "lru_wa_b", "lru_wx_b")],
                           axis=1).astype(BF16)
    bcat = jnp.concatenate([p[k] for k in ("lru_ba_f", "lru_bx_f", "lru_ba_b", "lru_bx_b")]).reshape(1, -1)
    lam = jnp.stack([p["lru_lam_f"], p["lru_lam_b"]])

    wq = p["mla_w_uq"].reshape(Q_RANK, MLA_HEADS, QK_HEAD)
    wq = jnp.concatenate([wq, wq[:, :, QK_NOPE + sw]], axis=2).reshape(Q_RANK, -1).astype(BF16)
    wkv = p["mla_w_ukv"].reshape(KV_RANK, MLA_HEADS, QK_NOPE + V_HEAD)
    wkv = jnp.concatenate([wkv[:, :, :QK_NOPE].reshape(KV_RANK, -1),
                           wkv[:, :, QK_NOPE:].reshape(KV_RANK, -1)], axis=1).astype(BF16)

    def qk_gains(g):
        pad = jnp.zeros((LANES - QK_ROPE,), F32)
        return jnp.stack([g[:QK_NOPE], jnp.concatenate([g[QK_NOPE:], pad]),
                          jnp.concatenate([g[QK_NOPE + sw], pad])]).astype(F32)

    wg = jnp.zeros((LANES, 2 * qk_w), F32)
    wg = wg.at[:GLA_GATE_RANK, :qk_w].set(p["gla_wa2_f"])
    wg = wg.at[GLA_GATE_RANK:2 * GLA_GATE_RANK, qk_w:].set(p["gla_wa2_b"])
    bg = jnp.concatenate([p["gla_ba2_f"], p["gla_ba2_b"]]).reshape(1, -1)

    w_out = p["w_out"].astype(BF16)
    d_ff = p["w_ffn_out"].shape[0]
    n_f = d_ff // FFN_COLS
    wgu = p["w_ffn_in"].astype(BF16).reshape(d_model, 2, n_f, FFN_COLS)
    wgu = wgu.transpose(2, 0, 1, 3).reshape(n_f, d_model, 2 * FFN_COLS)
    wdown = p["w_ffn_out"].astype(BF16).reshape(n_f, FFN_COLS, d_model)
    return dict(
        g_mix=row(p["g_mix"]), w_in=w_in,
        conv_w=p["conv_w"], conv_b=row(p["conv_b"]), wcat=wcat, bcat=bcat, lam=lam,
        gcq=row(p["mla_g_cq"]), wq=wq, gckv=row(p["mla_g_ckv"]), wkv=wkv,
        gq=qk_gains(p["mla_g_q"]), gk=qk_gains(p["mla_g_k"]),
        wg=wg.astype(BF16), bg=bg, go=row(jnp.tile(p["gla_g_o"], GLA_HEADS)),
        wol=w_out[:W_LRU], wom=w_out[W_LRU:W_LRU + W_MLA], wog=w_out[W_LRU + W_MLA:],
        g_ffn=row(p["g_ffn"]), wgu=wgu, wdown=wdown,
    )


def _rotary_tables(seq):
    half = QK_ROPE // 2
    inv = 1.0 / (ROPE_THETA ** (jnp.arange(half, dtype=F32) * 2.0 / QK_ROPE))
    ang = jnp.arange(seq, dtype=F32)[:, None] * inv[None, :]
    cos, sin = jnp.cos(ang), jnp.sin(ang)
    pad = jnp.zeros((seq, LANES - QK_ROPE), F32)
    return jnp.concatenate([cos, cos, pad], axis=1), jnp.concatenate([-sin, sin, pad], axis=1)


def _layer(x, cos, sin, lp):
    b, seq, d = x.shape
    x2 = x.reshape(b * seq, d)
    zl, zm, zg = _in_proj(x2, lp["g_mix"], lp["w_in"])
    o_lru = _lru_mixer(zl.reshape(b, seq, ZL_W), lp["conv_w"], lp["conv_b"], lp["wcat"], lp["bcat"], lp["lam"])
    o_mla = _mla_mixer(zm.reshape(b, seq, ZM_W), cos, sin, lp["gcq"], lp["wq"], lp["gckv"], lp["wkv"],
                       lp["gq"], lp["gk"])
    o_gla = _gla_mixer(zg.reshape(b, seq, ZG_W), lp["wg"], lp["bg"], lp["go"])
    y = _out_ffn(x2, o_lru.reshape(b * seq, W_LRU), o_mla.reshape(b * seq, W_MLA),
                 o_gla.reshape(b * seq, W_GLA), lp["wol"], lp["wom"], lp["wog"], lp["g_ffn"],
                 lp["wgu"], lp["wdown"])
    return y.reshape(b, seq, d)


_PARAM_NAMES = ("g_mix", "w_in", "conv_w", "conv_b",
                "lru_wa_f", "lru_ba_f", "lru_wx_f", "lru_bx_f", "lru_lam_f",
                "lru_wa_b", "lru_ba_b", "lru_wx_b", "lru_bx_b", "lru_lam_b",
                "mla_g_cq", "mla_w_uq", "mla_g_ckv", "mla_w_ukv", "mla_g_q", "mla_g_k",
                "gla_wa2_f", "gla_ba2_f", "gla_wa2_b", "gla_ba2_b", "gla_g_o",
                "w_out", "g_ffn", "w_ffn_in", "w_ffn_out")


def kernel(x_prompt, x_sample, g_mix, w_in, conv_w, conv_b, lru_wa_f, lru_ba_f, lru_wx_f, lru_bx_f, lru_lam_f, lru_wa_b, lru_ba_b, lru_wx_b, lru_bx_b, lru_lam_b, mla_g_cq, mla_w_uq, mla_g_ckv, mla_w_ukv, mla_g_q, mla_g_k, gla_wa2_f, gla_ba2_f, gla_wa2_b, gla_ba2_b, gla_g_o, w_out, g_ffn, w_ffn_in, w_ffn_out):
    stacked = dict(zip(_PARAM_NAMES, (
        g_mix, w_in, conv_w, conv_b, lru_wa_f, lru_ba_f, lru_wx_f, lru_bx_f, lru_lam_f,
        lru_wa_b, lru_ba_b, lru_wx_b, lru_bx_b, lru_lam_b,
        mla_g_cq, mla_w_uq, mla_g_ckv, mla_w_ukv, mla_g_q, mla_g_k,
        gla_wa2_f, gla_ba2_f, gla_wa2_b, gla_ba2_b, gla_g_o, w_out, g_ffn, w_ffn_in, w_ffn_out)))
    depth = g_mix.shape[0]
    layers = [_prep_layer({k: v[l] for k, v in stacked.items()}) for l in range(depth)]

    def trunk(x):
        cos, sin = _rotary_tables(x.shape[1])
        for lp in layers:
            x = _layer(x, cos, sin, lp)
        return x

    return trunk(x_prompt), trunk(x_sample)
```

```python
import functools

import jax
import jax.numpy as jnp
import numpy as np
from jax import lax
from jax.experimental import pallas as pl
from jax.experimental.pallas import tpu as pltpu

F32 = jnp.float32
BF16 = jnp.bfloat16

NORM_EPS = 1e-6
W_LRU = 256
W_MLA = 512
W_GLA = 256
LRU_BLOCKS = 8
LRU_C = 8.0
CONV_W = 4
CONV_LEFT = 2
MLA_HEADS = 4
QK_NOPE = 128
QK_ROPE = 64
QK_HEAD = QK_NOPE + QK_ROPE
V_HEAD = 128
Q_RANK = 256
KV_RANK = 128
ROPE_THETA = 10000.0
GLA_HEADS = 4
GLA_DK = 32
GLA_DV = 64
GLA_GATE_RANK = 16
GLA_TAU = 16.0
GLA_CHUNK = 64

LANES = 128
SUBLANES = 8
VMEM_LIMIT_BYTES = 56 * 1024 * 1024

ZL_W = 2 * W_LRU
ZM_W = Q_RANK + KV_RANK + 2 * QK_ROPE
ZG_W = 2 * GLA_HEADS * GLA_DK + W_GLA + LANES + W_GLA

IN_ROWS = 1024
FFN_ROWS = 1024
FFN_COLS = 256
MLA_Q_ROWS = 512
MLA_KV_ROWS = 512
MLA_KEY_BLOCK = 512
MLA_SAFE_SHIFT = 60.0
GLA_ROWS = 256
LRU_SEGS = SUBLANES


def _dot(a, b):
    return jnp.dot(a, b, preferred_element_type=F32)


def _dot_nt(a, b):
    return lax.dot_general(a, b, (((1,), (1,)), ((), ())), preferred_element_type=F32)


def _cparams(sem):
    return pltpu.CompilerParams(dimension_semantics=sem, vmem_limit_bytes=VMEM_LIMIT_BYTES)


def _full(shape):
    return pl.BlockSpec(shape, lambda *_: (0,) * len(shape))


def _in_proj_kernel(x_ref, g_ref, w_ref, zl_ref, zm_ref, zg_ref):
    x = x_ref[...]
    ms = jnp.mean(x * x, axis=-1, keepdims=True)
    h = (x * lax.rsqrt(ms + NORM_EPS) * g_ref[...]).astype(BF16)
    zl_ref[...] = _dot(h, w_ref[:, 0:ZL_W]).astype(BF16)
    zm_ref[...] = _dot(h, w_ref[:, ZL_W:ZL_W + ZM_W]).astype(BF16)
    zg_ref[...] = _dot(h, w_ref[:, ZL_W + ZM_W:]).astype(BF16)


def _in_proj(x2, g, w):
    n, d = x2.shape
    tm = min(IN_ROWS, n)
    row = lambda i: (i, 0)
    return pl.pallas_call(
        _in_proj_kernel,
        grid=(n // tm,),
        in_specs=[pl.BlockSpec((tm, d), row), _full(g.shape), _full(w.shape)],
        out_specs=[pl.BlockSpec((tm, ZL_W), row), pl.BlockSpec((tm, ZM_W), row),
                   pl.BlockSpec((tm, ZG_W), row)],
        out_shape=[jax.ShapeDtypeStruct((n, ZL_W), BF16), jax.ShapeDtypeStruct((n, ZM_W), BF16),
                   jax.ShapeDtypeStruct((n, ZG_W), BF16)],
        compiler_params=_cparams(("arbitrary",)),
        name="in_proj",
    )(x2, g, w)


def _softplus(x):
    return jnp.maximum(x, 0.0) + jnp.log1p(jnp.exp(-jnp.abs(x)))


def _sigmoid(x):
    return 0.5 * jnp.tanh(0.5 * x) + 0.5


def _lru_kernel(z_ref, cw_ref, cb_ref, wcat_ref, bcat_ref, lam_ref, o_ref,
                u_ref, af_ref, vf_ref, ab_ref, vb_ref, c_ref):
    seq = z_ref.shape[1]
    seg = seq // LRU_SEGS
    pitch = seg + SUBLANES
    halves = W_LRU // LANES

    xin = z_ref[0, :, 0:W_LRU].astype(F32)
    row = lax.broadcasted_iota(jnp.int32, (seq, W_LRU), 0)
    cw = cw_ref[...]
    u = xin * cw[CONV_LEFT:CONV_LEFT + 1]
    for j in range(CONV_W):
        off = j - CONV_LEFT
        if off == 0:
            continue
        shifted = pltpu.roll(xin, (-off) % seq, 0)
        valid = (row >= -off) if off < 0 else (row < seq - off)
        u = u + jnp.where(valid, shifted, 0.0) * cw[j:j + 1]
    u_ref[...] = u + cb_ref[...]

    half_c_sp = (-0.5 * LRU_C) * _softplus(-lam_ref[...])

    def gates(s, _):
        r0 = pl.multiple_of(s * seg, seg)
        p0 = pl.multiple_of(s * pitch, SUBLANES)
        uc = u_ref[pl.ds(r0, seg), :]
        g = _dot(uc.astype(BF16), wcat_ref[...]) + bcat_ref[...]
        half_u = 0.5 * uc
        for d, (a_ref, v_ref) in enumerate(((af_ref, vf_ref), (ab_ref, vb_ref))):
            base = d * 2 * W_LRU
            t_r = jnp.tanh(g[:, base:base + W_LRU])
            t_i = jnp.tanh(g[:, base + W_LRU:base + 2 * W_LRU])
            log_a = t_r * half_c_sp[d:d + 1] + half_c_sp[d:d + 1]
            a = jnp.exp(log_a)
            v = jnp.sqrt(-jnp.tanh(log_a) * (a * a + 1.0)) * ((t_i + 1.0) * half_u)
            for hh in range(halves):
                a_ref[hh, pl.ds(p0, seg), :] = a[:, hh * LANES:(hh + 1) * LANES]
                v_ref[hh, pl.ds(p0, seg), :] = v[:, hh * LANES:(hh + 1) * LANES]
        return 0

    lax.fori_loop(0, LRU_SEGS, gates, 0)

    def rows_at(i):
        return pl.ds(i, LRU_SEGS, stride=pitch)

    def scan(i, carry):
        hf, pf, hb, pb = carry
        ib = seg - 1 - i
        hf_n, pf_n, hb_n, pb_n = [], [], [], []
        for hh in range(halves):
            a = af_ref[hh, rows_at(i), :]
            h = a * hf[hh] + vf_ref[hh, rows_at(i), :]
            p = a * pf[hh]
            vf_ref[hh, rows_at(i), :] = h
            af_ref[hh, rows_at(i), :] = p
            hf_n.append(h)
            pf_n.append(p)
            a = ab_ref[hh, rows_at(ib), :]
            h = a * hb[hh] + vb_ref[hh, rows_at(ib), :]
            p = a * pb[hh]
            vb_ref[hh, rows_at(ib), :] = h
            ab_ref[hh, rows_at(ib), :] = p
            hb_n.append(h)
            pb_n.append(p)
        return tuple(hf_n), tuple(pf_n), tuple(hb_n), tuple(pb_n)

    zeros = tuple(jnp.zeros((LRU_SEGS, LANES), F32) for _ in range(halves))
    ones = tuple(jnp.ones((LRU_SEGS, LANES), F32) for _ in range(halves))
    hf_e, pf_e, hb_e, pb_e = lax.fori_loop(0, seg, scan, (zeros, ones, zeros, ones), unroll=2)

    srow = lax.broadcasted_iota(jnp.int32, (LRU_SEGS, LANES), 0)
    for hh in range(halves):
        c = jnp.zeros((LRU_SEGS, LANES), F32)
        for _ in range(LRU_SEGS - 1):
            c = jnp.where(srow == 0, 0.0, pltpu.roll(hf_e[hh] + pf_e[hh] * c, 1, 0))
        c_ref[0, hh] = c
        c = jnp.zeros((LRU_SEGS, LANES), F32)
        for _ in range(LRU_SEGS - 1):
            c = jnp.where(srow == LRU_SEGS - 1, 0.0,
                          pltpu.roll(hb_e[hh] + pb_e[hh] * c, LRU_SEGS - 1, 0))
        c_ref[1, hh] = c

    def finish(s, _):
        r0 = pl.multiple_of(s * seg, seg)
        p0 = pl.multiple_of(s * pitch, SUBLANES)
        for hh in range(halves):
            h = (vf_ref[hh, pl.ds(p0, seg), :] + af_ref[hh, pl.ds(p0, seg), :] * c_ref[0, hh, pl.ds(s, 1), :]
                 + vb_ref[hh, pl.ds(p0, seg), :] + ab_ref[hh, pl.ds(p0, seg), :] * c_ref[1, hh, pl.ds(s, 1), :])
            gate = z_ref[0, pl.ds(r0, seg), W_LRU + hh * LANES:W_LRU + (hh + 1) * LANES].astype(F32)
            o_ref[0, pl.ds(r0, seg), hh * LANES:(hh + 1) * LANES] = (h * jax.nn.gelu(gate)).astype(BF16)
        return 0

    lax.fori_loop(0, LRU_SEGS, finish, 0)


def _lru_mixer(zl, cw, cb, wcat, bcat, lam):
    b, seq, _ = zl.shape
    pitch = seq // LRU_SEGS + SUBLANES
    scan = pltpu.VMEM((W_LRU // LANES, LRU_SEGS * pitch, LANES), F32)
    return pl.pallas_call(
        _lru_kernel,
        grid=(b,),
        in_specs=[pl.BlockSpec((1, seq, ZL_W), lambda i: (i, 0, 0)), _full(cw.shape), _full(cb.shape),
                  _full(wcat.shape), _full(bcat.shape), _full(lam.shape)],
        out_specs=pl.BlockSpec((1, seq, W_LRU), lambda i: (i, 0, 0)),
        out_shape=jax.ShapeDtypeStruct((b, seq, W_LRU), BF16),
        scratch_shapes=[pltpu.VMEM((seq, W_LRU), F32), scan, scan, scan, scan,
                        pltpu.VMEM((2, W_LRU // LANES, LRU_SEGS, LANES), F32)],
        compiler_params=_cparams(("arbitrary",)),
        name="lru_mixer",
    )(zl, cw, cb, wcat, bcat, lam)


def _rms(x, g):
    ms = jnp.mean(x * x, axis=-1, keepdims=True)
    return x * lax.rsqrt(ms + NORM_EPS) * g


def _split_bf16(x):
    hi = x.astype(BF16)
    lo = (x - hi.astype(F32)).astype(BF16)
    return hi, lo


def _group_sums(x, ones):
    return _dot(x.astype(BF16), ones)


def _block_ones(n, block):
    ri = lax.broadcasted_iota(jnp.int32, (n, n), 0) // block
    ci = lax.broadcasted_iota(jnp.int32, (n, n), 1) // block
    return jnp.where(ri == ci, 1.0, 0.0).astype(BF16)


def _rope_part(t, cos, sin, gains):
    return t * cos * gains[1:2] + pltpu.roll(t, QK_ROPE, 1) * sin * gains[2:3]


def _mla_kernel(z_ref, cos_ref, sin_ref, gcq_ref, wq_ref, gckv_ref, wkv_ref, gq_ref, gk_ref,
                o_ref, k_scr, v_scr, q_scr):
    seq = z_ref.shape[1]
    qi = pl.program_id(1)
    c_kv0 = Q_RANK
    c_rope0 = Q_RANK + KV_RANK
    head_w = QK_NOPE + 2 * QK_ROPE
    ones_pair = _block_ones(2 * LANES, LANES)

    @pl.when(qi == 0)
    def _():
        gk = gk_ref[...]
        for c in range(seq // MLA_KV_ROWS):
            rows = slice(c * MLA_KV_ROWS, (c + 1) * MLA_KV_ROWS)
            ckv = _rms(z_ref[0, rows, c_kv0:c_rope0].astype(F32), gckv_ref[...])
            kv = _dot(ckv.astype(BF16), wkv_ref[...])
            t = z_ref[0, rows, c_rope0:c_rope0 + 2 * QK_ROPE].astype(F32)
            t_sq = 0.5 * t * t
            rope = _rope_part(t, cos_ref[rows, :], sin_ref[rows, :], gk)
            for h in range(MLA_HEADS):
                kn = kv[:, h * QK_NOPE:(h + 1) * QK_NOPE]
                ss = jnp.sum(kn * kn + t_sq, axis=-1, keepdims=True)
                inv = lax.rsqrt(ss * (1.0 / QK_HEAD) + NORM_EPS)
                k_scr[h, rows, 0:QK_NOPE] = (kn * inv * gk[0:1]).astype(BF16)
                k_scr[h, rows, QK_NOPE:head_w] = (rope * inv).astype(BF16)
                v0 = MLA_HEADS * QK_NOPE + h * V_HEAD
                v_scr[h, rows, 0:V_HEAD] = kv[:, v0:v0 + V_HEAD].astype(BF16)
                v_scr[h, rows, V_HEAD:2 * V_HEAD] = jnp.ones((MLA_KV_ROWS, V_HEAD), BF16)

    gq = gq_ref[...]
    scale = QK_HEAD ** -0.5 * np.log2(np.e)

    def prep_queries(tile, slot):
        qrows = pl.ds(pl.multiple_of(tile * MLA_Q_ROWS, MLA_Q_ROWS), MLA_Q_ROWS)
        cq = z_ref[0, qrows, 0:Q_RANK].astype(F32)
        ms = _group_sums(cq * cq, jnp.ones((Q_RANK, Q_RANK), BF16)) * (1.0 / Q_RANK)
        cq = cq * lax.rsqrt(ms + NORM_EPS) * gcq_ref[...]
        qall = _dot(cq.astype(BF16), wq_ref[...])
        cos_g = cos_ref[qrows, :] * gq[1:2]
        sin_g = sin_ref[qrows, :] * gq[2:3]
        sq = lambda h: (qall[:, h * head_w:h * head_w + QK_NOPE] ** 2
                        + 0.5 * qall[:, h * head_w + QK_NOPE:(h + 1) * head_w] ** 2)
        for h in range(MLA_HEADS):
            qn = qall[:, h * head_w:h * head_w + QK_NOPE]
            t = qall[:, h * head_w + QK_NOPE:(h + 1) * head_w]
            if h % 2 == 0:
                ss2 = _group_sums(jnp.concatenate([sq(h), sq(h + 1)], axis=-1), ones_pair)
            ss = ss2[:, (h % 2) * LANES:(h % 2 + 1) * LANES]
            inv = lax.rsqrt(ss * (1.0 / QK_HEAD) + NORM_EPS) * scale
            rope = t * cos_g + pltpu.roll(t, QK_ROPE, 1) * sin_g
            q_scr[slot, h] = jnp.concatenate([qn * inv * gq[0:1], rope * inv], axis=-1).astype(BF16)

    n_tiles = seq // MLA_Q_ROWS
    slot = qi % 2

    @pl.when(qi == 0)
    def _():
        prep_queries(0, 0)

    def prep_next():
        prep_queries(jnp.minimum(qi + 1, n_tiles - 1), 1 - slot)

    def write_head(h, ov):
        o_ref[0, :, h * V_HEAD:(h + 1) * V_HEAD] = (ov[:, 0:V_HEAD] / ov[:, V_HEAD:]).astype(BF16)

    g_max = lambda g: jnp.max(jnp.abs(g[0:2]))
    bound = scale * QK_HEAD * g_max(gq) * g_max(gk_ref[...])
    shift_is_safe = bound <= MLA_SAFE_SHIFT

    @pl.when(shift_is_safe)
    def _():
        for h in range(MLA_HEADS):
            ov = jnp.zeros((MLA_Q_ROWS, 2 * V_HEAD), F32)
            for kb in range(seq // MLA_KEY_BLOCK):
                keys = slice(kb * MLA_KEY_BLOCK, (kb + 1) * MLA_KEY_BLOCK)
                s = _dot_nt(q_scr[slot, h], k_scr[h, keys, :])
                ov = ov + _dot(jnp.exp2(s - bound).astype(BF16), v_scr[h, keys, :])
            write_head(h, ov)
        prep_next()

    @pl.when(jnp.logical_not(shift_is_safe))
    def _():
        for h in range(MLA_HEADS):
            s = _dot_nt(q_scr[slot, h], k_scr[h])
            m = jnp.max(s, axis=-1, keepdims=True)
            write_head(h, _dot(jnp.exp2(s - m).astype(BF16), v_scr[h]))
        prep_next()


def _mla_mixer(zm, cos, sin, gcq, wq, gckv, wkv, gq, gk):
    b, seq, _ = zm.shape
    head_w = QK_NOPE + 2 * QK_ROPE
    return pl.pallas_call(
        _mla_kernel,
        grid=(b, seq // MLA_Q_ROWS),
        in_specs=[pl.BlockSpec((1, seq, ZM_W), lambda i, j: (i, 0, 0)), _full(cos.shape), _full(sin.shape),
                  _full(gcq.shape), _full(wq.shape), _full(gckv.shape), _full(wkv.shape),
                  _full(gq.shape), _full(gk.shape)],
        out_specs=pl.BlockSpec((1, MLA_Q_ROWS, W_MLA), lambda i, j: (i, j, 0)),
        out_shape=jax.ShapeDtypeStruct((b, seq, W_MLA), BF16),
        scratch_shapes=[pltpu.VMEM((MLA_HEADS, seq, head_w), BF16),
                        pltpu.VMEM((MLA_HEADS, seq, 2 * V_HEAD), BF16),
                        pltpu.VMEM((2, MLA_HEADS, MLA_Q_ROWS, head_w), BF16)],
        compiler_params=_cparams(("arbitrary", "arbitrary")),
        name="mla_mixer",
    )(zm, cos, sin, gcq, wq, gckv, wkv, gq, gk)


def _log_sigmoid(x):
    return jnp.minimum(x, 0.0) - jnp.log(1.0 + jnp.exp(-jnp.abs(x)))


def _gla_kernel(z_ref, wg_ref, bg_ref, go_ref, o_ref, oi_scr, qd_scr, ki_scr, ke_scr, u_scr, dec_scr, s_scr):
    seq = z_ref.shape[1]
    n_groups = seq // GLA_ROWS
    chunks = GLA_ROWS // GLA_CHUNK
    qk_w = GLA_HEADS * GLA_DK
    k0, v0, g0, og0 = qk_w, 2 * qk_w, 2 * qk_w + W_GLA, 2 * qk_w + W_GLA + LANES

    ri = lax.broadcasted_iota(jnp.int32, (GLA_ROWS, GLA_ROWS), 0)
    ci = lax.broadcasted_iota(jnp.int32, (GLA_ROWS, GLA_ROWS), 1)
    same_chunk = (ri // GLA_CHUNK) == (ci // GLA_CHUNK)
    causal = (jnp.where(same_chunk, ci, GLA_ROWS) <= ri, jnp.where(same_chunk, ci, -1) >= ri)
    cum_mat = jnp.where(causal[0], 1.0, 0.0).astype(BF16)
    avg_mat = jnp.where((ri // GLA_DV) == (ci // GLA_DV), 1.0 / GLA_DV, 0.0).astype(BF16)
    qk_head = lax.broadcasted_iota(jnp.int32, (GLA_ROWS, qk_w), 1) // GLA_DK
    v_head = lax.broadcasted_iota(jnp.int32, (GLA_ROWS, W_GLA), 1) // GLA_DV
    st_mask = (lax.broadcasted_iota(jnp.int32, (W_GLA, qk_w), 0) // GLA_DV
               == lax.broadcasted_iota(jnp.int32, (W_GLA, qk_w), 1) // GLA_DK)
    col_chunk = ci // GLA_CHUNK
    qk_chunk = lax.broadcasted_iota(jnp.int32, (GLA_ROWS, qk_w), 0) // GLA_CHUNK

    def rows_of(g):
        return pl.ds(pl.multiple_of(g * GLA_ROWS, GLA_ROWS), GLA_ROWS)

    def decays(g, _):
        rows = rows_of(g)
        x = _dot(z_ref[0, rows, g0:g0 + LANES], wg_ref[...]) + bg_ref[...]
        la = _log_sigmoid(x) * (1.0 / GLA_TAU)
        la_hi, la_lo = _split_bf16(la)
        b_pre = _dot(cum_mat, la_hi) + _dot(cum_mat, la_lo)
        ends = [b_pre[(c + 1) * GLA_CHUNK - 1:(c + 1) * GLA_CHUNK] for c in range(chunks)]
        bt = jnp.concatenate([jnp.broadcast_to(e, (GLA_CHUNK, 2 * qk_w)) for e in ends], axis=0)
        b_dir = (b_pre[:, 0:qk_w], bt[:, qk_w:] - b_pre[:, qk_w:] + la[:, qk_w:])
        q = z_ref[0, rows, 0:qk_w].astype(F32) * (GLA_DK ** -0.5)
        k = z_ref[0, rows, k0:k0 + qk_w].astype(F32)
        for d in range(2):
            b = b_dir[d]
            cols = slice(d * qk_w, (d + 1) * qk_w)
            qd_scr[rows, cols] = (q * jnp.exp(b)).astype(BF16)
            ki_scr[rows, cols] = (k * jnp.exp(-b)).astype(BF16)
            ke_scr[rows, cols] = (k * jnp.exp(bt[:, cols] - b)).astype(BF16)
        dec_scr[g] = jnp.exp(jnp.concatenate(ends + ends, axis=0))
        return 0

    lax.fori_loop(0, n_groups, decays, 0, unroll=True)

    def local(g, _):
        rows = rows_of(g)
        v = z_ref[0, rows, v0:v0 + W_GLA]
        probs = None
        for d in range(2):
            cols = slice(d * qk_w, (d + 1) * qk_w)
            q_dec = qd_scr[rows, cols]
            q_stack = jnp.concatenate([jnp.where(qk_head == h, q_dec, jnp.zeros_like(q_dec))
                                       for h in range(GLA_HEADS)], axis=0)
            s = _dot_nt(q_stack, ki_scr[rows, cols])
            p = [jnp.where(causal[d], s[h * GLA_ROWS:(h + 1) * GLA_ROWS], 0.0) for h in range(GLA_HEADS)]
            probs = p if probs is None else [a + c for a, c in zip(probs, p)]
        v_stack = jnp.concatenate([jnp.where(v_head == h, v, jnp.zeros_like(v)) for h in range(GLA_HEADS)],
                                  axis=0)
        oi_scr[rows, :] = _dot(jnp.concatenate([p.astype(BF16) for p in probs], axis=1), v_stack)
        v_t = v.astype(F32).T
        vt_stack = jnp.concatenate([jnp.where(col_chunk == c, v_t, 0.0) for c in range(chunks)], axis=0)
        u_scr[g] = _dot(vt_stack.astype(BF16), ke_scr[rows, :])
        return 0

    lax.fori_loop(0, n_groups, local, 0)

    for d in range(2):
        def step(i, st, d=d):
            g = i if d == 0 else n_groups - 1 - i
            dec = dec_scr[g]
            for c in (range(chunks) if d == 0 else reversed(range(chunks))):
                blk = (d * chunks + c) * qk_w
                s_scr[g, :, blk:blk + qk_w] = st.astype(BF16)
                upd = u_scr[g, c * W_GLA:(c + 1) * W_GLA, d * qk_w:(d + 1) * qk_w]
                st = st * dec[c:c + 1, d * qk_w:(d + 1) * qk_w] + jnp.where(st_mask, upd, 0.0)
            return st

        lax.fori_loop(0, n_groups, step, jnp.zeros((W_GLA, qk_w), F32))

    def finish(g, _):
        rows = rows_of(g)
        qd = qd_scr[rows, :]
        lhs = jnp.concatenate([jnp.where(qk_chunk == c, qd[:, d * qk_w:(d + 1) * qk_w],
                                         jnp.zeros((GLA_ROWS, qk_w), BF16))
                               for d in range(2) for c in range(chunks)], axis=1)
        o = oi_scr[rows, :] + _dot_nt(lhs, s_scr[g])
        sq_hi, sq_lo = _split_bf16(o * o)
        ms = _dot(sq_hi, avg_mat) + _dot(sq_lo, avg_mat)
        og = z_ref[0, rows, og0:og0 + W_GLA].astype(F32)
        y = (o * lax.rsqrt(ms + NORM_EPS) * go_ref[...]) * (og * _sigmoid(og))
        o_ref[0, rows, :] = y.astype(BF16)
        return 0

    lax.fori_loop(0, n_groups, finish, 0, unroll=True)


def _gla_mixer(zg, wg, bg, go):
    b, seq, _ = zg.shape
    qk_w = GLA_HEADS * GLA_DK
    n_groups = seq // GLA_ROWS
    chunks = GLA_ROWS // GLA_CHUNK
    return pl.pallas_call(
        _gla_kernel,
        grid=(b,),
        in_specs=[pl.BlockSpec((1, seq, ZG_W), lambda i: (i, 0, 0)), _full(wg.shape), _full(bg.shape),
                  _full(go.shape)],
        out_specs=pl.BlockSpec((1, seq, W_GLA), lambda i: (i, 0, 0)),
        out_shape=jax.ShapeDtypeStruct((b, seq, W_GLA), BF16),
        scratch_shapes=[pltpu.VMEM((seq, W_GLA), F32),
                        pltpu.VMEM((seq, 2 * qk_w), BF16),
                        pltpu.VMEM((seq, 2 * qk_w), BF16),
                        pltpu.VMEM((seq, 2 * qk_w), BF16),
                        pltpu.VMEM((n_groups, chunks * W_GLA, 2 * qk_w), F32),
                        pltpu.VMEM((n_groups, 2 * chunks, 2 * qk_w), F32),
                        pltpu.VMEM((n_groups, W_GLA, 2 * chunks * qk_w), BF16)],
        compiler_params=_cparams(("arbitrary",)),
        name="gla_mixer",
    )(zg, wg, bg, go)


def _out_ffn_kernel(x_ref, ol_ref, om_ref, og_ref, wol_ref, wom_ref, wog_ref, g_ref,
                    wgu_ref, wdown_ref, o_ref, h_ref):
    x1 = (x_ref[...] + _dot(ol_ref[...], wol_ref[...]) + _dot(om_ref[...], wom_ref[...])
          + _dot(og_ref[...], wog_ref[...]))
    o_ref[...] = x1
    h_ref[...] = _rms(x1, g_ref[...]).astype(BF16)

    def hidden_chunk(j, _):
        gu = _dot(h_ref[...], wgu_ref[j])
        gate = gu[:, 0:FFN_COLS]
        act = (gate * jax.nn.sigmoid(gate) * gu[:, FFN_COLS:]).astype(BF16)
        o_ref[...] += _dot(act, wdown_ref[j])
        return 0

    lax.fori_loop(0, wgu_ref.shape[0], hidden_chunk, 0)


def _out_ffn(x2, ol, om, og, wol, wom, wog, g, wgu, wdown):
    n, d = x2.shape
    tm = min(FFN_ROWS, n)
    row = lambda i: (i, 0)
    resident = lambda a: pl.BlockSpec(a.shape, lambda i: (0,) * a.ndim, pipeline_mode=pl.Buffered(1))
    return pl.pallas_call(
        _out_ffn_kernel,
        grid=(n // tm,),
        in_specs=[pl.BlockSpec((tm, d), row), pl.BlockSpec((tm, W_LRU), row),
                  pl.BlockSpec((tm, W_MLA), row), pl.BlockSpec((tm, W_GLA), row),
                  resident(wol), resident(wom), resident(wog), resident(g),
                  resident(wgu), resident(wdown)],
        out_specs=pl.BlockSpec((tm, d), row),
        out_shape=jax.ShapeDtypeStruct((n, d), F32),
        scratch_shapes=[pltpu.VMEM((tm, d), BF16)],
        compiler_params=_cparams(("arbitrary",)),
        name="out_ffn",
    )(x2, ol, om, og, wol, wom, wog, g, wgu, wdown)


def _block_diag(w):
    nb, bi, bo = w.shape
    eye = jnp.eye(nb, dtype=w.dtype)
    return (w[:, :, None, :] * eye[:, None, :, None]).reshape(nb * bi, nb * bo)


def _swap_halves(n):
    return np.concatenate([np.arange(n // 2, n), np.arange(0, n // 2)])


def _prep_layer(p):
    row = lambda v: v.reshape(1, -1).astype(F32)
    d_model = p["w_in"].shape[0]
    qk_w = GLA_HEADS * GLA_DK
    o_cq = 2 * W_LRU
    o_ckv = o_cq + Q_RANK
    o_rope = o_ckv + KV_RANK
    o_glaq = o_rope + QK_ROPE
    o_gf = o_glaq + 2 * qk_w + W_GLA
    o_og = o_gf + 2 * GLA_GATE_RANK
    sw = _swap_halves(QK_ROPE)
    w = p["w_in"]
    w_in = jnp.concatenate([
        w[:, :o_glaq], w[:, o_rope + sw],
        w[:, o_glaq:o_og], jnp.zeros((d_model, LANES - 2 * GLA_GATE_RANK), w.dtype), w[:, o_og:],
    ], axis=1).astype(BF16)

    wcat = (0.5 * jnp.concatenate([_block_diag(p[k]) for k in ("lru_wa_f", "lru_wx_f", "lru_wa_b", "lru_wx_b")],
                                  axis=1)).astype(BF16)
    bcat = 0.5 * jnp.concatenate([p[k] for k in ("lru_ba_f", "lru_bx_f", "lru_ba_b", "lru_bx_b")]).reshape(1, -1)
    lam = jnp.stack([p["lru_lam_f"], p["lru_lam_b"]])

    wq = p["mla_w_uq"].reshape(Q_RANK, MLA_HEADS, QK_HEAD)
    wq = jnp.concatenate([wq, wq[:, :, QK_NOPE + sw]], axis=2).reshape(Q_RANK, -1).astype(BF16)
    wkv = p["mla_w_ukv"].reshape(KV_RANK, MLA_HEADS, QK_NOPE + V_HEAD)
    wkv = jnp.concatenate([wkv[:, :, :QK_NOPE].reshape(KV_RANK, -1),
                           wkv[:, :, QK_NOPE:].reshape(KV_RANK, -1)], axis=1).astype(BF16)

    def qk_gains(g):
        pad = jnp.zeros((LANES - QK_ROPE,), F32)
        return jnp.stack([g[:QK_NOPE], jnp.concatenate([g[QK_NOPE:], pad]),
                          jnp.concatenate([g[QK_NOPE + sw], pad])]).astype(F32)

    wg = jnp.zeros((LANES, 2 * qk_w), F32)
    wg = wg.at[:GLA_GATE_RANK, :qk_w].set(p["gla_wa2_f"])
    wg = wg.at[GLA_GATE_RANK:2 * GLA_GATE_RANK, qk_w:].set(p["gla_wa2_b"])
    bg = jnp.concatenate([p["gla_ba2_f"], p["gla_ba2_b"]]).reshape(1, -1)

    w_out = p["w_out"].astype(BF16)
    d_ff = p["w_ffn_out"].shape[0]
    n_f = d_ff // FFN_COLS
    wgu = p["w_ffn_in"].astype(BF16).reshape(d_model, 2, n_f, FFN_COLS)
    wgu = wgu.transpose(2, 0, 1, 3).reshape(n_f, d_model, 2 * FFN_COLS)
    wdown = p["w_ffn_out"].astype(BF16).reshape(n_f, FFN_COLS, d_model)
    return dict(
        g_mix=row(p["g_mix"]), w_in=w_in,
        conv_w=p["conv_w"], conv_b=row(p["conv_b"]), wcat=wcat, bcat=bcat, lam=lam,
        gcq=row(p["mla_g_cq"]), wq=wq, gckv=row(p["mla_g_ckv"]), wkv=wkv,
        gq=qk_gains(p["mla_g_q"]), gk=qk_gains(p["mla_g_k"]),
        wg=wg.astype(BF16), bg=bg, go=row(jnp.tile(p["gla_g_o"], GLA_HEADS)),
        wol=w_out[:W_LRU], wom=w_out[W_LRU:W_LRU + W_MLA], wog=w_out[W_LRU + W_MLA:],
        g_ffn=row(p["g_ffn"]), wgu=wgu, wdown=wdown,
    )


def _rotary_tables(seq):
    half = QK_ROPE // 2
    inv = 1.0 / (ROPE_THETA ** (jnp.arange(half, dtype=F32) * 2.0 / QK_ROPE))
    ang = jnp.arange(seq, dtype=F32)[:, None] * inv[None, :]
    cos, sin = jnp.cos(ang), jnp.sin(ang)
    pad = jnp.zeros((seq, LANES - QK_ROPE), F32)
    return jnp.concatenate([cos, cos, pad], axis=1), jnp.concatenate([-sin, sin, pad], axis=1)


def _layer(x, cos, sin, lp):
    b, seq, d = x.shape
    x2 = x.reshape(b * seq, d)
    zl, zm, zg = _in_proj(x2, lp["g_mix"], lp["w_in"])
    o_lru = _lru_mixer(zl.reshape(b, seq, ZL_W), lp["conv_w"], lp["conv_b"], lp["wcat"], lp["bcat"], lp["lam"])
    o_mla = _mla_mixer(zm.reshape(b, seq, ZM_W), cos, sin, lp["gcq"], lp["wq"], lp["gckv"], lp["wkv"],
                       lp["gq"], lp["gk"])
    o_gla = _gla_mixer(zg.reshape(b, seq, ZG_W), lp["wg"], lp["bg"], lp["go"])
    y = _out_ffn(x2, o_lru.reshape(b * seq, W_LRU), o_mla.reshape(b * seq, W_MLA),
                 o_gla.reshape(b * seq, W_GLA), lp["wol"], lp["wom"], lp["wog"], lp["g_ffn"],
                 lp["wgu"], lp["wdown"])
    return y.reshape(b, seq, d)


_PARAM_NAMES = ("g_mix", "w_in", "conv_w", "conv_b",
                "lru_wa_f", "lru_ba_f", "lru_wx_f", "lru_bx_f", "lru_lam_f",
                "lru_wa_b", "lru_ba_b", "lru_wx_b", "lru_bx_b", "lru_lam_b",
                "mla_g_cq", "mla_w_uq", "mla_g_ckv", "mla_w_ukv", "mla_g_q", "mla_g_k",
                "gla_wa2_f", "gla_ba2_f", "gla_wa2_b", "gla_ba2_b", "gla_g_o",
                "w_out", "g_ffn", "w_ffn_in", "w_ffn_out")


def kernel(x_prompt, x_sample, g_mix, w_in, conv_w, conv_b, lru_wa_f, lru_ba_f, lru_wx_f, lru_bx_f, lru_lam_f, lru_wa_b, lru_ba_b, lru_wx_b, lru_bx_b, lru_lam_b, mla_g_cq, mla_w_uq, mla_g_ckv, mla_w_ukv, mla_g_q, mla_g_k, gla_wa2_f, gla_ba2_f, gla_wa2_b, gla_ba2_b, gla_g_o, w_out, g_ffn, w_ffn_in, w_ffn_out):
    stacked = dict(zip(_PARAM_NAMES, (
        g_mix, w_in, conv_w, conv_b, lru_wa_f, lru_ba_f, lru_wx_f, lru_bx_f, lru_lam_f,
        lru_wa_b, lru_ba_b, lru_wx_b, lru_bx_b, lru_lam_b,
        mla_g_cq, mla_w_uq, mla_g_ckv, mla_w_ukv, mla_g_q, mla_g_k,
        gla_wa2_f, gla_ba2_f, gla_wa2_b, gla_ba2_b, gla_g_o, w_out, g_ffn, w_ffn_in, w_ffn_out)))
    depth = g_mix.shape[0]
    layers = [_prep_layer({k: v[l] for k, v in stacked.items()}) for l in range(depth)]

    def trunk(x):
        cos, sin = _rotary_tables(x.shape[1])
        for lp in layers:
            x = _layer(x, cos, sin, lp)
        return x

    return trunk(x_prompt), trunk(x_sample)
```

```python
import functools

import jax
import jax.numpy as jnp
import numpy as np
from jax import lax
from jax.experimental import pallas as pl
from jax.experimental.pallas import tpu as pltpu

F32 = jnp.float32
BF16 = jnp.bfloat16

NORM_EPS = 1e-6
W_LRU = 256
W_MLA = 512
W_GLA = 256
LRU_BLOCKS = 8
LRU_C = 8.0
CONV_W = 4
CONV_LEFT = 2
MLA_HEADS = 4
QK_NOPE = 128
QK_ROPE = 64
QK_HEAD = QK_NOPE + QK_ROPE
V_HEAD = 128
Q_RANK = 256
KV_RANK = 128
ROPE_THETA = 10000.0
GLA_HEADS = 4
GLA_DK = 32
GLA_DV = 64
GLA_GATE_RANK = 16
GLA_TAU = 16.0
GLA_CHUNK = 64

LANES = 128
SUBLANES = 8
VMEM_LIMIT_BYTES = 56 * 1024 * 1024

ZL_W = 2 * W_LRU
ZM_W = Q_RANK + KV_RANK + 2 * QK_ROPE
ZG_W = 2 * GLA_HEADS * GLA_DK + W_GLA + LANES + W_GLA

IN_ROWS = 1024
IN_SUB_ROWS = 256
FFN_ROWS = 1024
FFN_COLS = 256
MLA_Q_ROWS = 512
MLA_KV_ROWS = 512
MLA_KEY_BLOCK = 512
MLA_SAFE_SHIFT = 60.0
GLA_ROWS = 256
LRU_SEGS = SUBLANES


def _dot(a, b):
    return jnp.dot(a, b, preferred_element_type=F32)


def _dot_nt(a, b):
    return lax.dot_general(a, b, (((1,), (1,)), ((), ())), preferred_element_type=F32)


def _cparams(sem):
    return pltpu.CompilerParams(dimension_semantics=sem, vmem_limit_bytes=VMEM_LIMIT_BYTES)


def _full(shape):
    return pl.BlockSpec(shape, lambda *_: (0,) * len(shape))


def _in_proj_kernel(x_ref, g_ref, w_ref, zl_ref, zm_ref, zg_ref):
    for r in range(x_ref.shape[0] // IN_SUB_ROWS):
        rows = slice(r * IN_SUB_ROWS, (r + 1) * IN_SUB_ROWS)
        h = _rms(x_ref[rows, :], g_ref[...]).astype(BF16)
        zl_ref[rows, :] = _dot(h, w_ref[:, 0:ZL_W]).astype(BF16)
        zm_ref[rows, :] = _dot(h, w_ref[:, ZL_W:ZL_W + ZM_W]).astype(BF16)
        zg_ref[rows, :] = _dot(h, w_ref[:, ZL_W + ZM_W:]).astype(BF16)


def _in_proj(x2, g, w):
    n, d = x2.shape
    tm = min(IN_ROWS, n)
    row = lambda i: (i, 0)
    return pl.pallas_call(
        _in_proj_kernel,
        grid=(n // tm,),
        in_specs=[pl.BlockSpec((tm, d), row), _full(g.shape), _full(w.shape)],
        out_specs=[pl.BlockSpec((tm, ZL_W), row), pl.BlockSpec((tm, ZM_W), row),
                   pl.BlockSpec((tm, ZG_W), row)],
        out_shape=[jax.ShapeDtypeStruct((n, ZL_W), BF16), jax.ShapeDtypeStruct((n, ZM_W), BF16),
                   jax.ShapeDtypeStruct((n, ZG_W), BF16)],
        compiler_params=_cparams(("arbitrary",)),
        name="in_proj",
    )(x2, g, w)


def _softplus(x):
    return jnp.maximum(x, 0.0) + jnp.log1p(jnp.exp(-jnp.abs(x)))


def _sigmoid(x):
    return 0.5 * jnp.tanh(0.5 * x) + 0.5


def _lru_kernel(z_ref, cw_ref, cb_ref, wcat_ref, bcat_ref, lam_ref, o_ref,
                u_ref, af_ref, vf_ref, ab_ref, vb_ref, c_ref):
    seq = z_ref.shape[1]
    seg = seq // LRU_SEGS
    pitch = seg + SUBLANES
    halves = W_LRU // LANES

    xin = z_ref[0, :, 0:W_LRU].astype(F32)
    row = lax.broadcasted_iota(jnp.int32, (seq, W_LRU), 0)
    cw = cw_ref[...]
    u = xin * cw[CONV_LEFT:CONV_LEFT + 1]
    for j in range(CONV_W):
        off = j - CONV_LEFT
        if off == 0:
            continue
        shifted = pltpu.roll(xin, (-off) % seq, 0)
        valid = (row >= -off) if off < 0 else (row < seq - off)
        u = u + jnp.where(valid, shifted, 0.0) * cw[j:j + 1]
    u_ref[...] = u + cb_ref[...]

    half_c_sp = (-0.5 * LRU_C) * _softplus(-lam_ref[...])

    def gates(s, _):
        r0 = pl.multiple_of(s * seg, seg)
        p0 = pl.multiple_of(s * pitch, SUBLANES)
        uc = u_ref[pl.ds(r0, seg), :]
        g = _dot(uc.astype(BF16), wcat_ref[...]) + bcat_ref[...]
        half_u = 0.5 * uc
        for d, (a_ref, v_ref) in enumerate(((af_ref, vf_ref), (ab_ref, vb_ref))):
            base = d * 2 * W_LRU
            t_r = jnp.tanh(g[:, base:base + W_LRU])
            t_i = jnp.tanh(g[:, base + W_LRU:base + 2 * W_LRU])
            log_a = t_r * half_c_sp[d:d + 1] + half_c_sp[d:d + 1]
            a = jnp.exp(log_a)
            v = jnp.sqrt(-jnp.tanh(log_a) * (a * a + 1.0)) * ((t_i + 1.0) * half_u)
            for hh in range(halves):
                a_ref[hh, pl.ds(p0, seg), :] = a[:, hh * LANES:(hh + 1) * LANES]
                v_ref[hh, pl.ds(p0, seg), :] = v[:, hh * LANES:(hh + 1) * LANES]
        return 0

    lax.fori_loop(0, LRU_SEGS, gates, 0, unroll=4)

    def rows_at(i):
        return pl.ds(i, LRU_SEGS, stride=pitch)

    def scan(i, carry):
        hf, pf, hb, pb = carry
        ib = seg - 1 - i
        hf_n, pf_n, hb_n, pb_n = [], [], [], []
        for hh in range(halves):
            a = af_ref[hh, rows_at(i), :]
            h = a * hf[hh] + vf_ref[hh, rows_at(i), :]
            p = a * pf[hh]
            vf_ref[hh, rows_at(i), :] = h
            af_ref[hh, rows_at(i), :] = p
            hf_n.append(h)
            pf_n.append(p)
            a = ab_ref[hh, rows_at(ib), :]
            h = a * hb[hh] + vb_ref[hh, rows_at(ib), :]
            p = a * pb[hh]
            vb_ref[hh, rows_at(ib), :] = h
            ab_ref[hh, rows_at(ib), :] = p
            hb_n.append(h)
            pb_n.append(p)
        return tuple(hf_n), tuple(pf_n), tuple(hb_n), tuple(pb_n)

    zeros = tuple(jnp.zeros((LRU_SEGS, LANES), F32) for _ in range(halves))
    ones = tuple(jnp.ones((LRU_SEGS, LANES), F32) for _ in range(halves))
    hf_e, pf_e, hb_e, pb_e = lax.fori_loop(0, seg, scan, (zeros, ones, zeros, ones), unroll=4)

    srow = lax.broadcasted_iota(jnp.int32, (LRU_SEGS, LANES), 0)
    for hh in range(halves):
        c = jnp.zeros((LRU_SEGS, LANES), F32)
        for _ in range(LRU_SEGS - 1):
            c = jnp.where(srow == 0, 0.0, pltpu.roll(hf_e[hh] + pf_e[hh] * c, 1, 0))
        c_ref[0, hh] = c
        c = jnp.zeros((LRU_SEGS, LANES), F32)
        for _ in range(LRU_SEGS - 1):
            c = jnp.where(srow == LRU_SEGS - 1, 0.0,
                          pltpu.roll(hb_e[hh] + pb_e[hh] * c, LRU_SEGS - 1, 0))
        c_ref[1, hh] = c

    def finish(s, _):
        r0 = pl.multiple_of(s * seg, seg)
        p0 = pl.multiple_of(s * pitch, SUBLANES)
        for hh in range(halves):
            h = (vf_ref[hh, pl.ds(p0, seg), :] + af_ref[hh, pl.ds(p0, seg), :] * c_ref[0, hh, pl.ds(s, 1), :]
                 + vb_ref[hh, pl.ds(p0, seg), :] + ab_ref[hh, pl.ds(p0, seg), :] * c_ref[1, hh, pl.ds(s, 1), :])
            gate = z_ref[0, pl.ds(r0, seg), W_LRU + hh * LANES:W_LRU + (hh + 1) * LANES].astype(F32)
            o_ref[0, pl.ds(r0, seg), hh * LANES:(hh + 1) * LANES] = (h * jax.nn.gelu(gate)).astype(BF16)
        return 0

    lax.fori_loop(0, LRU_SEGS, finish, 0, unroll=2)


def _lru_mixer(zl, cw, cb, wcat, bcat, lam):
    b, seq, _ = zl.shape
    pitch = seq // LRU_SEGS + SUBLANES
    scan = pltpu.VMEM((W_LRU // LANES, LRU_SEGS * pitch, LANES), F32)
    return pl.pallas_call(
        _lru_kernel,
        grid=(b,),
        in_specs=[pl.BlockSpec((1, seq, ZL_W), lambda i: (i, 0, 0)), _full(cw.shape), _full(cb.shape),
                  _full(wcat.shape), _full(bcat.shape), _full(lam.shape)],
        out_specs=pl.BlockSpec((1, seq, W_LRU), lambda i: (i, 0, 0)),
        out_shape=jax.ShapeDtypeStruct((b, seq, W_LRU), BF16),
        scratch_shapes=[pltpu.VMEM((seq, W_LRU), F32), scan, scan, scan, scan,
                        pltpu.VMEM((2, W_LRU // LANES, LRU_SEGS, LANES), F32)],
        compiler_params=_cparams(("arbitrary",)),
        name="lru_mixer",
    )(zl, cw, cb, wcat, bcat, lam)


def _rms(x, g):
    ms = jnp.mean(x * x, axis=-1, keepdims=True)
    return x * lax.rsqrt(ms + NORM_EPS) * g


def _split_bf16(x):
    hi = x.astype(BF16)
    lo = (x - hi.astype(F32)).astype(BF16)
    return hi, lo


def _group_sums(x, ones):
    return _dot(x.astype(BF16), ones)


def _block_ones(n, block):
    ri = lax.broadcasted_iota(jnp.int32, (n, n), 0) // block
    ci = lax.broadcasted_iota(jnp.int32, (n, n), 1) // block
    return jnp.where(ri == ci, 1.0, 0.0).astype(BF16)


def _rope_part(t, cos, sin, gains):
    return t * cos * gains[1:2] + pltpu.roll(t, QK_ROPE, 1) * sin * gains[2:3]


def _mla_kernel(z_ref, cos_ref, sin_ref, gcq_ref, wq_ref, gckv_ref, wkv_ref, gq_ref, gk_ref,
                o_ref, k_scr, v_scr, q_scr):
    seq = z_ref.shape[1]
    qi = pl.program_id(1)
    c_kv0 = Q_RANK
    c_rope0 = Q_RANK + KV_RANK
    head_w = QK_NOPE + 2 * QK_ROPE
    ones_pair = _block_ones(2 * LANES, LANES)

    @pl.when(qi == 0)
    def _():
        gk = gk_ref[...]
        for c in range(seq // MLA_KV_ROWS):
            rows = slice(c * MLA_KV_ROWS, (c + 1) * MLA_KV_ROWS)
            ckv = _rms(z_ref[0, rows, c_kv0:c_rope0].astype(F32), gckv_ref[...])
            kv = _dot(ckv.astype(BF16), wkv_ref[...])
            t = z_ref[0, rows, c_rope0:c_rope0 + 2 * QK_ROPE].astype(F32)
            t_sq = 0.5 * t * t
            rope = _rope_part(t, cos_ref[rows, :], sin_ref[rows, :], gk)
            for h in range(MLA_HEADS):
                kn = kv[:, h * QK_NOPE:(h + 1) * QK_NOPE]
                ss = jnp.sum(kn * kn + t_sq, axis=-1, keepdims=True)
                inv = lax.rsqrt(ss * (1.0 / QK_HEAD) + NORM_EPS)
                k_scr[h, rows, 0:QK_NOPE] = (kn * inv * gk[0:1]).astype(BF16)
                k_scr[h, rows, QK_NOPE:head_w] = (rope * inv).astype(BF16)
                v0 = MLA_HEADS * QK_NOPE + h * V_HEAD
                v_scr[h, rows, 0:V_HEAD] = kv[:, v0:v0 + V_HEAD].astype(BF16)
                v_scr[h, rows, V_HEAD:2 * V_HEAD] = jnp.ones((MLA_KV_ROWS, V_HEAD), BF16)

    gq = gq_ref[...]
    scale = QK_HEAD ** -0.5 * np.log2(np.e)

    def prep_queries(tile, slot):
        qrows = pl.ds(pl.multiple_of(tile * MLA_Q_ROWS, MLA_Q_ROWS), MLA_Q_ROWS)
        cq = z_ref[0, qrows, 0:Q_RANK].astype(F32)
        ms = _group_sums(cq * cq, jnp.ones((Q_RANK, Q_RANK), BF16)) * (1.0 / Q_RANK)
        cq = cq * lax.rsqrt(ms + NORM_EPS) * gcq_ref[...]
        qall = _dot(cq.astype(BF16), wq_ref[...])
        cos_g = cos_ref[qrows, :] * gq[1:2]
        sin_g = sin_ref[qrows, :] * gq[2:3]
        sq = lambda h: (qall[:, h * head_w:h * head_w + QK_NOPE] ** 2
                        + 0.5 * qall[:, h * head_w + QK_NOPE:(h + 1) * head_w] ** 2)
        for h in range(MLA_HEADS):
            qn = qall[:, h * head_w:h * head_w + QK_NOPE]
            t = qall[:, h * head_w + QK_NOPE:(h + 1) * head_w]
            if h % 2 == 0:
                ss2 = _group_sums(jnp.concatenate([sq(h), sq(h + 1)], axis=-1), ones_pair)
            ss = ss2[:, (h % 2) * LANES:(h % 2 + 1) * LANES]
            inv = lax.rsqrt(ss * (1.0 / QK_HEAD) + NORM_EPS) * scale
            rope = t * cos_g + pltpu.roll(t, QK_ROPE, 1) * sin_g
            q_scr[slot, h] = jnp.concatenate([qn * inv * gq[0:1], rope * inv], axis=-1).astype(BF16)

    n_tiles = seq // MLA_Q_ROWS
    slot = qi % 2

    @pl.when(qi == 0)
    def _():
        prep_queries(0, 0)

    def prep_next():
        prep_queries(jnp.minimum(qi + 1, n_tiles - 1), 1 - slot)

    def write_head(h, ov):
        o_ref[0, :, h * V_HEAD:(h + 1) * V_HEAD] = (ov[:, 0:V_HEAD] / ov[:, V_HEAD:]).astype(BF16)

    g_max = lambda g: jnp.max(jnp.abs(g[0:2]))
    bound = scale * QK_HEAD * g_max(gq) * g_max(gk_ref[...])
    shift_is_safe = bound <= MLA_SAFE_SHIFT

    @pl.when(shift_is_safe)
    def _():
        for h in range(MLA_HEADS):
            ov = jnp.zeros((MLA_Q_ROWS, 2 * V_HEAD), F32)
            for kb in range(seq // MLA_KEY_BLOCK):
                keys = slice(kb * MLA_KEY_BLOCK, (kb + 1) * MLA_KEY_BLOCK)
                s = _dot_nt(q_scr[slot, h], k_scr[h, keys, :])
                ov = ov + _dot(jnp.exp2(s - bound).astype(BF16), v_scr[h, keys, :])
            write_head(h, ov)
        prep_next()

    @pl.when(jnp.logical_not(shift_is_safe))
    def _():
        for h in range(MLA_HEADS):
            s = _dot_nt(q_scr[slot, h], k_scr[h])
            m = jnp.max(s, axis=-1, keepdims=True)
            write_head(h, _dot(jnp.exp2(s - m).astype(BF16), v_scr[h]))
        prep_next()


def _mla_mixer(zm, cos, sin, gcq, wq, gckv, wkv, gq, gk):
    b, seq, _ = zm.shape
    head_w = QK_NOPE + 2 * QK_ROPE
    return pl.pallas_call(
        _mla_kernel,
        grid=(b, seq // MLA_Q_ROWS),
        in_specs=[pl.BlockSpec((1, seq, ZM_W), lambda i, j: (i, 0, 0)), _full(cos.shape), _full(sin.shape),
                  _full(gcq.shape), _full(wq.shape), _full(gckv.shape), _full(wkv.shape),
                  _full(gq.shape), _full(gk.shape)],
        out_specs=pl.BlockSpec((1, MLA_Q_ROWS, W_MLA), lambda i, j: (i, j, 0)),
        out_shape=jax.ShapeDtypeStruct((b, seq, W_MLA), BF16),
        scratch_shapes=[pltpu.VMEM((MLA_HEADS, seq, head_w), BF16),
                        pltpu.VMEM((MLA_HEADS, seq, 2 * V_HEAD), BF16),
                        pltpu.VMEM((2, MLA_HEADS, MLA_Q_ROWS, head_w), BF16)],
        compiler_params=_cparams(("arbitrary", "arbitrary")),
        name="mla_mixer",
    )(zm, cos, sin, gcq, wq, gckv, wkv, gq, gk)


def _log_sigmoid(x):
    return jnp.minimum(x, 0.0) - jnp.log(1.0 + jnp.exp(-jnp.abs(x)))


def _gla_kernel(z_ref, wg_ref, bg_ref, go_ref, o_ref, oi_scr, qd_scr, ki_scr, ke_scr, u_scr, dec_scr, s_scr):
    seq = z_ref.shape[1]
    n_groups = seq // GLA_ROWS
    chunks = GLA_ROWS // GLA_CHUNK
    qk_w = GLA_HEADS * GLA_DK
    k0, v0, g0, og0 = qk_w, 2 * qk_w, 2 * qk_w + W_GLA, 2 * qk_w + W_GLA + LANES

    ri = lax.broadcasted_iota(jnp.int32, (GLA_ROWS, GLA_ROWS), 0)
    ci = lax.broadcasted_iota(jnp.int32, (GLA_ROWS, GLA_ROWS), 1)
    same_chunk = (ri // GLA_CHUNK) == (ci // GLA_CHUNK)
    causal = (jnp.where(same_chunk, ci, GLA_ROWS) <= ri, jnp.where(same_chunk, ci, -1) >= ri)
    cum_mat = jnp.where(causal[0], 1.0, 0.0).astype(BF16)
    avg_mat = jnp.where((ri // GLA_DV) == (ci // GLA_DV), 1.0 / GLA_DV, 0.0).astype(BF16)
    qk_head = lax.broadcasted_iota(jnp.int32, (GLA_ROWS, qk_w), 1) // GLA_DK
    v_head = lax.broadcasted_iota(jnp.int32, (GLA_ROWS, W_GLA), 1) // GLA_DV
    st_mask = (lax.broadcasted_iota(jnp.int32, (W_GLA, qk_w), 0) // GLA_DV
               == lax.broadcasted_iota(jnp.int32, (W_GLA, qk_w), 1) // GLA_DK)
    col_chunk = ci // GLA_CHUNK
    qk_chunk = lax.broadcasted_iota(jnp.int32, (GLA_ROWS, qk_w), 0) // GLA_CHUNK

    def rows_of(g):
        return pl.ds(pl.multiple_of(g * GLA_ROWS, GLA_ROWS), GLA_ROWS)

    def decays(g, _):
        rows = rows_of(g)
        x = _dot(z_ref[0, rows, g0:g0 + LANES], wg_ref[...]) + bg_ref[...]
        la = _log_sigmoid(x) * (1.0 / GLA_TAU)
        la_hi, la_lo = _split_bf16(la)
        b_pre = _dot(cum_mat, la_hi) + _dot(cum_mat, la_lo)
        ends = [b_pre[(c + 1) * GLA_CHUNK - 1:(c + 1) * GLA_CHUNK] for c in range(chunks)]
        bt = jnp.concatenate([jnp.broadcast_to(e, (GLA_CHUNK, 2 * qk_w)) for e in ends], axis=0)
        b_dir = (b_pre[:, 0:qk_w], bt[:, qk_w:] - b_pre[:, qk_w:] + la[:, qk_w:])
        q = z_ref[0, rows, 0:qk_w].astype(F32) * (GLA_DK ** -0.5)
        k = z_ref[0, rows, k0:k0 + qk_w].astype(F32)
        for d in range(2):
            b = b_dir[d]
            cols = slice(d * qk_w, (d + 1) * qk_w)
            qd_scr[rows, cols] = (q * jnp.exp(b)).astype(BF16)
            ki_scr[rows, cols] = (k * jnp.exp(-b)).astype(BF16)
            ke_scr[rows, cols] = (k * jnp.exp(bt[:, cols] - b)).astype(BF16)
        dec_scr[g] = jnp.exp(jnp.concatenate(ends + ends, axis=0))
        return 0

    lax.fori_loop(0, n_groups, decays, 0, unroll=True)

    def local(g, _):
        rows = rows_of(g)
        v = z_ref[0, rows, v0:v0 + W_GLA]
        probs = None
        for d in range(2):
            cols = slice(d * qk_w, (d + 1) * qk_w)
            q_dec = qd_scr[rows, cols]
            q_stack = jnp.concatenate([jnp.where(qk_head == h, q_dec, jnp.zeros_like(q_dec))
                                       for h in range(GLA_HEADS)], axis=0)
            s = _dot_nt(q_stack, ki_scr[rows, cols])
            p = [jnp.where(causal[d], s[h * GLA_ROWS:(h + 1) * GLA_ROWS], 0.0) for h in range(GLA_HEADS)]
            probs = p if probs is None else [a + c for a, c in zip(probs, p)]
        v_stack = jnp.concatenate([jnp.where(v_head == h, v, jnp.zeros_like(v)) for h in range(GLA_HEADS)],
                                  axis=0)
        oi_scr[rows, :] = _dot(jnp.concatenate([p.astype(BF16) for p in probs], axis=1), v_stack)
        v_t = v.astype(F32).T
        vt_stack = jnp.concatenate([jnp.where(col_chunk == c, v_t, 0.0) for c in range(chunks)], axis=0)
        u_scr[g] = _dot(vt_stack.astype(BF16), ke_scr[rows, :])
        return 0

    lax.fori_loop(0, n_groups, local, 0, unroll=4)

    for d in range(2):
        def step(i, st, d=d):
            g = i if d == 0 else n_groups - 1 - i
            dec = dec_scr[g]
            for c in (range(chunks) if d == 0 else reversed(range(chunks))):
                blk = (d * chunks + c) * qk_w
                s_scr[g, :, blk:blk + qk_w] = st.astype(BF16)
                upd = u_scr[g, c * W_GLA:(c + 1) * W_GLA, d * qk_w:(d + 1) * qk_w]
                st = st * dec[c:c + 1, d * qk_w:(d + 1) * qk_w] + jnp.where(st_mask, upd, 0.0)
            return st

        lax.fori_loop(0, n_groups, step, jnp.zeros((W_GLA, qk_w), F32))

    def finish(g, _):
        rows = rows_of(g)
        qd = qd_scr[rows, :]
        lhs = jnp.concatenate([jnp.where(qk_chunk == c, qd[:, d * qk_w:(d + 1) * qk_w],
                                         jnp.zeros((GLA_ROWS, qk_w), BF16))
                               for d in range(2) for c in range(chunks)], axis=1)
        o = oi_scr[rows, :] + _dot_nt(lhs, s_scr[g])
        sq_hi, sq_lo = _split_bf16(o * o)
        ms = _dot(sq_hi, avg_mat) + _dot(sq_lo, avg_mat)
        og = z_ref[0, rows, og0:og0 + W_GLA].astype(F32)
        y = (o * lax.rsqrt(ms + NORM_EPS) * go_ref[...]) * (og * _sigmoid(og))
        o_ref[0, rows, :] = y.astype(BF16)
        return 0

    lax.fori_loop(0, n_groups, finish, 0, unroll=True)


def _gla_mixer(zg, wg, bg, go):
    b, seq, _ = zg.shape
    qk_w = GLA_HEADS * GLA_DK
    n_groups = seq // GLA_ROWS
    chunks = GLA_ROWS // GLA_CHUNK
    return pl.pallas_call(
        _gla_kernel,
        grid=(b,),
        in_specs=[pl.BlockSpec((1, seq, ZG_W), lambda i: (i, 0, 0)), _full(wg.shape), _full(bg.shape),
                  _full(go.shape)],
        out_specs=pl.BlockSpec((1, seq, W_GLA), lambda i: (i, 0, 0)),
        out_shape=jax.ShapeDtypeStruct((b, seq, W_GLA), BF16),
        scratch_shapes=[pltpu.VMEM((seq, W_GLA), F32),
                        pltpu.VMEM((seq, 2 * qk_w), BF16),
                        pltpu.VMEM((seq, 2 * qk_w), BF16),
                        pltpu.VMEM((seq, 2 * qk_w), BF16),
                        pltpu.VMEM((n_groups, chunks * W_GLA, 2 * qk_w), F32),
                        pltpu.VMEM((n_groups, 2 * chunks, 2 * qk_w), F32),
                        pltpu.VMEM((n_groups, W_GLA, 2 * chunks * qk_w), BF16)],
        compiler_params=_cparams(("arbitrary",)),
        name="gla_mixer",
    )(zg, wg, bg, go)


def _out_ffn_kernel(x_ref, ol_ref, om_ref, og_ref, wol_ref, wom_ref, wog_ref, g_ref,
                    wgu_ref, wdown_ref, o_ref, h_ref):
    x1 = (x_ref[...] + _dot(ol_ref[...], wol_ref[...]) + _dot(om_ref[...], wom_ref[...])
          + _dot(og_ref[...], wog_ref[...]))
    o_ref[...] = x1
    h_ref[...] = _rms(x1, g_ref[...]).astype(BF16)

    def hidden_chunk(j, _):
        gu = _dot(h_ref[...], wgu_ref[j])
        gate = gu[:, 0:FFN_COLS]
        act = (gate * jax.nn.sigmoid(gate) * gu[:, FFN_COLS:]).astype(BF16)
        o_ref[...] += _dot(act, wdown_ref[j])
        return 0

    lax.fori_loop(0, wgu_ref.shape[0], hidden_chunk, 0, unroll=True)


def _out_ffn(x2, ol, om, og, wol, wom, wog, g, wgu, wdown):
    n, d = x2.shape
    tm = min(FFN_ROWS, n)
    row = lambda i: (i, 0)
    resident = lambda a: pl.BlockSpec(a.shape, lambda i: (0,) * a.ndim, pipeline_mode=pl.Buffered(1))
    return pl.pallas_call(
        _out_ffn_kernel,
        grid=(n // tm,),
        in_specs=[pl.BlockSpec((tm, d), row), pl.BlockSpec((tm, W_LRU), row),
                  pl.BlockSpec((tm, W_MLA), row), pl.BlockSpec((tm, W_GLA), row),
                  resident(wol), resident(wom), resident(wog), resident(g),
                  resident(wgu), resident(wdown)],
        out_specs=pl.BlockSpec((tm, d), row),
        out_shape=jax.ShapeDtypeStruct((n, d), F32),
        scratch_shapes=[pltpu.VMEM((tm, d), BF16)],
        compiler_params=_cparams(("arbitrary",)),
        name="out_ffn",
    )(x2, ol, om, og, wol, wom, wog, g, wgu, wdown)


def _block_diag(w):
    nb, bi, bo = w.shape
    eye = jnp.eye(nb, dtype=w.dtype)
    return (w[:, :, None, :] * eye[:, None, :, None]).reshape(nb * bi, nb * bo)


def _swap_halves(n):
    return np.concatenate([np.arange(n // 2, n), np.arange(0, n // 2)])


def _prep_layer(p):
    row = lambda v: v.reshape(1, -1).astype(F32)
    d_model = p["w_in"].shape[0]
    qk_w = GLA_HEADS * GLA_DK
    o_cq = 2 * W_LRU
    o_ckv = o_cq + Q_RANK
    o_rope = o_ckv + KV_RANK
    o_glaq = o_rope + QK_ROPE
    o_gf = o_glaq + 2 * qk_w + W_GLA
    o_og = o_gf + 2 * GLA_GATE_RANK
    sw = _swap_halves(QK_ROPE)
    w = p["w_in"]
    w_in = jnp.concatenate([
        w[:, :o_glaq], w[:, o_rope + sw],
        w[:, o_glaq:o_og], jnp.zeros((d_model, LANES - 2 * GLA_GATE_RANK), w.dtype), w[:, o_og:],
    ], axis=1).astype(BF16)

    wcat = (0.5 * jnp.concatenate([_block_diag(p[k]) for k in ("lru_wa_f", "lru_wx_f", "lru_wa_b", "lru_wx_b")],
                                  axis=1)).astype(BF16)
    bcat = 0.5 * jnp.concatenate([p[k] for k in ("lru_ba_f", "lru_bx_f", "lru_ba_b", "lru_bx_b")]).reshape(1, -1)
    lam = jnp.stack([p["lru_lam_f"], p["lru_lam_b"]])

    wq = p["mla_w_uq"].reshape(Q_RANK, MLA_HEADS, QK_HEAD)
    wq = jnp.concatenate([wq, wq[:, :, QK_NOPE + sw]], axis=2).reshape(Q_RANK, -1).astype(BF16)
    wkv = p["mla_w_ukv"].reshape(KV_RANK, MLA_HEADS, QK_NOPE + V_HEAD)
    wkv = jnp.concatenate([wkv[:, :, :QK_NOPE].reshape(KV_RANK, -1),
                           wkv[:, :, QK_NOPE:].reshape(KV_RANK, -1)], axis=1).astype(BF16)

    def qk_gains(g):
        pad = jnp.zeros((LANES - QK_ROPE,), F32)
        return jnp.stack([g[:QK_NOPE], jnp.concatenate([g[QK_NOPE:], pad]),
                          jnp.concatenate([g[QK_NOPE + sw], pad])]).astype(F32)

    wg = jnp.zeros((LANES, 2 * qk_w), F32)
    wg = wg.at[:GLA_GATE_RANK, :qk_w].set(p["gla_wa2_f"])
    wg = wg.at[GLA_GATE_RANK:2 * GLA_GATE_RANK, qk_w:].set(p["gla_wa2_b"])
    bg = jnp.concatenate([p["gla_ba2_f"], p["gla_ba2_b"]]).reshape(1, -1)

    w_out = p["w_out"].astype(BF16)
    d_ff = p["w_ffn_out"].shape[0]
    n_f = d_ff // FFN_COLS
    wgu = p["w_ffn_in"].astype(BF16).reshape(d_model, 2, n_f, FFN_COLS)
    wgu = wgu.transpose(2, 0, 1, 3).reshape(n_f, d_model, 2 * FFN_COLS)
    wdown = p["w_ffn_out"].astype(BF16).reshape(n_f, FFN_COLS, d_model)
    return dict(
        g_mix=row(p["g_mix"]), w_in=w_in,
        conv_w=p["conv_w"], conv_b=row(p["conv_b"]), wcat=wcat, bcat=bcat, lam=lam,
        gcq=row(p["mla_g_cq"]), wq=wq, gckv=row(p["mla_g_ckv"]), wkv=wkv,
        gq=qk_gains(p["mla_g_q"]), gk=qk_gains(p["mla_g_k"]),
        wg=wg.astype(BF16), bg=bg, go=row(jnp.tile(p["gla_g_o"], GLA_HEADS)),
        wol=w_out[:W_LRU], wom=w_out[W_LRU:W_LRU + W_MLA], wog=w_out[W_LRU + W_MLA:],
        g_ffn=row(p["g_ffn"]), wgu=wgu, wdown=wdown,
    )


def _rotary_tables(seq):
    half = QK_ROPE // 2
    inv = 1.0 / (ROPE_THETA ** (jnp.arange(half, dtype=F32) * 2.0 / QK_ROPE))
    ang = jnp.arange(seq, dtype=F32)[:, None] * inv[None, :]
    cos, sin = jnp.cos(ang), jnp.sin(ang)
    pad = jnp.zeros((seq, LANES - QK_ROPE), F32)
    return jnp.concatenate([cos, cos, pad], axis=1), jnp.concatenate([-sin, sin, pad], axis=1)


def _layer(x, cos, sin, lp):
    b, seq, d = x.shape
    x2 = x.reshape(b * seq, d)
    zl, zm, zg = _in_proj(x2, lp["g_mix"], lp["w_in"])
    o_lru = _lru_mixer(zl.reshape(b, seq, ZL_W), lp["conv_w"], lp["conv_b"], lp["wcat"], lp["bcat"], lp["lam"])
    o_mla = _mla_mixer(zm.reshape(b, seq, ZM_W), cos, sin, lp["gcq"], lp["wq"], lp["gckv"], lp["wkv"],
                       lp["gq"], lp["gk"])
    o_gla = _gla_mixer(zg.reshape(b, seq, ZG_W), lp["wg"], lp["bg"], lp["go"])
    y = _out_ffn(x2, o_lru.reshape(b * seq, W_LRU), o_mla.reshape(b * seq, W_MLA),
                 o_gla.reshape(b * seq, W_GLA), lp["wol"], lp["wom"], lp["wog"], lp["g_ffn"],
                 lp["wgu"], lp["wdown"])
    return y.reshape(b, seq, d)


_PARAM_NAMES = ("g_mix", "w_in", "conv_w", "conv_b",
                "lru_wa_f", "lru_ba_f", "lru_wx_f", "lru_bx_f", "lru_lam_f",
                "lru_wa_b", "lru_ba_b", "lru_wx_b", "lru_bx_b", "lru_lam_b",
                "mla_g_cq", "mla_w_uq", "mla_g_ckv", "mla_w_ukv", "mla_g_q", "mla_g_k",
                "gla_wa2_f", "gla_ba2_f", "gla_wa2_b", "gla_ba2_b", "gla_g_o",
                "w_out", "g_ffn", "w_ffn_in", "w_ffn_out")


def kernel(x_prompt, x_sample, g_mix, w_in, conv_w, conv_b, lru_wa_f, lru_ba_f, lru_wx_f, lru_bx_f, lru_lam_f, lru_wa_b, lru_ba_b, lru_wx_b, lru_bx_b, lru_lam_b, mla_g_cq, mla_w_uq, mla_g_ckv, mla_w_ukv, mla_g_q, mla_g_k, gla_wa2_f, gla_ba2_f, gla_wa2_b, gla_ba2_b, gla_g_o, w_out, g_ffn, w_ffn_in, w_ffn_out):
    stacked = dict(zip(_PARAM_NAMES, (
        g_mix, w_in, conv_w, conv_b, lru_wa_f, lru_ba_f, lru_wx_f, lru_bx_f, lru_lam_f,
        lru_wa_b, lru_ba_b, lru_wx_b, lru_bx_b, lru_lam_b,
        mla_g_cq, mla_w_uq, mla_g_ckv, mla_w_ukv, mla_g_q, mla_g_k,
        gla_wa2_f, gla_ba2_f, gla_wa2_b, gla_ba2_b, gla_g_o, w_out, g_ffn, w_ffn_in, w_ffn_out)))
    depth = g_mix.shape[0]
    layers = [_prep_layer({k: v[l] for k, v in stacked.items()}) for l in range(depth)]

    def trunk(x):
        cos, sin = _rotary_tables(x.shape[1])
        for lp in layers:
            x = _layer(x, cos, sin, lp)
        return x

    return trunk(x_prompt), trunk(x_sample)
```

```python
import functools

import jax
import jax.numpy as jnp
import numpy as np
from jax import lax
from jax.experimental import pallas as pl
from jax.experimental.pallas import tpu as pltpu

F32 = jnp.float32
BF16 = jnp.bfloat16

NORM_EPS = 1e-6
W_LRU = 256
W_MLA = 512
W_GLA = 256
LRU_BLOCKS = 8
LRU_C = 8.0
CONV_W = 4
CONV_LEFT = 2
MLA_HEADS = 4
QK_NOPE = 128
QK_ROPE = 64
QK_HEAD = QK_NOPE + QK_ROPE
V_HEAD = 128
Q_RANK = 256
KV_RANK = 128
ROPE_THETA = 10000.0
GLA_HEADS = 4
GLA_DK = 32
GLA_DV = 64
GLA_GATE_RANK = 16
GLA_TAU = 16.0
GLA_CHUNK = 64

LANES = 128
SUBLANES = 8
VMEM_LIMIT_BYTES = 56 * 1024 * 1024

ZL_W = 2 * W_LRU
ZM_W = Q_RANK + KV_RANK + 2 * QK_ROPE
ZG_W = 2 * GLA_HEADS * GLA_DK + W_GLA + LANES + W_GLA

IN_ROWS = 1024
IN_SUB_ROWS = 256
FFN_ROWS = 1024
FFN_COLS = 256
MLA_Q_ROWS = 512
MLA_KV_ROWS = 512
MLA_KEY_BLOCK = 512
MLA_SAFE_SHIFT = 60.0
GLA_ROWS = 256
LRU_SEGS = SUBLANES


def _dot(a, b):
    return jnp.dot(a, b, preferred_element_type=F32)


def _dot_nt(a, b):
    return lax.dot_general(a, b, (((1,), (1,)), ((), ())), preferred_element_type=F32)


def _cparams(sem):
    return pltpu.CompilerParams(dimension_semantics=sem, vmem_limit_bytes=VMEM_LIMIT_BYTES)


def _full(shape):
    return pl.BlockSpec(shape, lambda *_: (0,) * len(shape))


def _in_proj_kernel(x_ref, g_ref, w_ref, zl_ref, zm_ref, zg_ref):
    for r in range(x_ref.shape[0] // IN_SUB_ROWS):
        rows = slice(r * IN_SUB_ROWS, (r + 1) * IN_SUB_ROWS)
        h = _rms(x_ref[rows, :], g_ref[...]).astype(BF16)
        zl_ref[rows, :] = _dot(h, w_ref[:, 0:ZL_W]).astype(BF16)
        zm_ref[rows, :] = _dot(h, w_ref[:, ZL_W:ZL_W + ZM_W]).astype(BF16)
        zg_ref[rows, :] = _dot(h, w_ref[:, ZL_W + ZM_W:]).astype(BF16)


def _in_proj(x2, g, w):
    n, d = x2.shape
    tm = min(IN_ROWS, n)
    row = lambda i: (i, 0)
    return pl.pallas_call(
        _in_proj_kernel,
        grid=(n // tm,),
        in_specs=[pl.BlockSpec((tm, d), row), _full(g.shape), _full(w.shape)],
        out_specs=[pl.BlockSpec((tm, ZL_W), row), pl.BlockSpec((tm, ZM_W), row),
                   pl.BlockSpec((tm, ZG_W), row)],
        out_shape=[jax.ShapeDtypeStruct((n, ZL_W), BF16), jax.ShapeDtypeStruct((n, ZM_W), BF16),
                   jax.ShapeDtypeStruct((n, ZG_W), BF16)],
        compiler_params=_cparams(("arbitrary",)),
        name="in_proj",
    )(x2, g, w)


def _softplus(x):
    return jnp.maximum(x, 0.0) + jnp.log1p(jnp.exp(-jnp.abs(x)))


def _sigmoid(x):
    return 0.5 * jnp.tanh(0.5 * x) + 0.5


def _lru_kernel(z_ref, cw_ref, cb_ref, wcat_ref, bcat_ref, lam_ref, o_ref,
                u_ref, af_ref, vf_ref, ab_ref, vb_ref, c_ref):
    seq = z_ref.shape[1]
    seg = seq // LRU_SEGS
    pitch = seg + SUBLANES
    halves = W_LRU // LANES

    xin = z_ref[0, :, 0:W_LRU].astype(F32)
    row = lax.broadcasted_iota(jnp.int32, (seq, W_LRU), 0)
    cw = cw_ref[...]
    u = xin * cw[CONV_LEFT:CONV_LEFT + 1]
    for j in range(CONV_W):
        off = j - CONV_LEFT
        if off == 0:
            continue
        shifted = pltpu.roll(xin, (-off) % seq, 0)
        valid = (row >= -off) if off < 0 else (row < seq - off)
        u = u + jnp.where(valid, shifted, 0.0) * cw[j:j + 1]
    u_ref[...] = u + cb_ref[...]

    half_c_sp = (-0.5 * LRU_C) * _softplus(-lam_ref[...])

    def gates(s, _):
        r0 = pl.multiple_of(s * seg, seg)
        p0 = pl.multiple_of(s * pitch, SUBLANES)
        uc = u_ref[pl.ds(r0, seg), :]
        g = _dot(uc.astype(BF16), wcat_ref[...]) + bcat_ref[...]
        half_u = 0.5 * uc
        for d, (a_ref, v_ref) in enumerate(((af_ref, vf_ref), (ab_ref, vb_ref))):
            base = d * 2 * W_LRU
            t_r = jnp.tanh(g[:, base:base + W_LRU])
            t_i = jnp.tanh(g[:, base + W_LRU:base + 2 * W_LRU])
            log_a = t_r * half_c_sp[d:d + 1] + half_c_sp[d:d + 1]
            a = jnp.exp(log_a)
            v = jnp.sqrt(-jnp.tanh(log_a) * (a * a + 1.0)) * ((t_i + 1.0) * half_u)
            for hh in range(halves):
                a_ref[hh, pl.ds(p0, seg), :] = a[:, hh * LANES:(hh + 1) * LANES]
                v_ref[hh, pl.ds(p0, seg), :] = v[:, hh * LANES:(hh + 1) * LANES]
        return 0

    lax.fori_loop(0, LRU_SEGS, gates, 0, unroll=4)

    def rows_at(i):
        return pl.ds(i, LRU_SEGS, stride=pitch)

    def scan(i, carry):
        hf, pf, hb, pb = carry
        ib = seg - 1 - i
        hf_n, pf_n, hb_n, pb_n = [], [], [], []
        for hh in range(halves):
            a = af_ref[hh, rows_at(i), :]
            h = a * hf[hh] + vf_ref[hh, rows_at(i), :]
            p = a * pf[hh]
            vf_ref[hh, rows_at(i), :] = h
            af_ref[hh, rows_at(i), :] = p
            hf_n.append(h)
            pf_n.append(p)
            a = ab_ref[hh, rows_at(ib), :]
            h = a * hb[hh] + vb_ref[hh, rows_at(ib), :]
            p = a * pb[hh]
            vb_ref[hh, rows_at(ib), :] = h
            ab_ref[hh, rows_at(ib), :] = p
            hb_n.append(h)
            pb_n.append(p)
        return tuple(hf_n), tuple(pf_n), tuple(hb_n), tuple(pb_n)

    zeros = tuple(jnp.zeros((LRU_SEGS, LANES), F32) for _ in range(halves))
    ones = tuple(jnp.ones((LRU_SEGS, LANES), F32) for _ in range(halves))
    hf_e, pf_e, hb_e, pb_e = lax.fori_loop(0, seg, scan, (zeros, ones, zeros, ones), unroll=4)

    srow = lax.broadcasted_iota(jnp.int32, (LRU_SEGS, LANES), 0)
    for hh in range(halves):
        c = jnp.zeros((LRU_SEGS, LANES), F32)
        for _ in range(LRU_SEGS - 1):
            c = jnp.where(srow == 0, 0.0, pltpu.roll(hf_e[hh] + pf_e[hh] * c, 1, 0))
        c_ref[0, hh] = c
        c = jnp.zeros((LRU_SEGS, LANES), F32)
        for _ in range(LRU_SEGS - 1):
            c = jnp.where(srow == LRU_SEGS - 1, 0.0,
                          pltpu.roll(hb_e[hh] + pb_e[hh] * c, LRU_SEGS - 1, 0))
        c_ref[1, hh] = c

    def finish(s, _):
        r0 = pl.multiple_of(s * seg, seg)
        p0 = pl.multiple_of(s * pitch, SUBLANES)
        for hh in range(halves):
            h = (vf_ref[hh, pl.ds(p0, seg), :] + af_ref[hh, pl.ds(p0, seg), :] * c_ref[0, hh, pl.ds(s, 1), :]
                 + vb_ref[hh, pl.ds(p0, seg), :] + ab_ref[hh, pl.ds(p0, seg), :] * c_ref[1, hh, pl.ds(s, 1), :])
            gate = z_ref[0, pl.ds(r0, seg), W_LRU + hh * LANES:W_LRU + (hh + 1) * LANES].astype(F32)
            o_ref[0, pl.ds(r0, seg), hh * LANES:(hh + 1) * LANES] = (h * jax.nn.gelu(gate)).astype(BF16)
        return 0

    lax.fori_loop(0, LRU_SEGS, finish, 0, unroll=2)


def _lru_mixer(zl, cw, cb, wcat, bcat, lam):
    b, seq, _ = zl.shape
    pitch = seq // LRU_SEGS + SUBLANES
    scan = pltpu.VMEM((W_LRU // LANES, LRU_SEGS * pitch, LANES), F32)
    return pl.pallas_call(
        _lru_kernel,
        grid=(b,),
        in_specs=[pl.BlockSpec((1, seq, ZL_W), lambda i: (i, 0, 0)), _full(cw.shape), _full(cb.shape),
                  _full(wcat.shape), _full(bcat.shape), _full(lam.shape)],
        out_specs=pl.BlockSpec((1, seq, W_LRU), lambda i: (i, 0, 0)),
        out_shape=jax.ShapeDtypeStruct((b, seq, W_LRU), BF16),
        scratch_shapes=[pltpu.VMEM((seq, W_LRU), F32), scan, scan, scan, scan,
                        pltpu.VMEM((2, W_LRU // LANES, LRU_SEGS, LANES), F32)],
        compiler_params=_cparams(("arbitrary",)),
        name="lru_mixer",
    )(zl, cw, cb, wcat, bcat, lam)


def _rms(x, g):
    ms = jnp.mean(x * x, axis=-1, keepdims=True)
    return x * lax.rsqrt(ms + NORM_EPS) * g


def _split_bf16(x):
    hi = x.astype(BF16)
    lo = (x - hi.astype(F32)).astype(BF16)
    return hi, lo


def _group_sums(x, ones):
    return _dot(x.astype(BF16), ones)


def _block_ones(n, block):
    ri = lax.broadcasted_iota(jnp.int32, (n, n), 0) // block
    ci = lax.broadcasted_iota(jnp.int32, (n, n), 1) // block
    return jnp.where(ri == ci, 1.0, 0.0).astype(BF16)


def _rope_part(t, cos, sin, gains):
    return t * cos * gains[1:2] + pltpu.roll(t, QK_ROPE, 1) * sin * gains[2:3]


def _mla_kernel(z_ref, cos_ref, sin_ref, gcq_ref, wq_ref, gckv_ref, wkv_ref, gq_ref, gk_ref,
                o_ref, k_scr, v_scr, q_scr):
    seq = z_ref.shape[1]
    qi = pl.program_id(1)
    c_kv0 = Q_RANK
    c_rope0 = Q_RANK + KV_RANK
    head_w = QK_NOPE + 2 * QK_ROPE
    ones_pair = _block_ones(2 * LANES, LANES)

    @pl.when(jnp.logical_and(pl.program_id(0) == 0, qi == 0))
    def _():
        for h in range(MLA_HEADS):
            v_scr[h, :, V_HEAD:2 * V_HEAD] = jnp.ones((seq, V_HEAD), BF16)

    @pl.when(qi == 0)
    def _():
        gk = gk_ref[...]
        for c in range(seq // MLA_KV_ROWS):
            rows = slice(c * MLA_KV_ROWS, (c + 1) * MLA_KV_ROWS)
            ckv = _rms(z_ref[0, rows, c_kv0:c_rope0].astype(F32), gckv_ref[...])
            kv = _dot(ckv.astype(BF16), wkv_ref[...])
            t = z_ref[0, rows, c_rope0:c_rope0 + 2 * QK_ROPE].astype(F32)
            t_sq = 0.5 * t * t
            rope = _rope_part(t, cos_ref[rows, :], sin_ref[rows, :], gk)
            for h in range(MLA_HEADS):
                kn = kv[:, h * QK_NOPE:(h + 1) * QK_NOPE]
                ss = jnp.sum(kn * kn + t_sq, axis=-1, keepdims=True)
                inv = lax.rsqrt(ss * (1.0 / QK_HEAD) + NORM_EPS)
                k_scr[h, rows, 0:QK_NOPE] = (kn * inv * gk[0:1]).astype(BF16)
                k_scr[h, rows, QK_NOPE:head_w] = (rope * inv).astype(BF16)
                v0 = MLA_HEADS * QK_NOPE + h * V_HEAD
                v_scr[h, rows, 0:V_HEAD] = kv[:, v0:v0 + V_HEAD].astype(BF16)

    gq = gq_ref[...]
    scale = QK_HEAD ** -0.5 * np.log2(np.e)

    def prep_queries(tile, slot):
        qrows = pl.ds(pl.multiple_of(tile * MLA_Q_ROWS, MLA_Q_ROWS), MLA_Q_ROWS)
        cq = z_ref[0, qrows, 0:Q_RANK].astype(F32)
        ms = _group_sums(cq * cq, jnp.ones((Q_RANK, Q_RANK), BF16)) * (1.0 / Q_RANK)
        cq = cq * lax.rsqrt(ms + NORM_EPS) * gcq_ref[...]
        qall = _dot(cq.astype(BF16), wq_ref[...])
        cos_g = cos_ref[qrows, :] * gq[1:2]
        sin_g = sin_ref[qrows, :] * gq[2:3]
        sq = lambda h: (qall[:, h * head_w:h * head_w + QK_NOPE] ** 2
                        + 0.5 * qall[:, h * head_w + QK_NOPE:(h + 1) * head_w] ** 2)
        for h in range(MLA_HEADS):
            qn = qall[:, h * head_w:h * head_w + QK_NOPE]
            t = qall[:, h * head_w + QK_NOPE:(h + 1) * head_w]
            if h % 2 == 0:
                ss2 = _group_sums(jnp.concatenate([sq(h), sq(h + 1)], axis=-1), ones_pair)
            ss = ss2[:, (h % 2) * LANES:(h % 2 + 1) * LANES]
            inv = lax.rsqrt(ss * (1.0 / QK_HEAD) + NORM_EPS) * scale
            rope = t * cos_g + pltpu.roll(t, QK_ROPE, 1) * sin_g
            q_scr[slot, h] = jnp.concatenate([qn * inv * gq[0:1], rope * inv], axis=-1).astype(BF16)

    n_tiles = seq // MLA_Q_ROWS
    slot = qi % 2

    @pl.when(qi == 0)
    def _():
        prep_queries(0, 0)

    def prep_next():
        prep_queries(jnp.minimum(qi + 1, n_tiles - 1), 1 - slot)

    def write_head(h, ov):
        o_ref[0, :, h * V_HEAD:(h + 1) * V_HEAD] = (ov[:, 0:V_HEAD] / ov[:, V_HEAD:]).astype(BF16)

    g_max = lambda g: jnp.max(jnp.abs(g[0:2]))
    bound = scale * QK_HEAD * g_max(gq) * g_max(gk_ref[...])
    shift_is_safe = bound <= MLA_SAFE_SHIFT

    @pl.when(shift_is_safe)
    def _():
        for h in range(MLA_HEADS):
            ov = jnp.zeros((MLA_Q_ROWS, 2 * V_HEAD), F32)
            for kb in range(seq // MLA_KEY_BLOCK):
                keys = slice(kb * MLA_KEY_BLOCK, (kb + 1) * MLA_KEY_BLOCK)
                s = _dot_nt(q_scr[slot, h], k_scr[h, keys, :])
                ov = ov + _dot(jnp.exp2(s - bound).astype(BF16), v_scr[h, keys, :])
            write_head(h, ov)
        prep_next()

    @pl.when(jnp.logical_not(shift_is_safe))
    def _():
        for h in range(MLA_HEADS):
            s = _dot_nt(q_scr[slot, h], k_scr[h])
            m = jnp.max(s, axis=-1, keepdims=True)
            write_head(h, _dot(jnp.exp2(s - m).astype(BF16), v_scr[h]))
        prep_next()


def _mla_mixer(zm, cos, sin, gcq, wq, gckv, wkv, gq, gk):
    b, seq, _ = zm.shape
    head_w = QK_NOPE + 2 * QK_ROPE
    return pl.pallas_call(
        _mla_kernel,
        grid=(b, seq // MLA_Q_ROWS),
        in_specs=[pl.BlockSpec((1, seq, ZM_W), lambda i, j: (i, 0, 0)), _full(cos.shape), _full(sin.shape),
                  _full(gcq.shape), _full(wq.shape), _full(gckv.shape), _full(wkv.shape),
                  _full(gq.shape), _full(gk.shape)],
        out_specs=pl.BlockSpec((1, MLA_Q_ROWS, W_MLA), lambda i, j: (i, j, 0)),
        out_shape=jax.ShapeDtypeStruct((b, seq, W_MLA), BF16),
        scratch_shapes=[pltpu.VMEM((MLA_HEADS, seq, head_w), BF16),
                        pltpu.VMEM((MLA_HEADS, seq, 2 * V_HEAD), BF16),
                        pltpu.VMEM((2, MLA_HEADS, MLA_Q_ROWS, head_w), BF16)],
        compiler_params=_cparams(("arbitrary", "arbitrary")),
        name="mla_mixer",
    )(zm, cos, sin, gcq, wq, gckv, wkv, gq, gk)


def _log_sigmoid(x):
    return jnp.minimum(x, 0.0) - jnp.log(1.0 + jnp.exp(-jnp.abs(x)))


def _gla_kernel(z_ref, wg_ref, bg_ref, go_ref, o_ref, oi_scr, qd_scr, ki_scr, ke_scr, u_scr, dec_scr, s_scr):
    seq = z_ref.shape[1]
    n_groups = seq // GLA_ROWS
    chunks = GLA_ROWS // GLA_CHUNK
    qk_w = GLA_HEADS * GLA_DK
    k0, v0, g0, og0 = qk_w, 2 * qk_w, 2 * qk_w + W_GLA, 2 * qk_w + W_GLA + LANES

    ri = lax.broadcasted_iota(jnp.int32, (GLA_ROWS, GLA_ROWS), 0)
    ci = lax.broadcasted_iota(jnp.int32, (GLA_ROWS, GLA_ROWS), 1)
    same_chunk = (ri // GLA_CHUNK) == (ci // GLA_CHUNK)
    causal = (jnp.where(same_chunk, ci, GLA_ROWS) <= ri, jnp.where(same_chunk, ci, -1) >= ri)
    cum_mat = jnp.where(causal[0], 1.0, 0.0).astype(BF16)
    avg_mat = jnp.where((ri // GLA_DV) == (ci // GLA_DV), 1.0 / GLA_DV, 0.0).astype(BF16)
    qk_head = lax.broadcasted_iota(jnp.int32, (GLA_ROWS, qk_w), 1) // GLA_DK
    v_head = lax.broadcasted_iota(jnp.int32, (GLA_ROWS, W_GLA), 1) // GLA_DV
    st_mask = (lax.broadcasted_iota(jnp.int32, (W_GLA, qk_w), 0) // GLA_DV
               == lax.broadcasted_iota(jnp.int32, (W_GLA, qk_w), 1) // GLA_DK)
    col_chunk = ci // GLA_CHUNK

    def rows_of(g):
        return pl.ds(pl.multiple_of(g * GLA_ROWS, GLA_ROWS), GLA_ROWS)

    def decays(g, _):
        rows = rows_of(g)
        x = _dot(z_ref[0, rows, g0:g0 + LANES], wg_ref[...]) + bg_ref[...]
        la = _log_sigmoid(x) * (1.0 / GLA_TAU)
        la_hi, la_lo = _split_bf16(la)
        b_pre = _dot(cum_mat, la_hi) + _dot(cum_mat, la_lo)
        ends = [b_pre[(c + 1) * GLA_CHUNK - 1:(c + 1) * GLA_CHUNK] for c in range(chunks)]
        bt = jnp.concatenate([jnp.broadcast_to(e, (GLA_CHUNK, 2 * qk_w)) for e in ends], axis=0)
        b_dir = (b_pre[:, 0:qk_w], bt[:, qk_w:] - b_pre[:, qk_w:] + la[:, qk_w:])
        q = z_ref[0, rows, 0:qk_w].astype(F32) * (GLA_DK ** -0.5)
        k = z_ref[0, rows, k0:k0 + qk_w].astype(F32)
        for d in range(2):
            b = b_dir[d]
            cols = slice(d * qk_w, (d + 1) * qk_w)
            qd_scr[rows, cols] = (q * jnp.exp(b)).astype(BF16)
            ki_scr[rows, cols] = (k * jnp.exp(-b)).astype(BF16)
            ke_scr[rows, cols] = (k * jnp.exp(bt[:, cols] - b)).astype(BF16)
        dec_scr[g] = jnp.exp(jnp.concatenate(ends + ends, axis=0))
        return 0

    lax.fori_loop(0, n_groups, decays, 0, unroll=True)

    def local(g, _):
        rows = rows_of(g)
        v = z_ref[0, rows, v0:v0 + W_GLA]
        probs = None
        for d in range(2):
            cols = slice(d * qk_w, (d + 1) * qk_w)
            q_dec = qd_scr[rows, cols]
            q_stack = jnp.concatenate([jnp.where(qk_head == h, q_dec, jnp.zeros_like(q_dec))
                                       for h in range(GLA_HEADS)], axis=0)
            s = _dot_nt(q_stack, ki_scr[rows, cols])
            p = [jnp.where(causal[d], s[h * GLA_ROWS:(h + 1) * GLA_ROWS], 0.0) for h in range(GLA_HEADS)]
            probs = p if probs is None else [a + c for a, c in zip(probs, p)]
        v_stack = jnp.concatenate([jnp.where(v_head == h, v, jnp.zeros_like(v)) for h in range(GLA_HEADS)],
                                  axis=0)
        oi_scr[rows, :] = _dot(jnp.concatenate([p.astype(BF16) for p in probs], axis=1), v_stack)
        v_t = v.astype(F32).T
        vt_stack = jnp.concatenate([jnp.where(col_chunk == c, v_t, 0.0) for c in range(chunks)], axis=0)
        u_scr[g] = _dot(vt_stack.astype(BF16), ke_scr[rows, :])
        return 0

    lax.fori_loop(0, n_groups, local, 0, unroll=4)

    for d in range(2):
        def step(i, st, d=d):
            g = i if d == 0 else n_groups - 1 - i
            dec = dec_scr[g]
            for c in (range(chunks) if d == 0 else reversed(range(chunks))):
                blk = (c * 2 + d) * qk_w
                s_scr[g, :, blk:blk + qk_w] = st.astype(BF16)
                upd = u_scr[g, c * W_GLA:(c + 1) * W_GLA, d * qk_w:(d + 1) * qk_w]
                st = st * dec[c:c + 1, d * qk_w:(d + 1) * qk_w] + jnp.where(st_mask, upd, 0.0)
            return st

        lax.fori_loop(0, n_groups, step, jnp.zeros((W_GLA, qk_w), F32))

    def finish(g, _):
        rows = rows_of(g)
        inter = []
        for c in range(chunks):
            crow = pl.ds(pl.multiple_of(g * GLA_ROWS + c * GLA_CHUNK, GLA_CHUNK), GLA_CHUNK)
            inter.append(_dot_nt(qd_scr[crow, :], s_scr[g, :, c * 2 * qk_w:(c + 1) * 2 * qk_w]))
        o = oi_scr[rows, :] + jnp.concatenate(inter, axis=0)
        ms = _dot((o * o).astype(BF16), avg_mat)
        og = z_ref[0, rows, og0:og0 + W_GLA].astype(F32)
        y = (o * lax.rsqrt(ms + NORM_EPS) * go_ref[...]) * (og * _sigmoid(og))
        o_ref[0, rows, :] = y.astype(BF16)
        return 0

    lax.fori_loop(0, n_groups, finish, 0, unroll=True)


def _gla_mixer(zg, wg, bg, go):
    b, seq, _ = zg.shape
    qk_w = GLA_HEADS * GLA_DK
    n_groups = seq // GLA_ROWS
    chunks = GLA_ROWS // GLA_CHUNK
    return pl.pallas_call(
        _gla_kernel,
        grid=(b,),
        in_specs=[pl.BlockSpec((1, seq, ZG_W), lambda i: (i, 0, 0)), _full(wg.shape), _full(bg.shape),
                  _full(go.shape)],
        out_specs=pl.BlockSpec((1, seq, W_GLA), lambda i: (i, 0, 0)),
        out_shape=jax.ShapeDtypeStruct((b, seq, W_GLA), BF16),
        scratch_shapes=[pltpu.VMEM((seq, W_GLA), F32),
                        pltpu.VMEM((seq, 2 * qk_w), BF16),
                        pltpu.VMEM((seq, 2 * qk_w), BF16),
                        pltpu.VMEM((seq, 2 * qk_w), BF16),
                        pltpu.VMEM((n_groups, chunks * W_GLA, 2 * qk_w), F32),
                        pltpu.VMEM((n_groups, 2 * chunks, 2 * qk_w), F32),
                        pltpu.VMEM((n_groups, W_GLA, 2 * chunks * qk_w), BF16)],
        compiler_params=_cparams(("arbitrary",)),
        name="gla_mixer",
    )(zg, wg, bg, go)


def _out_ffn_kernel(x_ref, ol_ref, om_ref, og_ref, wol_ref, wom_ref, wog_ref, g_ref,
                    wgu_ref, wdown_ref, o_ref, h_ref):
    x1 = (x_ref[...] + _dot(ol_ref[...], wol_ref[...]) + _dot(om_ref[...], wom_ref[...])
          + _dot(og_ref[...], wog_ref[...]))
    o_ref[...] = x1
    h_ref[...] = _rms(x1, g_ref[...]).astype(BF16)

    d_ff = wdown_ref.shape[0]
    for j in range(d_ff // FFN_COLS):
        cols = slice(j * FFN_COLS, (j + 1) * FFN_COLS)
        up_cols = slice(d_ff + j * FFN_COLS, d_ff + (j + 1) * FFN_COLS)
        gate = _dot(h_ref[...], wgu_ref[:, cols])
        up = _dot(h_ref[...], wgu_ref[:, up_cols])
        act = (gate * jax.nn.sigmoid(gate) * up).astype(BF16)
        o_ref[...] += _dot(act, wdown_ref[cols, :])


def _out_ffn(x2, ol, om, og, wol, wom, wog, g, wgu, wdown):
    n, d = x2.shape
    tm = min(FFN_ROWS, n)
    row = lambda i: (i, 0)
    resident = lambda a: pl.BlockSpec(a.shape, lambda i: (0,) * a.ndim, pipeline_mode=pl.Buffered(1))
    return pl.pallas_call(
        _out_ffn_kernel,
        grid=(n // tm,),
        in_specs=[pl.BlockSpec((tm, d), row), pl.BlockSpec((tm, W_LRU), row),
                  pl.BlockSpec((tm, W_MLA), row), pl.BlockSpec((tm, W_GLA), row),
                  resident(wol), resident(wom), resident(wog), resident(g),
                  resident(wgu), resident(wdown)],
        out_specs=pl.BlockSpec((tm, d), row),
        out_shape=jax.ShapeDtypeStruct((n, d), F32),
        scratch_shapes=[pltpu.VMEM((tm, d), BF16)],
        compiler_params=_cparams(("arbitrary",)),
        name="out_ffn",
    )(x2, ol, om, og, wol, wom, wog, g, wgu, wdown)


def _block_diag(w):
    nb, bi, bo = w.shape
    eye = jnp.eye(nb, dtype=w.dtype)
    return (w[:, :, None, :] * eye[:, None, :, None]).reshape(nb * bi, nb * bo)


def _swap_halves(n):
    return np.concatenate([np.arange(n // 2, n), np.arange(0, n // 2)])


def _prep_layer(p):
    row = lambda v: v.reshape(1, -1).astype(F32)
    d_model = p["w_in"].shape[0]
    qk_w = GLA_HEADS * GLA_DK
    o_cq = 2 * W_LRU
    o_ckv = o_cq + Q_RANK
    o_rope = o_ckv + KV_RANK
    o_glaq = o_rope + QK_ROPE
    o_gf = o_glaq + 2 * qk_w + W_GLA
    o_og = o_gf + 2 * GLA_GATE_RANK
    sw = _swap_halves(QK_ROPE)
    w = p["w_in"]
    w_in = jnp.concatenate([
        w[:, :o_glaq], w[:, o_rope + sw],
        w[:, o_glaq:o_og], jnp.zeros((d_model, LANES - 2 * GLA_GATE_RANK), w.dtype), w[:, o_og:],
    ], axis=1).astype(BF16)

    wcat = (0.5 * jnp.concatenate([_block_diag(p[k]) for k in ("lru_wa_f", "lru_wx_f", "lru_wa_b", "lru_wx_b")],
                                  axis=1)).astype(BF16)
    bcat = 0.5 * jnp.concatenate([p[k] for k in ("lru_ba_f", "lru_bx_f", "lru_ba_b", "lru_bx_b")]).reshape(1, -1)
    lam = jnp.stack([p["lru_lam_f"], p["lru_lam_b"]])

    wq = p["mla_w_uq"].reshape(Q_RANK, MLA_HEADS, QK_HEAD)
    wq = jnp.concatenate([wq, wq[:, :, QK_NOPE + sw]], axis=2).reshape(Q_RANK, -1).astype(BF16)
    wkv = p["mla_w_ukv"].reshape(KV_RANK, MLA_HEADS, QK_NOPE + V_HEAD)
    wkv = jnp.concatenate([wkv[:, :, :QK_NOPE].reshape(KV_RANK, -1),
                           wkv[:, :, QK_NOPE:].reshape(KV_RANK, -1)], axis=1).astype(BF16)

    def qk_gains(g):
        pad = jnp.zeros((LANES - QK_ROPE,), F32)
        return jnp.stack([g[:QK_NOPE], jnp.concatenate([g[QK_NOPE:], pad]),
                          jnp.concatenate([g[QK_NOPE + sw], pad])]).astype(F32)

    wg = jnp.zeros((LANES, 2 * qk_w), F32)
    wg = wg.at[:GLA_GATE_RANK, :qk_w].set(p["gla_wa2_f"])
    wg = wg.at[GLA_GATE_RANK:2 * GLA_GATE_RANK, qk_w:].set(p["gla_wa2_b"])
    bg = jnp.concatenate([p["gla_ba2_f"], p["gla_ba2_b"]]).reshape(1, -1)

    w_out = p["w_out"].astype(BF16)
    wgu = p["w_ffn_in"].astype(BF16)
    wdown = p["w_ffn_out"].astype(BF16)
    return dict(
        g_mix=row(p["g_mix"]), w_in=w_in,
        conv_w=p["conv_w"], conv_b=row(p["conv_b"]), wcat=wcat, bcat=bcat, lam=lam,
        gcq=row(p["mla_g_cq"]), wq=wq, gckv=row(p["mla_g_ckv"]), wkv=wkv,
        gq=qk_gains(p["mla_g_q"]), gk=qk_gains(p["mla_g_k"]),
        wg=wg.astype(BF16), bg=bg, go=row(jnp.tile(p["gla_g_o"], GLA_HEADS)),
        wol=w_out[:W_LRU], wom=w_out[W_LRU:W_LRU + W_MLA], wog=w_out[W_LRU + W_MLA:],
        g_ffn=row(p["g_ffn"]), wgu=wgu, wdown=wdown,
    )


def _rotary_tables(seq):
    half = QK_ROPE // 2
    inv = 1.0 / (ROPE_THETA ** (jnp.arange(half, dtype=F32) * 2.0 / QK_ROPE))
    ang = jnp.arange(seq, dtype=F32)[:, None] * inv[None, :]
    cos, sin = jnp.cos(ang), jnp.sin(ang)
    pad = jnp.zeros((seq, LANES - QK_ROPE), F32)
    return jnp.concatenate([cos, cos, pad], axis=1), jnp.concatenate([-sin, sin, pad], axis=1)


def _layer(x, cos, sin, lp):
    b, seq, d = x.shape
    x2 = x.reshape(b * seq, d)
    zl, zm, zg = _in_proj(x2, lp["g_mix"], lp["w_in"])
    o_lru = _lru_mixer(zl.reshape(b, seq, ZL_W), lp["conv_w"], lp["conv_b"], lp["wcat"], lp["bcat"], lp["lam"])
    o_mla = _mla_mixer(zm.reshape(b, seq, ZM_W), cos, sin, lp["gcq"], lp["wq"], lp["gckv"], lp["wkv"],
                       lp["gq"], lp["gk"])
    o_gla = _gla_mixer(zg.reshape(b, seq, ZG_W), lp["wg"], lp["bg"], lp["go"])
    y = _out_ffn(x2, o_lru.reshape(b * seq, W_LRU), o_mla.reshape(b * seq, W_MLA),
                 o_gla.reshape(b * seq, W_GLA), lp["wol"], lp["wom"], lp["wog"], lp["g_ffn"],
                 lp["wgu"], lp["wdown"])
    return y.reshape(b, seq, d)


_PARAM_NAMES = ("g_mix", "w_in", "conv_w", "conv_b",
                "lru_wa_f", "lru_ba_f", "lru_wx_f", "lru_bx_f", "lru_lam_f",
                "lru_wa_b", "lru_ba_b", "lru_wx_b", "lru_bx_b", "lru_lam_b",
                "mla_g_cq", "mla_w_uq", "mla_g_ckv", "mla_w_ukv", "mla_g_q", "mla_g_k",
                "gla_wa2_f", "gla_ba2_f", "gla_wa2_b", "gla_ba2_b", "gla_g_o",
                "w_out", "g_ffn", "w_ffn_in", "w_ffn_out")


def kernel(x_prompt, x_sample, g_mix, w_in, conv_w, conv_b, lru_wa_f, lru_ba_f, lru_wx_f, lru_bx_f, lru_lam_f, lru_wa_b, lru_ba_b, lru_wx_b, lru_bx_b, lru_lam_b, mla_g_cq, mla_w_uq, mla_g_ckv, mla_w_ukv, mla_g_q, mla_g_k, gla_wa2_f, gla_ba2_f, gla_wa2_b, gla_ba2_b, gla_g_o, w_out, g_ffn, w_ffn_in, w_ffn_out):
    stacked = dict(zip(_PARAM_NAMES, (
        g_mix, w_in, conv_w, conv_b, lru_wa_f, lru_ba_f, lru_wx_f, lru_bx_f, lru_lam_f,
        lru_wa_b, lru_ba_b, lru_wx_b, lru_bx_b, lru_lam_b,
        mla_g_cq, mla_w_uq, mla_g_ckv, mla_w_ukv, mla_g_q, mla_g_k,
        gla_wa2_f, gla_ba2_f, gla_wa2_b, gla_ba2_b, gla_g_o, w_out, g_ffn, w_ffn_in, w_ffn_out)))
    depth = g_mix.shape[0]
    layers = [_prep_layer({k: v[l] for k, v in stacked.items()}) for l in range(depth)]

    def trunk(x):
        cos, sin = _rotary_tables(x.shape[1])
        for lp in layers:
            x = _layer(x, cos, sin, lp)
        return x

    return trunk(x_prompt), trunk(x_sample)
```

```python
import functools

import jax
import jax.numpy as jnp
import numpy as np
from jax import lax
from jax.experimental import pallas as pl
from jax.experimental.pallas import tpu as pltpu

F32 = jnp.float32
BF16 = jnp.bfloat16

NORM_EPS = 1e-6
W_LRU = 256
W_MLA = 512
W_GLA = 256
LRU_BLOCKS = 8
LRU_C = 8.0
CONV_W = 4
CONV_LEFT = 2
MLA_HEADS = 4
QK_NOPE = 128
QK_ROPE = 64
QK_HEAD = QK_NOPE + QK_ROPE
V_HEAD = 128
Q_RANK = 256
KV_RANK = 128
ROPE_THETA = 10000.0
GLA_HEADS = 4
GLA_DK = 32
GLA_DV = 64
GLA_GATE_RANK = 16
GLA_TAU = 16.0
GLA_CHUNK = 64

LANES = 128
SUBLANES = 8
VMEM_LIMIT_BYTES = 56 * 1024 * 1024

ZL_W = 2 * W_LRU
ZM_W = Q_RANK + KV_RANK + LANES
ZG_W = 2 * GLA_HEADS * GLA_DK + W_GLA + W_GLA
GATE_LANE0 = QK_ROPE

IN_ROWS = 2048
IN_SUB_ROWS = 256
FFN_ROWS = 1024
FFN_COLS = 256
FFN_SUB_ROWS = 256
MLA_Q_ROWS = 512
MLA_KV_ROWS = 512
MLA_KEY_BLOCK = 512
MLA_SAFE_SHIFT = 60.0
GLA_ROWS = 256
LRU_SEGS = SUBLANES


def _dot(a, b):
    return jnp.dot(a, b, preferred_element_type=F32)


def _dot_nt(a, b):
    return lax.dot_general(a, b, (((1,), (1,)), ((), ())), preferred_element_type=F32)


def _cparams(sem):
    return pltpu.CompilerParams(dimension_semantics=sem, vmem_limit_bytes=VMEM_LIMIT_BYTES)


def _full(shape):
    return pl.BlockSpec(shape, lambda *_: (0,) * len(shape))


def _in_proj_kernel(x_ref, g_ref, w_ref, zl_ref, zm_ref, zg_ref):
    for r in range(x_ref.shape[0] // IN_SUB_ROWS):
        rows = slice(r * IN_SUB_ROWS, (r + 1) * IN_SUB_ROWS)
        h = _rms(x_ref[rows, :], g_ref[...]).astype(BF16)
        zl_ref[rows, :] = _dot(h, w_ref[:, 0:ZL_W]).astype(BF16)
        zm_ref[rows, :] = _dot(h, w_ref[:, ZL_W:ZL_W + ZM_W]).astype(BF16)
        zg_ref[rows, :] = _dot(h, w_ref[:, ZL_W + ZM_W:]).astype(BF16)


def _in_proj(x2, g, w):
    n, d = x2.shape
    tm = min(IN_ROWS, n)
    row = lambda i: (i, 0)
    return pl.pallas_call(
        _in_proj_kernel,
        grid=(n // tm,),
        in_specs=[pl.BlockSpec((tm, d), row), _full(g.shape), _full(w.shape)],
        out_specs=[pl.BlockSpec((tm, ZL_W), row), pl.BlockSpec((tm, ZM_W), row),
                   pl.BlockSpec((tm, ZG_W), row)],
        out_shape=[jax.ShapeDtypeStruct((n, ZL_W), BF16), jax.ShapeDtypeStruct((n, ZM_W), BF16),
                   jax.ShapeDtypeStruct((n, ZG_W), BF16)],
        compiler_params=_cparams(("arbitrary",)),
        name="in_proj",
    )(x2, g, w)


def _softplus(x):
    return jnp.maximum(x, 0.0) + jnp.log1p(jnp.exp(-jnp.abs(x)))


def _sigmoid(x):
    return 0.5 * jnp.tanh(0.5 * x) + 0.5


def _lru_kernel(z_ref, cw_ref, cb_ref, wcat_ref, bcat_ref, lam_ref, o_ref,
                u_ref, af_ref, vf_ref, ab_ref, vb_ref, c_ref):
    seq = z_ref.shape[1]
    seg = seq // LRU_SEGS
    pitch = seg + SUBLANES
    halves = W_LRU // LANES

    xin = z_ref[0, :, 0:W_LRU].astype(F32)
    row = lax.broadcasted_iota(jnp.int32, (seq, W_LRU), 0)
    cw = cw_ref[...]
    u = xin * cw[CONV_LEFT:CONV_LEFT + 1]
    for j in range(CONV_W):
        off = j - CONV_LEFT
        if off == 0:
            continue
        shifted = pltpu.roll(xin, (-off) % seq, 0)
        valid = (row >= -off) if off < 0 else (row < seq - off)
        u = u + jnp.where(valid, shifted, 0.0) * cw[j:j + 1]
    u_ref[...] = u + cb_ref[...]

    half_c_sp = (-0.5 * LRU_C) * _softplus(-lam_ref[...])

    def gates(s, _):
        r0 = pl.multiple_of(s * seg, seg)
        p0 = pl.multiple_of(s * pitch, SUBLANES)
        uc = u_ref[pl.ds(r0, seg), :]
        g = _dot(uc.astype(BF16), wcat_ref[...]) + bcat_ref[...]
        half_u = 0.5 * uc
        for d, (a_ref, v_ref) in enumerate(((af_ref, vf_ref), (ab_ref, vb_ref))):
            base = d * 2 * W_LRU
            t_r = jnp.tanh(g[:, base:base + W_LRU])
            t_i = jnp.tanh(g[:, base + W_LRU:base + 2 * W_LRU])
            log_a = t_r * half_c_sp[d:d + 1] + half_c_sp[d:d + 1]
            a = jnp.exp(log_a)
            v = jnp.sqrt(-jnp.tanh(log_a) * (a * a + 1.0)) * ((t_i + 1.0) * half_u)
            for hh in range(halves):
                a_ref[hh, pl.ds(p0, seg), :] = a[:, hh * LANES:(hh + 1) * LANES]
                v_ref[hh, pl.ds(p0, seg), :] = v[:, hh * LANES:(hh + 1) * LANES]
        return 0

    lax.fori_loop(0, LRU_SEGS, gates, 0, unroll=4)

    def rows_at(i):
        return pl.ds(i, LRU_SEGS, stride=pitch)

    def scan(i, carry):
        hf, pf, hb, pb = carry
        ib = seg - 1 - i
        hf_n, pf_n, hb_n, pb_n = [], [], [], []
        for hh in range(halves):
            a = af_ref[hh, rows_at(i), :]
            h = a * hf[hh] + vf_ref[hh, rows_at(i), :]
            p = a * pf[hh]
            vf_ref[hh, rows_at(i), :] = h
            af_ref[hh, rows_at(i), :] = p
            hf_n.append(h)
            pf_n.append(p)
            a = ab_ref[hh, rows_at(ib), :]
            h = a * hb[hh] + vb_ref[hh, rows_at(ib), :]
            p = a * pb[hh]
            vb_ref[hh, rows_at(ib), :] = h
            ab_ref[hh, rows_at(ib), :] = p
            hb_n.append(h)
            pb_n.append(p)
        return tuple(hf_n), tuple(pf_n), tuple(hb_n), tuple(pb_n)

    zeros = tuple(jnp.zeros((LRU_SEGS, LANES), F32) for _ in range(halves))
    ones = tuple(jnp.ones((LRU_SEGS, LANES), F32) for _ in range(halves))
    hf_e, pf_e, hb_e, pb_e = lax.fori_loop(0, seg, scan, (zeros, ones, zeros, ones), unroll=4)

    srow = lax.broadcasted_iota(jnp.int32, (LRU_SEGS, LANES), 0)
    for hh in range(halves):
        c = jnp.zeros((LRU_SEGS, LANES), F32)
        for _ in range(LRU_SEGS - 1):
            c = jnp.where(srow == 0, 0.0, pltpu.roll(hf_e[hh] + pf_e[hh] * c, 1, 0))
        c_ref[0, hh] = c
        c = jnp.zeros((LRU_SEGS, LANES), F32)
        for _ in range(LRU_SEGS - 1):
            c = jnp.where(srow == LRU_SEGS - 1, 0.0,
                          pltpu.roll(hb_e[hh] + pb_e[hh] * c, LRU_SEGS - 1, 0))
        c_ref[1, hh] = c

    def finish(s, _):
        r0 = pl.multiple_of(s * seg, seg)
        p0 = pl.multiple_of(s * pitch, SUBLANES)
        for hh in range(halves):
            h = (vf_ref[hh, pl.ds(p0, seg), :] + af_ref[hh, pl.ds(p0, seg), :] * c_ref[0, hh, pl.ds(s, 1), :]
                 + vb_ref[hh, pl.ds(p0, seg), :] + ab_ref[hh, pl.ds(p0, seg), :] * c_ref[1, hh, pl.ds(s, 1), :])
            gate = z_ref[0, pl.ds(r0, seg), W_LRU + hh * LANES:W_LRU + (hh + 1) * LANES].astype(F32)
            o_ref[0, pl.ds(r0, seg), hh * LANES:(hh + 1) * LANES] = (h * jax.nn.gelu(gate)).astype(BF16)
        return 0

    lax.fori_loop(0, LRU_SEGS, finish, 0, unroll=2)


def _lru_mixer(zl, cw, cb, wcat, bcat, lam):
    b, seq, _ = zl.shape
    pitch = seq // LRU_SEGS + SUBLANES
    scan = pltpu.VMEM((W_LRU // LANES, LRU_SEGS * pitch, LANES), F32)
    return pl.pallas_call(
        _lru_kernel,
        grid=(b,),
        in_specs=[pl.BlockSpec((1, seq, ZL_W), lambda i: (i, 0, 0)), _full(cw.shape), _full(cb.shape),
                  _full(wcat.shape), _full(bcat.shape), _full(lam.shape)],
        out_specs=pl.BlockSpec((1, seq, W_LRU), lambda i: (i, 0, 0)),
        out_shape=jax.ShapeDtypeStruct((b, seq, W_LRU), BF16),
        scratch_shapes=[pltpu.VMEM((seq, W_LRU), F32), scan, scan, scan, scan,
                        pltpu.VMEM((2, W_LRU // LANES, LRU_SEGS, LANES), F32)],
        compiler_params=_cparams(("arbitrary",)),
        name="lru_mixer",
    )(zl, cw, cb, wcat, bcat, lam)


def _rms(x, g):
    ms = jnp.mean(x * x, axis=-1, keepdims=True)
    return x * lax.rsqrt(ms + NORM_EPS) * g


def _split_bf16(x):
    hi = x.astype(BF16)
    lo = (x - hi.astype(F32)).astype(BF16)
    return hi, lo


def _group_sums(x, ones):
    return _dot(x.astype(BF16), ones)


def _block_ones(n, block):
    ri = lax.broadcasted_iota(jnp.int32, (n, n), 0) // block
    ci = lax.broadcasted_iota(jnp.int32, (n, n), 1) // block
    return jnp.where(ri == ci, 1.0, 0.0).astype(BF16)


def _mla_kernel(z_ref, cos_ref, sin_ref, gcq_ref, wq_ref, gckv_ref, wkv_ref, gq_ref, gk_ref,
                o_ref, k_scr, v_scr, q_scr):
    seq = z_ref.shape[1]
    qi = pl.program_id(1)
    c_kv0 = Q_RANK
    c_rope0 = Q_RANK + KV_RANK
    head_w = QK_NOPE + 2 * QK_ROPE
    ones_pair = _block_ones(2 * LANES, LANES)

    @pl.when(jnp.logical_and(pl.program_id(0) == 0, qi == 0))
    def _():
        for h in range(MLA_HEADS):
            v_scr[h, :, V_HEAD:2 * V_HEAD] = jnp.ones((seq, V_HEAD), BF16)

    @pl.when(qi == 0)
    def _():
        gk = gk_ref[...]
        for c in range(seq // MLA_KV_ROWS):
            rows = slice(c * MLA_KV_ROWS, (c + 1) * MLA_KV_ROWS)
            ckv = _rms(z_ref[0, rows, c_kv0:c_rope0].astype(F32), gckv_ref[...])
            kv = _dot(ckv.astype(BF16), wkv_ref[...])
            t = z_ref[0, rows, c_rope0:c_rope0 + LANES].astype(F32)
            lane = lax.broadcasted_iota(jnp.int32, t.shape, 1)
            t_sq = jnp.where(lane < QK_ROPE, t * t, 0.0)
            half = QK_ROPE // 2
            swapped = jnp.where(lane < half, pltpu.roll(t, LANES - half, 1), pltpu.roll(t, half, 1))
            rope = t * cos_ref[rows, :] * gk[1:2] + swapped * sin_ref[rows, :] * gk[2:3]
            for h in range(MLA_HEADS):
                kn = kv[:, h * QK_NOPE:(h + 1) * QK_NOPE]
                ss = jnp.sum(kn * kn + t_sq, axis=-1, keepdims=True)
                inv = lax.rsqrt(ss * (1.0 / QK_HEAD) + NORM_EPS)
                k_scr[h, rows, 0:QK_NOPE] = (kn * inv * gk[0:1]).astype(BF16)
                k_scr[h, rows, QK_NOPE:head_w] = (rope * inv).astype(BF16)
                v0 = MLA_HEADS * QK_NOPE + h * V_HEAD
                v_scr[h, rows, 0:V_HEAD] = kv[:, v0:v0 + V_HEAD].astype(BF16)

    gq = gq_ref[...]
    scale = QK_HEAD ** -0.5 * np.log2(np.e)

    def prep_queries(tile, slot):
        qrows = pl.ds(pl.multiple_of(tile * MLA_Q_ROWS, MLA_Q_ROWS), MLA_Q_ROWS)
        cq = z_ref[0, qrows, 0:Q_RANK].astype(F32)
        ms = _group_sums(cq * cq, jnp.ones((Q_RANK, Q_RANK), BF16)) * (1.0 / Q_RANK)
        cq = cq * lax.rsqrt(ms + NORM_EPS) * gcq_ref[...]
        qall = _dot(cq.astype(BF16), wq_ref[...])
        cos_g = cos_ref[qrows, :] * gq[1:2]
        sin_g = sin_ref[qrows, :] * gq[2:3]
        sq = lambda h: (qall[:, h * head_w:h * head_w + QK_NOPE] ** 2
                        + 0.5 * qall[:, h * head_w + QK_NOPE:(h + 1) * head_w] ** 2)
        for h in range(MLA_HEADS):
            qn = qall[:, h * head_w:h * head_w + QK_NOPE]
            t = qall[:, h * head_w + QK_NOPE:(h + 1) * head_w]
            if h % 2 == 0:
                ss2 = _group_sums(jnp.concatenate([sq(h), sq(h + 1)], axis=-1), ones_pair)
            ss = ss2[:, (h % 2) * LANES:(h % 2 + 1) * LANES]
            inv = lax.rsqrt(ss * (1.0 / QK_HEAD) + NORM_EPS) * scale
            rope = t * cos_g + pltpu.roll(t, QK_ROPE, 1) * sin_g
            q_scr[slot, h] = jnp.concatenate([qn * inv * gq[0:1], rope * inv], axis=-1).astype(BF16)

    n_tiles = seq // MLA_Q_ROWS
    slot = qi % 2

    @pl.when(qi == 0)
    def _():
        prep_queries(0, 0)

    def prep_next():
        prep_queries(jnp.minimum(qi + 1, n_tiles - 1), 1 - slot)

    def write_head(h, ov):
        o_ref[0, :, h * V_HEAD:(h + 1) * V_HEAD] = (ov[:, 0:V_HEAD] / ov[:, V_HEAD:]).astype(BF16)

    g_max = lambda g: jnp.max(jnp.abs(g[0:2]))
    bound = scale * QK_HEAD * g_max(gq) * g_max(gk_ref[...])
    shift_is_safe = bound <= MLA_SAFE_SHIFT

    @pl.when(shift_is_safe)
    def _():
        for h in range(MLA_HEADS):
            ov = jnp.zeros((MLA_Q_ROWS, 2 * V_HEAD), F32)
            for kb in range(seq // MLA_KEY_BLOCK):
                keys = slice(kb * MLA_KEY_BLOCK, (kb + 1) * MLA_KEY_BLOCK)
                s = _dot_nt(q_scr[slot, h], k_scr[h, keys, :])
                ov = ov + _dot(jnp.exp2(s - bound).astype(BF16), v_scr[h, keys, :])
            write_head(h, ov)
        prep_next()

    @pl.when(jnp.logical_not(shift_is_safe))
    def _():
        for h in range(MLA_HEADS):
            s = _dot_nt(q_scr[slot, h], k_scr[h])
            m = jnp.max(s, axis=-1, keepdims=True)
            write_head(h, _dot(jnp.exp2(s - m).astype(BF16), v_scr[h]))
        prep_next()


def _mla_mixer(zm, cos, sin, gcq, wq, gckv, wkv, gq, gk):
    b, seq, _ = zm.shape
    head_w = QK_NOPE + 2 * QK_ROPE
    return pl.pallas_call(
        _mla_kernel,
        grid=(b, seq // MLA_Q_ROWS),
        in_specs=[pl.BlockSpec((1, seq, ZM_W), lambda i, j: (i, 0, 0)), _full(cos.shape), _full(sin.shape),
                  _full(gcq.shape), _full(wq.shape), _full(gckv.shape), _full(wkv.shape),
                  _full(gq.shape), _full(gk.shape)],
        out_specs=pl.BlockSpec((1, MLA_Q_ROWS, W_MLA), lambda i, j: (i, j, 0)),
        out_shape=jax.ShapeDtypeStruct((b, seq, W_MLA), BF16),
        scratch_shapes=[pltpu.VMEM((MLA_HEADS, seq, head_w), BF16),
                        pltpu.VMEM((MLA_HEADS, seq, 2 * V_HEAD), BF16),
                        pltpu.VMEM((2, MLA_HEADS, MLA_Q_ROWS, head_w), BF16)],
        compiler_params=_cparams(("arbitrary", "arbitrary")),
        name="mla_mixer",
    )(zm, cos, sin, gcq, wq, gckv, wkv, gq, gk)


def _log_sigmoid(x):
    return jnp.minimum(x, 0.0) - jnp.log(1.0 + jnp.exp(-jnp.abs(x)))


def _gla_kernel(z_ref, zgate_ref, wg_ref, bg_ref, go_ref, o_ref,
                oi_scr, qd_scr, ki_scr, ke_scr, u_scr, dec_scr, s_scr):
    seq = z_ref.shape[1]
    n_groups = seq // GLA_ROWS
    chunks = GLA_ROWS // GLA_CHUNK
    qk_w = GLA_HEADS * GLA_DK
    k0, v0, og0 = qk_w, 2 * qk_w, 2 * qk_w + W_GLA

    ri = lax.broadcasted_iota(jnp.int32, (GLA_ROWS, GLA_ROWS), 0)
    ci = lax.broadcasted_iota(jnp.int32, (GLA_ROWS, GLA_ROWS), 1)
    same_chunk = (ri // GLA_CHUNK) == (ci // GLA_CHUNK)
    causal = (jnp.where(same_chunk, ci, GLA_ROWS) <= ri, jnp.where(same_chunk, ci, -1) >= ri)
    cum_mat = jnp.where(causal[0], 1.0, 0.0).astype(BF16)
    avg_mat = jnp.where((ri // GLA_DV) == (ci // GLA_DV), 1.0 / GLA_DV, 0.0).astype(BF16)
    qk_head = lax.broadcasted_iota(jnp.int32, (GLA_ROWS, qk_w), 1) // GLA_DK
    v_head = lax.broadcasted_iota(jnp.int32, (GLA_ROWS, W_GLA), 1) // GLA_DV
    st_mask = (lax.broadcasted_iota(jnp.int32, (W_GLA, qk_w), 0) // GLA_DV
               == lax.broadcasted_iota(jnp.int32, (W_GLA, qk_w), 1) // GLA_DK)
    col_chunk = ci // GLA_CHUNK

    def rows_of(g):
        return pl.ds(pl.multiple_of(g * GLA_ROWS, GLA_ROWS), GLA_ROWS)

    def decays(g, _):
        rows = rows_of(g)
        x = _dot(zgate_ref[0, rows, :], wg_ref[...]) + bg_ref[...]
        la = _log_sigmoid(x) * (1.0 / GLA_TAU)
        la_hi, la_lo = _split_bf16(la)
        b_pre = _dot(cum_mat, la_hi) + _dot(cum_mat, la_lo)
        ends = [b_pre[(c + 1) * GLA_CHUNK - 1:(c + 1) * GLA_CHUNK] for c in range(chunks)]
        bt = jnp.concatenate([jnp.broadcast_to(e, (GLA_CHUNK, 2 * qk_w)) for e in ends], axis=0)
        b_dir = (b_pre[:, 0:qk_w], bt[:, qk_w:] - b_pre[:, qk_w:] + la[:, qk_w:])
        q = z_ref[0, rows, 0:qk_w].astype(F32) * (GLA_DK ** -0.5)
        k = z_ref[0, rows, k0:k0 + qk_w].astype(F32)
        for d in range(2):
            b = b_dir[d]
            cols = slice(d * qk_w, (d + 1) * qk_w)
            qd_scr[rows, cols] = (q * jnp.exp(b)).astype(BF16)
            ki_scr[rows, cols] = (k * jnp.exp(-b)).astype(BF16)
            ke_scr[rows, cols] = (k * jnp.exp(bt[:, cols] - b)).astype(BF16)
        dec_scr[g] = jnp.exp(jnp.concatenate(ends + ends, axis=0))
        return 0

    lax.fori_loop(0, n_groups, decays, 0, unroll=True)

    def local(g, _):
        rows = rows_of(g)
        v = z_ref[0, rows, v0:v0 + W_GLA]
        probs = None
        for d in range(2):
            cols = slice(d * qk_w, (d + 1) * qk_w)
            q_dec = qd_scr[rows, cols]
            q_stack = jnp.concatenate([jnp.where(qk_head == h, q_dec, jnp.zeros_like(q_dec))
                                       for h in range(GLA_HEADS)], axis=0)
            s = _dot_nt(q_stack, ki_scr[rows, cols])
            p = [jnp.where(causal[d], s[h * GLA_ROWS:(h + 1) * GLA_ROWS], 0.0) for h in range(GLA_HEADS)]
            probs = p if probs is None else [a + c for a, c in zip(probs, p)]
        v_stack = jnp.concatenate([jnp.where(v_head == h, v, jnp.zeros_like(v)) for h in range(GLA_HEADS)],
                                  axis=0)
        oi_scr[rows, :] = _dot(jnp.concatenate([p.astype(BF16) for p in probs], axis=1), v_stack)
        v_t = v.astype(F32).T
        vt_stack = jnp.concatenate([jnp.where(col_chunk == c, v_t, 0.0) for c in range(chunks)], axis=0)
        u_scr[g] = _dot(vt_stack.astype(BF16), ke_scr[rows, :])
        return 0

    lax.fori_loop(0, n_groups, local, 0, unroll=4)

    for d in range(2):
        def step(i, st, d=d):
            g = i if d == 0 else n_groups - 1 - i
            dec = dec_scr[g]
            for c in (range(chunks) if d == 0 else reversed(range(chunks))):
                blk = (c * 2 + d) * qk_w
                s_scr[g, :, blk:blk + qk_w] = st.astype(BF16)
                upd = u_scr[g, c * W_GLA:(c + 1) * W_GLA, d * qk_w:(d + 1) * qk_w]
                st = st * dec[c:c + 1, d * qk_w:(d + 1) * qk_w] + jnp.where(st_mask, upd, 0.0)
            return st

        lax.fori_loop(0, n_groups, step, jnp.zeros((W_GLA, qk_w), F32))

    def finish(g, _):
        rows = rows_of(g)
        inter = []
        for c in range(chunks):
            crow = pl.ds(pl.multiple_of(g * GLA_ROWS + c * GLA_CHUNK, GLA_CHUNK), GLA_CHUNK)
            inter.append(_dot_nt(qd_scr[crow, :], s_scr[g, :, c * 2 * qk_w:(c + 1) * 2 * qk_w]))
        o = oi_scr[rows, :] + jnp.concatenate(inter, axis=0)
        ms = _dot((o * o).astype(BF16), avg_mat)
        og = z_ref[0, rows, og0:og0 + W_GLA].astype(F32)
        y = (o * lax.rsqrt(ms + NORM_EPS) * go_ref[...]) * (og * _sigmoid(og))
        o_ref[0, rows, :] = y.astype(BF16)
        return 0

    lax.fori_loop(0, n_groups, finish, 0, unroll=True)


def _gla_mixer(zg, zm, wg, bg, go):
    b, seq, _ = zg.shape
    gate_block = (ZM_W - LANES) // LANES
    qk_w = GLA_HEADS * GLA_DK
    n_groups = seq // GLA_ROWS
    chunks = GLA_ROWS // GLA_CHUNK
    return pl.pallas_call(
        _gla_kernel,
        grid=(b,),
        in_specs=[pl.BlockSpec((1, seq, ZG_W), lambda i: (i, 0, 0)),
                  pl.BlockSpec((1, seq, LANES), lambda i: (i, 0, gate_block)),
                  _full(wg.shape), _full(bg.shape), _full(go.shape)],
        out_specs=pl.BlockSpec((1, seq, W_GLA), lambda i: (i, 0, 0)),
        out_shape=jax.ShapeDtypeStruct((b, seq, W_GLA), BF16),
        scratch_shapes=[pltpu.VMEM((seq, W_GLA), F32),
                        pltpu.VMEM((seq, 2 * qk_w), BF16),
                        pltpu.VMEM((seq, 2 * qk_w), BF16),
                        pltpu.VMEM((seq, 2 * qk_w), BF16),
                        pltpu.VMEM((n_groups, chunks * W_GLA, 2 * qk_w), F32),
                        pltpu.VMEM((n_groups, 2 * chunks, 2 * qk_w), F32),
                        pltpu.VMEM((n_groups, W_GLA, 2 * chunks * qk_w), BF16)],
        compiler_params=_cparams(("arbitrary",)),
        name="gla_mixer",
    )(zg, zm, wg, bg, go)


def _out_ffn_kernel(x_ref, ol_ref, om_ref, og_ref, wol_ref, wom_ref, wog_ref, g_ref,
                    wgu_ref, wdown_ref, o_ref, h_ref):
    for r in range(x_ref.shape[0] // FFN_SUB_ROWS):
        rows = slice(r * FFN_SUB_ROWS, (r + 1) * FFN_SUB_ROWS)
        x1 = (x_ref[rows, :] + _dot(ol_ref[rows, :], wol_ref[...]) + _dot(om_ref[rows, :], wom_ref[...])
              + _dot(og_ref[rows, :], wog_ref[...]))
        o_ref[rows, :] = x1
        h_ref[rows, :] = _rms(x1, g_ref[...]).astype(BF16)

    d_ff = wdown_ref.shape[0]
    for j in range(d_ff // FFN_COLS):
        cols = slice(j * FFN_COLS, (j + 1) * FFN_COLS)
        up_cols = slice(d_ff + j * FFN_COLS, d_ff + (j + 1) * FFN_COLS)
        gate = _dot(h_ref[...], wgu_ref[:, cols])
        up = _dot(h_ref[...], wgu_ref[:, up_cols])
        act = (gate * jax.nn.sigmoid(gate) * up).astype(BF16)
        o_ref[...] += _dot(act, wdown_ref[cols, :])


def _out_ffn(x2, ol, om, og, wol, wom, wog, g, wgu, wdown):
    n, d = x2.shape
    tm = min(FFN_ROWS, n)
    row = lambda i: (i, 0)
    resident = lambda a: pl.BlockSpec(a.shape, lambda i: (0,) * a.ndim, pipeline_mode=pl.Buffered(1))
    return pl.pallas_call(
        _out_ffn_kernel,
        grid=(n // tm,),
        in_specs=[pl.BlockSpec((tm, d), row), pl.BlockSpec((tm, W_LRU), row),
                  pl.BlockSpec((tm, W_MLA), row), pl.BlockSpec((tm, W_GLA), row),
                  resident(wol), resident(wom), resident(wog), resident(g),
                  resident(wgu), resident(wdown)],
        out_specs=pl.BlockSpec((tm, d), row),
        out_shape=jax.ShapeDtypeStruct((n, d), F32),
        scratch_shapes=[pltpu.VMEM((tm, d), BF16)],
        compiler_params=_cparams(("arbitrary",)),
        name="out_ffn",
    )(x2, ol, om, og, wol, wom, wog, g, wgu, wdown)


def _block_diag(w):
    nb, bi, bo = w.shape
    eye = jnp.eye(nb, dtype=w.dtype)
    return (w[:, :, None, :] * eye[:, None, :, None]).reshape(nb * bi, nb * bo)


def _swap_halves(n):
    return np.concatenate([np.arange(n // 2, n), np.arange(0, n // 2)])


def _prep_layer(p):
    row = lambda v: v.reshape(1, -1).astype(F32)
    d_model = p["w_in"].shape[0]
    qk_w = GLA_HEADS * GLA_DK
    o_cq = 2 * W_LRU
    o_ckv = o_cq + Q_RANK
    o_rope = o_ckv + KV_RANK
    o_glaq = o_rope + QK_ROPE
    o_gf = o_glaq + 2 * qk_w + W_GLA
    o_og = o_gf + 2 * GLA_GATE_RANK
    sw = _swap_halves(QK_ROPE)
    w = p["w_in"]
    pad = LANES - QK_ROPE - 2 * GLA_GATE_RANK
    w_in = jnp.concatenate([
        w[:, :o_glaq], w[:, o_gf:o_og], jnp.zeros((d_model, pad), w.dtype),
        w[:, o_glaq:o_gf], w[:, o_og:],
    ], axis=1).astype(BF16)

    wcat = (0.5 * jnp.concatenate([_block_diag(p[k]) for k in ("lru_wa_f", "lru_wx_f", "lru_wa_b", "lru_wx_b")],
                                  axis=1)).astype(BF16)
    bcat = 0.5 * jnp.concatenate([p[k] for k in ("lru_ba_f", "lru_bx_f", "lru_ba_b", "lru_bx_b")]).reshape(1, -1)
    lam = jnp.stack([p["lru_lam_f"], p["lru_lam_b"]])

    wq = p["mla_w_uq"].reshape(Q_RANK, MLA_HEADS, QK_HEAD)
    wq = jnp.concatenate([wq, wq[:, :, QK_NOPE + sw]], axis=2).reshape(Q_RANK, -1).astype(BF16)
    wkv = p["mla_w_ukv"].reshape(KV_RANK, MLA_HEADS, QK_NOPE + V_HEAD)
    wkv = jnp.concatenate([wkv[:, :, :QK_NOPE].reshape(KV_RANK, -1),
                           wkv[:, :, QK_NOPE:].reshape(KV_RANK, -1)], axis=1).astype(BF16)

    def qk_gains(g):
        pad = jnp.zeros((LANES - QK_ROPE,), F32)
        return jnp.stack([g[:QK_NOPE], jnp.concatenate([g[QK_NOPE:], pad]),
                          jnp.concatenate([g[QK_NOPE + sw], pad])]).astype(F32)

    wg = jnp.zeros((LANES, 2 * qk_w), F32)
    wg = wg.at[GATE_LANE0:GATE_LANE0 + GLA_GATE_RANK, :qk_w].set(p["gla_wa2_f"])
    wg = wg.at[GATE_LANE0 + GLA_GATE_RANK:GATE_LANE0 + 2 * GLA_GATE_RANK, qk_w:].set(p["gla_wa2_b"])
    bg = jnp.concatenate([p["gla_ba2_f"], p["gla_ba2_b"]]).reshape(1, -1)

    w_out = p["w_out"].astype(BF16)
    wgu = p["w_ffn_in"].astype(BF16)
    wdown = p["w_ffn_out"].astype(BF16)
    return dict(
        g_mix=row(p["g_mix"]), w_in=w_in,
        conv_w=p["conv_w"], conv_b=row(p["conv_b"]), wcat=wcat, bcat=bcat, lam=lam,
        gcq=row(p["mla_g_cq"]), wq=wq, gckv=row(p["mla_g_ckv"]), wkv=wkv,
        gq=qk_gains(p["mla_g_q"]), gk=qk_gains(p["mla_g_k"]),
        wg=wg.astype(BF16), bg=bg, go=row(jnp.tile(p["gla_g_o"], GLA_HEADS)),
        wol=w_out[:W_LRU], wom=w_out[W_LRU:W_LRU + W_MLA], wog=w_out[W_LRU + W_MLA:],
        g_ffn=row(p["g_ffn"]), wgu=wgu, wdown=wdown,
    )


def _rotary_tables(seq):
    half = QK_ROPE // 2
    inv = 1.0 / (ROPE_THETA ** (jnp.arange(half, dtype=F32) * 2.0 / QK_ROPE))
    ang = jnp.arange(seq, dtype=F32)[:, None] * inv[None, :]
    cos, sin = jnp.cos(ang), jnp.sin(ang)
    pad = jnp.zeros((seq, LANES - QK_ROPE), F32)
    return jnp.concatenate([cos, cos, pad], axis=1), jnp.concatenate([-sin, sin, pad], axis=1)


def _layer(x, cos, sin, lp):
    b, seq, d = x.shape
    x2 = x.reshape(b * seq, d)
    zl, zm, zg = _in_proj(x2, lp["g_mix"], lp["w_in"])
    o_lru = _lru_mixer(zl.reshape(b, seq, ZL_W), lp["conv_w"], lp["conv_b"], lp["wcat"], lp["bcat"], lp["lam"])
    o_mla = _mla_mixer(zm.reshape(b, seq, ZM_W), cos, sin, lp["gcq"], lp["wq"], lp["gckv"], lp["wkv"],
                       lp["gq"], lp["gk"])
    o_gla = _gla_mixer(zg.reshape(b, seq, ZG_W), zm.reshape(b, seq, ZM_W), lp["wg"], lp["bg"], lp["go"])
    y = _out_ffn(x2, o_lru.reshape(b * seq, W_LRU), o_mla.reshape(b * seq, W_MLA),
                 o_gla.reshape(b * seq, W_GLA), lp["wol"], lp["wom"], lp["wog"], lp["g_ffn"],
                 lp["wgu"], lp["wdown"])
    return y.reshape(b, seq, d)


_PARAM_NAMES = ("g_mix", "w_in", "conv_w", "conv_b",
                "lru_wa_f", "lru_ba_f", "lru_wx_f", "lru_bx_f", "lru_lam_f",
                "lru_wa_b", "lru_ba_b", "lru_wx_b", "lru_bx_b", "lru_lam_b",
                "mla_g_cq", "mla_w_uq", "mla_g_ckv", "mla_w_ukv", "mla_g_q", "mla_g_k",
                "gla_wa2_f", "gla_ba2_f", "gla_wa2_b", "gla_ba2_b", "gla_g_o",
                "w_out", "g_ffn", "w_ffn_in", "w_ffn_out")


def kernel(x_prompt, x_sample, g_mix, w_in, conv_w, conv_b, lru_wa_f, lru_ba_f, lru_wx_f, lru_bx_f, lru_lam_f, lru_wa_b, lru_ba_b, lru_wx_b, lru_bx_b, lru_lam_b, mla_g_cq, mla_w_uq, mla_g_ckv, mla_w_ukv, mla_g_q, mla_g_k, gla_wa2_f, gla_ba2_f, gla_wa2_b, gla_ba2_b, gla_g_o, w_out, g_ffn, w_ffn_in, w_ffn_out):
    stacked = dict(zip(_PARAM_NAMES, (
        g_mix, w_in, conv_w, conv_b, lru_wa_f, lru_ba_f, lru_wx_f, lru_bx_f, lru_lam_f,
        lru_wa_b, lru_ba_b, lru_wx_b, lru_bx_b, lru_lam_b,
        mla_g_cq, mla_w_uq, mla_g_ckv, mla_w_ukv, mla_g_q, mla_g_k,
        gla_wa2_f, gla_ba2_f, gla_wa2_b, gla_ba2_b, gla_g_o, w_out, g_ffn, w_ffn_in, w_ffn_out)))
    depth = g_mix.shape[0]
    layers = [_prep_layer({k: v[l] for k, v in stacked.items()}) for l in range(depth)]

    def trunk(x):
        cos, sin = _rotary_tables(x.shape[1])
        for lp in layers:
            x = _layer(x, cos, sin, lp)
        return x

    return trunk(x_prompt), trunk(x_sample)
```

```python
import jax
import jax.numpy as jnp
import numpy as np
from jax import lax
from jax.experimental import pallas as pl
from jax.experimental.pallas import tpu as pltpu

F32 = jnp.float32
BF16 = jnp.bfloat16

NORM_EPS = 1e-6
W_LRU = 256
W_MLA = 512
W_GLA = 256
LRU_BLOCKS = 8
LRU_C = 8.0
CONV_W = 4
CONV_LEFT = 2
MLA_HEADS = 4
QK_NOPE = 128
QK_ROPE = 64
QK_HEAD = QK_NOPE + QK_ROPE
V_HEAD = 128
Q_RANK = 256
KV_RANK = 128
ROPE_THETA = 10000.0
GLA_HEADS = 4
GLA_DK = 32
GLA_DV = 64
GLA_GATE_RANK = 16
GLA_TAU = 16.0
GLA_CHUNK = 64

LANES = 128
SUBLANES = 8
VMEM_LIMIT_BYTES = 56 * 1024 * 1024

ZL_W = 2 * W_LRU
ZM_W = Q_RANK + KV_RANK + LANES
ZG_W = 2 * GLA_HEADS * GLA_DK + W_GLA + W_GLA
GATE_LANE0 = QK_ROPE

IN_ROWS = 2048
IN_SUB_ROWS = 256
FFN_ROWS = 1024
FFN_COLS = 256
FFN_SUB_ROWS = 256
MLA_Q_ROWS = 512
MLA_KV_ROWS = 1024
MLA_KEY_BLOCK = 1024
MLA_SAFE_SHIFT = 60.0
GLA_ROWS = 256
LRU_SEGS = SUBLANES


def _dot(a, b):
    return jnp.dot(a, b, preferred_element_type=F32)


def _dot_nt(a, b):
    return lax.dot_general(a, b, (((1,), (1,)), ((), ())), preferred_element_type=F32)


def _cparams(sem):
    return pltpu.CompilerParams(dimension_semantics=sem, vmem_limit_bytes=VMEM_LIMIT_BYTES)


def _full(shape):
    return pl.BlockSpec(shape, lambda *_: (0,) * len(shape))


def _in_proj_kernel(x_ref, g_ref, w_ref, zl_ref, zm_ref, zg_ref):
    for r in range(x_ref.shape[0] // IN_SUB_ROWS):
        rows = slice(r * IN_SUB_ROWS, (r + 1) * IN_SUB_ROWS)
        h = _rms(x_ref[rows, :], g_ref[...]).astype(BF16)
        zl_ref[rows, :] = _dot(h, w_ref[:, 0:ZL_W]).astype(BF16)
        zm_ref[rows, :] = _dot(h, w_ref[:, ZL_W:ZL_W + ZM_W]).astype(BF16)
        zg_ref[rows, :] = _dot(h, w_ref[:, ZL_W + ZM_W:]).astype(BF16)


def _in_proj(x2, g, w):
    n, d = x2.shape
    tm = min(IN_ROWS, n)
    row = lambda i: (i, 0)
    return pl.pallas_call(
        _in_proj_kernel,
        grid=(n // tm,),
        in_specs=[pl.BlockSpec((tm, d), row), _full(g.shape), _full(w.shape)],
        out_specs=[pl.BlockSpec((tm, ZL_W), row), pl.BlockSpec((tm, ZM_W), row),
                   pl.BlockSpec((tm, ZG_W), row)],
        out_shape=[jax.ShapeDtypeStruct((n, ZL_W), BF16), jax.ShapeDtypeStruct((n, ZM_W), BF16),
                   jax.ShapeDtypeStruct((n, ZG_W), BF16)],
        compiler_params=_cparams(("arbitrary",)),
        name="in_proj",
    )(x2, g, w)


def _softplus(x):
    return jnp.maximum(x, 0.0) + jnp.log1p(jnp.exp(-jnp.abs(x)))


def _sigmoid(x):
    return 0.5 * jnp.tanh(0.5 * x) + 0.5


def _lru_kernel(z_ref, cw_ref, cb_ref, wcat_ref, bcat_ref, lam_ref, o_ref,
                u_ref, af_ref, vf_ref, ab_ref, vb_ref, c_ref):
    seq = z_ref.shape[1]
    seg = seq // LRU_SEGS
    pitch = seg + SUBLANES
    halves = W_LRU // LANES

    xin = z_ref[0, :, 0:W_LRU].astype(F32)
    row = lax.broadcasted_iota(jnp.int32, (seq, W_LRU), 0)
    cw = cw_ref[...]
    u = xin * cw[CONV_LEFT:CONV_LEFT + 1]
    for j in range(CONV_W):
        off = j - CONV_LEFT
        if off == 0:
            continue
        shifted = pltpu.roll(xin, (-off) % seq, 0)
        valid = (row >= -off) if off < 0 else (row < seq - off)
        u = u + jnp.where(valid, shifted, 0.0) * cw[j:j + 1]
    u_ref[...] = u + cb_ref[...]

    half_c_sp = (-0.5 * LRU_C) * _softplus(-lam_ref[...])

    def gates(s, _):
        r0 = pl.multiple_of(s * seg, seg)
        p0 = pl.multiple_of(s * pitch, SUBLANES)
        uc = u_ref[pl.ds(r0, seg), :]
        g = _dot(uc.astype(BF16), wcat_ref[...]) + bcat_ref[...]
        half_u = 0.5 * uc
        for d, (a_ref, v_ref) in enumerate(((af_ref, vf_ref), (ab_ref, vb_ref))):
            base = d * 2 * W_LRU
            t_r = jnp.tanh(g[:, base:base + W_LRU])
            t_i = jnp.tanh(g[:, base + W_LRU:base + 2 * W_LRU])
            log_a = t_r * half_c_sp[d:d + 1] + half_c_sp[d:d + 1]
            a = jnp.exp(log_a)
            v = jnp.sqrt(-jnp.tanh(log_a) * (a * a + 1.0)) * ((t_i + 1.0) * half_u)
            for hh in range(halves):
                a_ref[hh, pl.ds(p0, seg), :] = a[:, hh * LANES:(hh + 1) * LANES]
                v_ref[hh, pl.ds(p0, seg), :] = v[:, hh * LANES:(hh + 1) * LANES]
        return 0

    lax.fori_loop(0, LRU_SEGS, gates, 0, unroll=4)

    def rows_at(i):
        return pl.ds(i, LRU_SEGS, stride=pitch)

    def scan(i, carry):
        hf, pf, hb, pb = carry
        ib = seg - 1 - i
        hf_n, pf_n, hb_n, pb_n = [], [], [], []
        for hh in range(halves):
            a = af_ref[hh, rows_at(i), :]
            h = a * hf[hh] + vf_ref[hh, rows_at(i), :]
            p = a * pf[hh]
            vf_ref[hh, rows_at(i), :] = h
            af_ref[hh, rows_at(i), :] = p
            hf_n.append(h)
            pf_n.append(p)
            a = ab_ref[hh, rows_at(ib), :]
            h = a * hb[hh] + vb_ref[hh, rows_at(ib), :]
            p = a * pb[hh]
            vb_ref[hh, rows_at(ib), :] = h
            ab_ref[hh, rows_at(ib), :] = p
            hb_n.append(h)
            pb_n.append(p)
        return tuple(hf_n), tuple(pf_n), tuple(hb_n), tuple(pb_n)

    zeros = tuple(jnp.zeros((LRU_SEGS, LANES), F32) for _ in range(halves))
    ones = tuple(jnp.ones((LRU_SEGS, LANES), F32) for _ in range(halves))
    hf_e, pf_e, hb_e, pb_e = lax.fori_loop(0, seg, scan, (zeros, ones, zeros, ones), unroll=4)

    srow = lax.broadcasted_iota(jnp.int32, (LRU_SEGS, LANES), 0)
    for hh in range(halves):
        c = jnp.zeros((LRU_SEGS, LANES), F32)
        for _ in range(LRU_SEGS - 1):
            c = jnp.where(srow == 0, 0.0, pltpu.roll(hf_e[hh] + pf_e[hh] * c, 1, 0))
        c_ref[0, hh] = c
        c = jnp.zeros((LRU_SEGS, LANES), F32)
        for _ in range(LRU_SEGS - 1):
            c = jnp.where(srow == LRU_SEGS - 1, 0.0,
                          pltpu.roll(hb_e[hh] + pb_e[hh] * c, LRU_SEGS - 1, 0))
        c_ref[1, hh] = c

    def finish(s, _):
        r0 = pl.multiple_of(s * seg, seg)
        p0 = pl.multiple_of(s * pitch, SUBLANES)
        for hh in range(halves):
            h = (vf_ref[hh, pl.ds(p0, seg), :] + af_ref[hh, pl.ds(p0, seg), :] * c_ref[0, hh, pl.ds(s, 1), :]
                 + vb_ref[hh, pl.ds(p0, seg), :] + ab_ref[hh, pl.ds(p0, seg), :] * c_ref[1, hh, pl.ds(s, 1), :])
            gate = z_ref[0, pl.ds(r0, seg), W_LRU + hh * LANES:W_LRU + (hh + 1) * LANES].astype(F32)
            o_ref[0, pl.ds(r0, seg), hh * LANES:(hh + 1) * LANES] = (h * jax.nn.gelu(gate)).astype(BF16)
        return 0

    lax.fori_loop(0, LRU_SEGS, finish, 0, unroll=2)


def _lru_mixer(zl, cw, cb, wcat, bcat, lam):
    b, seq, _ = zl.shape
    pitch = seq // LRU_SEGS + SUBLANES
    scan = pltpu.VMEM((W_LRU // LANES, LRU_SEGS * pitch, LANES), F32)
    return pl.pallas_call(
        _lru_kernel,
        grid=(b,),
        in_specs=[pl.BlockSpec((1, seq, ZL_W), lambda i: (i, 0, 0)), _full(cw.shape), _full(cb.shape),
                  _full(wcat.shape), _full(bcat.shape), _full(lam.shape)],
        out_specs=pl.BlockSpec((1, seq, W_LRU), lambda i: (i, 0, 0)),
        out_shape=jax.ShapeDtypeStruct((b, seq, W_LRU), BF16),
        scratch_shapes=[pltpu.VMEM((seq, W_LRU), F32), scan, scan, scan, scan,
                        pltpu.VMEM((2, W_LRU // LANES, LRU_SEGS, LANES), F32)],
        compiler_params=_cparams(("arbitrary",)),
        name="lru_mixer",
    )(zl, cw, cb, wcat, bcat, lam)


def _rms(x, g):
    ms = jnp.mean(x * x, axis=-1, keepdims=True)
    return x * lax.rsqrt(ms + NORM_EPS) * g


def _split_bf16(x):
    hi = x.astype(BF16)
    lo = (x - hi.astype(F32)).astype(BF16)
    return hi, lo


def _group_sums(x, ones):
    return _dot(x.astype(BF16), ones)


def _block_ones(n, block):
    ri = lax.broadcasted_iota(jnp.int32, (n, n), 0) // block
    ci = lax.broadcasted_iota(jnp.int32, (n, n), 1) // block
    return jnp.where(ri == ci, 1.0, 0.0).astype(BF16)


def _mla_kernel(z_ref, cos_ref, sin_ref, gcq_ref, wq_ref, gckv_ref, wkv_ref, gq_ref, gk_ref,
                o_ref, k_scr, v_scr, q_scr):
    seq = z_ref.shape[1]
    c_kv0 = Q_RANK
    c_rope0 = Q_RANK + KV_RANK
    head_w = QK_NOPE + 2 * QK_ROPE
    ones_pair = _block_ones(2 * LANES, LANES)

    @pl.when(pl.program_id(0) == 0)
    def _():
        for h in range(MLA_HEADS):
            v_scr[h, :, V_HEAD:2 * V_HEAD] = jnp.ones((seq, V_HEAD), BF16)

    every_step = pl.program_id(0) >= 0

    @pl.when(every_step)
    def _():
        gk = gk_ref[...]
        for c in range(seq // MLA_KV_ROWS):
            rows = slice(c * MLA_KV_ROWS, (c + 1) * MLA_KV_ROWS)
            ckv = _rms(z_ref[0, rows, c_kv0:c_rope0].astype(F32), gckv_ref[...])
            kv = _dot(ckv.astype(BF16), wkv_ref[...])
            t = z_ref[0, rows, c_rope0:c_rope0 + LANES].astype(F32)
            lane = lax.broadcasted_iota(jnp.int32, t.shape, 1)
            t_sq = jnp.where(lane < QK_ROPE, t * t, 0.0)
            half = QK_ROPE // 2
            swapped = jnp.where(lane < half, pltpu.roll(t, LANES - half, 1), pltpu.roll(t, half, 1))
            rope = t * cos_ref[rows, :] * gk[1:2] + swapped * sin_ref[rows, :] * gk[2:3]
            for h in range(MLA_HEADS):
                kn = kv[:, h * QK_NOPE:(h + 1) * QK_NOPE]
                ss = jnp.sum(kn * kn + t_sq, axis=-1, keepdims=True)
                inv = lax.rsqrt(ss * (1.0 / QK_HEAD) + NORM_EPS)
                k_scr[h, rows, 0:QK_NOPE] = (kn * inv * gk[0:1]).astype(BF16)
                k_scr[h, rows, QK_NOPE:head_w] = (rope * inv).astype(BF16)
                v0 = MLA_HEADS * QK_NOPE + h * V_HEAD
                v_scr[h, rows, 0:V_HEAD] = kv[:, v0:v0 + V_HEAD].astype(BF16)

    gq = gq_ref[...]
    scale = QK_HEAD ** -0.5 * np.log2(np.e)

    def prep_queries(tile, slot):
        qrows = pl.ds(pl.multiple_of(tile * MLA_Q_ROWS, MLA_Q_ROWS), MLA_Q_ROWS)
        cq = z_ref[0, qrows, 0:Q_RANK].astype(F32)
        ms = _group_sums(cq * cq, jnp.ones((Q_RANK, Q_RANK), BF16)) * (1.0 / Q_RANK)
        cq = cq * lax.rsqrt(ms + NORM_EPS) * gcq_ref[...]
        qall = _dot(cq.astype(BF16), wq_ref[...])
        cos_g = cos_ref[qrows, :] * gq[1:2]
        sin_g = sin_ref[qrows, :] * gq[2:3]
        sq = lambda h: (qall[:, h * head_w:h * head_w + QK_NOPE] ** 2
                        + 0.5 * qall[:, h * head_w + QK_NOPE:(h + 1) * head_w] ** 2)
        for h in range(MLA_HEADS):
            qn = qall[:, h * head_w:h * head_w + QK_NOPE]
            t = qall[:, h * head_w + QK_NOPE:(h + 1) * head_w]
            if h % 2 == 0:
                ss2 = _group_sums(jnp.concatenate([sq(h), sq(h + 1)], axis=-1), ones_pair)
            ss = ss2[:, (h % 2) * LANES:(h % 2 + 1) * LANES]
            inv = lax.rsqrt(ss * (1.0 / QK_HEAD) + NORM_EPS) * scale
            rope = t * cos_g + pltpu.roll(t, QK_ROPE, 1) * sin_g
            q_scr[slot, h] = jnp.concatenate([qn * inv * gq[0:1], rope * inv], axis=-1).astype(BF16)

    g_max = lambda g: jnp.max(jnp.abs(g[0:2]))
    bound = scale * QK_HEAD * g_max(gq) * g_max(gk_ref[...])
    shift_is_safe = bound <= MLA_SAFE_SHIFT

    n_tiles = seq // MLA_Q_ROWS

    @pl.when(every_step)
    def _():
        prep_queries(0, 0)

    def attend_tile(qi, _):
        slot = qi % 2
        qrows = pl.ds(pl.multiple_of(qi * MLA_Q_ROWS, MLA_Q_ROWS), MLA_Q_ROWS)

        def prep_next():
            prep_queries(jnp.minimum(qi + 1, n_tiles - 1), 1 - slot)

        def write_head(h, ov):
            o_ref[0, qrows, h * V_HEAD:(h + 1) * V_HEAD] = (ov[:, 0:V_HEAD] / ov[:, V_HEAD:]).astype(BF16)

        @pl.when(shift_is_safe)
        def _():
            for h in range(MLA_HEADS):
                ov = jnp.zeros((MLA_Q_ROWS, 2 * V_HEAD), F32)
                for kb in range(seq // MLA_KEY_BLOCK):
                    keys = slice(kb * MLA_KEY_BLOCK, (kb + 1) * MLA_KEY_BLOCK)
                    s = _dot_nt(q_scr[slot, h], k_scr[h, keys, :])
                    ov = ov + _dot(jnp.exp2(s - bound).astype(BF16), v_scr[h, keys, :])
                write_head(h, ov)
            prep_next()

        @pl.when(jnp.logical_not(shift_is_safe))
        def _():
            for h in range(MLA_HEADS):
                s = _dot_nt(q_scr[slot, h], k_scr[h])
                m = jnp.max(s, axis=-1, keepdims=True)
                write_head(h, _dot(jnp.exp2(s - m).astype(BF16), v_scr[h]))
            prep_next()

        return 0

    lax.fori_loop(0, n_tiles, attend_tile, 0)


def _mla_mixer(zm, cos, sin, gcq, wq, gckv, wkv, gq, gk):
    b, seq, _ = zm.shape
    head_w = QK_NOPE + 2 * QK_ROPE
    return pl.pallas_call(
        _mla_kernel,
        grid=(b,),
        in_specs=[pl.BlockSpec((1, seq, ZM_W), lambda i: (i, 0, 0)), _full(cos.shape), _full(sin.shape),
                  _full(gcq.shape), _full(wq.shape), _full(gckv.shape), _full(wkv.shape),
                  _full(gq.shape), _full(gk.shape)],
        out_specs=pl.BlockSpec((1, seq, W_MLA), lambda i: (i, 0, 0)),
        out_shape=jax.ShapeDtypeStruct((b, seq, W_MLA), BF16),
        scratch_shapes=[pltpu.VMEM((MLA_HEADS, seq, head_w), BF16),
                        pltpu.VMEM((MLA_HEADS, seq, 2 * V_HEAD), BF16),
                        pltpu.VMEM((2, MLA_HEADS, MLA_Q_ROWS, head_w), BF16)],
        compiler_params=_cparams(("arbitrary",)),
        name="mla_mixer",
    )(zm, cos, sin, gcq, wq, gckv, wkv, gq, gk)


def _log_sigmoid(x):
    return jnp.minimum(x, 0.0) - jnp.log(1.0 + jnp.exp(-jnp.abs(x)))


def _gla_kernel(z_ref, zgate_ref, wg_ref, bg_ref, go_ref, o_ref,
                oi_scr, qd_scr, ki_scr, ke_scr, u_scr, dec_scr, s_scr):
    seq = z_ref.shape[1]
    n_groups = seq // GLA_ROWS
    chunks = GLA_ROWS // GLA_CHUNK
    qk_w = GLA_HEADS * GLA_DK
    k0, v0, og0 = qk_w, 2 * qk_w, 2 * qk_w + W_GLA

    ri = lax.broadcasted_iota(jnp.int32, (GLA_ROWS, GLA_ROWS), 0)
    ci = lax.broadcasted_iota(jnp.int32, (GLA_ROWS, GLA_ROWS), 1)
    same_chunk = (ri // GLA_CHUNK) == (ci // GLA_CHUNK)
    causal = (jnp.where(same_chunk, ci, GLA_ROWS) <= ri, jnp.where(same_chunk, ci, -1) >= ri)
    cum_mat = jnp.where(causal[0], 1.0, 0.0).astype(BF16)
    avg_mat = jnp.where((ri // GLA_DV) == (ci // GLA_DV), 1.0 / GLA_DV, 0.0).astype(BF16)
    qk_head = lax.broadcasted_iota(jnp.int32, (GLA_ROWS, qk_w), 1) // GLA_DK
    v_head = lax.broadcasted_iota(jnp.int32, (GLA_ROWS, W_GLA), 1) // GLA_DV
    st_mask = (lax.broadcasted_iota(jnp.int32, (W_GLA, qk_w), 0) // GLA_DV
               == lax.broadcasted_iota(jnp.int32, (W_GLA, qk_w), 1) // GLA_DK)
    col_chunk = ci // GLA_CHUNK

    def rows_of(g):
        return pl.ds(pl.multiple_of(g * GLA_ROWS, GLA_ROWS), GLA_ROWS)

    def decays(g, _):
        rows = rows_of(g)
        x = _dot(zgate_ref[0, rows, :], wg_ref[...]) + bg_ref[...]
        la = _log_sigmoid(x) * (1.0 / GLA_TAU)
        la_hi, la_lo = _split_bf16(la)
        b_pre = _dot(cum_mat, la_hi) + _dot(cum_mat, la_lo)
        ends = [b_pre[(c + 1) * GLA_CHUNK - 1:(c + 1) * GLA_CHUNK] for c in range(chunks)]
        bt = jnp.concatenate([jnp.broadcast_to(e, (GLA_CHUNK, 2 * qk_w)) for e in ends], axis=0)
        b_dir = (b_pre[:, 0:qk_w], bt[:, qk_w:] - b_pre[:, qk_w:] + la[:, qk_w:])
        q = z_ref[0, rows, 0:qk_w].astype(F32) * (GLA_DK ** -0.5)
        k = z_ref[0, rows, k0:k0 + qk_w].astype(F32)
        for d in range(2):
            b = b_dir[d]
            cols = slice(d * qk_w, (d + 1) * qk_w)
            qd_scr[rows, cols] = (q * jnp.exp(b)).astype(BF16)
            ki_scr[rows, cols] = (k * jnp.exp(-b)).astype(BF16)
            ke_scr[rows, cols] = (k * jnp.exp(bt[:, cols] - b)).astype(BF16)
        dec_scr[g] = jnp.exp(jnp.concatenate(ends + ends, axis=0))
        return 0

    lax.fori_loop(0, n_groups, decays, 0, unroll=True)

    def local(g, _):
        rows = rows_of(g)
        v = z_ref[0, rows, v0:v0 + W_GLA]
        probs = None
        for d in range(2):
            cols = slice(d * qk_w, (d + 1) * qk_w)
            q_dec = qd_scr[rows, cols]
            q_stack = jnp.concatenate([jnp.where(qk_head == h, q_dec, jnp.zeros_like(q_dec))
                                       for h in range(GLA_HEADS)], axis=0)
            s = _dot_nt(q_stack, ki_scr[rows, cols])
            p = [jnp.where(causal[d], s[h * GLA_ROWS:(h + 1) * GLA_ROWS], 0.0) for h in range(GLA_HEADS)]
            probs = p if probs is None else [a + c for a, c in zip(probs, p)]
        v_stack = jnp.concatenate([jnp.where(v_head == h, v, jnp.zeros_like(v)) for h in range(GLA_HEADS)],
                                  axis=0)
        oi_scr[rows, :] = _dot(jnp.concatenate([p.astype(BF16) for p in probs], axis=1), v_stack)
        v_t = v.astype(F32).T
        vt_stack = jnp.concatenate([jnp.where(col_chunk == c, v_t, 0.0) for c in range(chunks)], axis=0)
        u_scr[g] = _dot(vt_stack.astype(BF16), ke_scr[rows, :])
        return 0

    lax.fori_loop(0, n_groups, local, 0, unroll=4)

    for d in range(2):
        def step(i, st, d=d):
            g = i if d == 0 else n_groups - 1 - i
            dec = dec_scr[g]
            for c in (range(chunks) if d == 0 else reversed(range(chunks))):
                blk = (c * 2 + d) * qk_w
                s_scr[g, :, blk:blk + qk_w] = st.astype(BF16)
                upd = u_scr[g, c * W_GLA:(c + 1) * W_GLA, d * qk_w:(d + 1) * qk_w]
                st = st * dec[c:c + 1, d * qk_w:(d + 1) * qk_w] + jnp.where(st_mask, upd, 0.0)
            return st

        lax.fori_loop(0, n_groups, step, jnp.zeros((W_GLA, qk_w), F32))

    def finish(g, _):
        rows = rows_of(g)
        inter = []
        for c in range(chunks):
            crow = pl.ds(pl.multiple_of(g * GLA_ROWS + c * GLA_CHUNK, GLA_CHUNK), GLA_CHUNK)
            inter.append(_dot_nt(qd_scr[crow, :], s_scr[g, :, c * 2 * qk_w:(c + 1) * 2 * qk_w]))
        o = oi_scr[rows, :] + jnp.concatenate(inter, axis=0)
        ms = _dot((o * o).astype(BF16), avg_mat)
        og = z_ref[0, rows, og0:og0 + W_GLA].astype(F32)
        y = (o * lax.rsqrt(ms + NORM_EPS) * go_ref[...]) * (og * _sigmoid(og))
        o_ref[0, rows, :] = y.astype(BF16)
        return 0

    lax.fori_loop(0, n_groups, finish, 0, unroll=True)


def _gla_mixer(zg, zm, wg, bg, go):
    b, seq, _ = zg.shape
    gate_block = (ZM_W - LANES) // LANES
    qk_w = GLA_HEADS * GLA_DK
    n_groups = seq // GLA_ROWS
    chunks = GLA_ROWS // GLA_CHUNK
    return pl.pallas_call(
        _gla_kernel,
        grid=(b,),
        in_specs=[pl.BlockSpec((1, seq, ZG_W), lambda i: (i, 0, 0)),
                  pl.BlockSpec((1, seq, LANES), lambda i: (i, 0, gate_block)),
                  _full(wg.shape), _full(bg.shape), _full(go.shape)],
        out_specs=pl.BlockSpec((1, seq, W_GLA), lambda i: (i, 0, 0)),
        out_shape=jax.ShapeDtypeStruct((b, seq, W_GLA), BF16),
        scratch_shapes=[pltpu.VMEM((seq, W_GLA), F32),
                        pltpu.VMEM((seq, 2 * qk_w), BF16),
                        pltpu.VMEM((seq, 2 * qk_w), BF16),
                        pltpu.VMEM((seq, 2 * qk_w), BF16),
                        pltpu.VMEM((n_groups, chunks * W_GLA, 2 * qk_w), F32),
                        pltpu.VMEM((n_groups, 2 * chunks, 2 * qk_w), F32),
                        pltpu.VMEM((n_groups, W_GLA, 2 * chunks * qk_w), BF16)],
        compiler_params=_cparams(("arbitrary",)),
        name="gla_mixer",
    )(zg, zm, wg, bg, go)


def _out_ffn_kernel(x_ref, ol_ref, om_ref, og_ref, wol_ref, wom_ref, wog_ref, g_ref,
                    wgu_ref, wdown_ref, o_ref, h_ref):
    for r in range(x_ref.shape[0] // FFN_SUB_ROWS):
        rows = slice(r * FFN_SUB_ROWS, (r + 1) * FFN_SUB_ROWS)
        x1 = (x_ref[rows, :] + _dot(ol_ref[rows, :], wol_ref[...]) + _dot(om_ref[rows, :], wom_ref[...])
              + _dot(og_ref[rows, :], wog_ref[...]))
        o_ref[rows, :] = x1
        h_ref[rows, :] = _rms(x1, g_ref[...]).astype(BF16)

    d_ff = wdown_ref.shape[0]
    for j in range(d_ff // FFN_COLS):
        cols = slice(j * FFN_COLS, (j + 1) * FFN_COLS)
        up_cols = slice(d_ff + j * FFN_COLS, d_ff + (j + 1) * FFN_COLS)
        gate = _dot(h_ref[...], wgu_ref[:, cols])
        up = _dot(h_ref[...], wgu_ref[:, up_cols])
        act = (gate * jax.nn.sigmoid(gate) * up).astype(BF16)
        o_ref[...] += _dot(act, wdown_ref[cols, :])


def _out_ffn(x2, ol, om, og, wol, wom, wog, g, wgu, wdown):
    n, d = x2.shape
    tm = min(FFN_ROWS, n)
    row = lambda i: (i, 0)
    resident = lambda a: pl.BlockSpec(a.shape, lambda i: (0,) * a.ndim, pipeline_mode=pl.Buffered(1))
    return pl.pallas_call(
        _out_ffn_kernel,
        grid=(n // tm,),
        in_specs=[pl.BlockSpec((tm, d), row), pl.BlockSpec((tm, W_LRU), row),
                  pl.BlockSpec((tm, W_MLA), row), pl.BlockSpec((tm, W_GLA), row),
                  resident(wol), resident(wom), resident(wog), resident(g),
                  resident(wgu), resident(wdown)],
        out_specs=pl.BlockSpec((tm, d), row),
        out_shape=jax.ShapeDtypeStruct((n, d), F32),
        scratch_shapes=[pltpu.VMEM((tm, d), BF16)],
        compiler_params=_cparams(("arbitrary",)),
        name="out_ffn",
    )(x2, ol, om, og, wol, wom, wog, g, wgu, wdown)


def _block_diag(w):
    nb, bi, bo = w.shape
    eye = jnp.eye(nb, dtype=w.dtype)
    return (w[:, :, None, :] * eye[:, None, :, None]).reshape(nb * bi, nb * bo)


def _swap_halves(n):
    return np.concatenate([np.arange(n // 2, n), np.arange(0, n // 2)])


def _prep_layer(p):
    row = lambda v: v.reshape(1, -1).astype(F32)
    d_model = p["w_in"].shape[0]
    qk_w = GLA_HEADS * GLA_DK
    o_cq = 2 * W_LRU
    o_ckv = o_cq + Q_RANK
    o_rope = o_ckv + KV_RANK
    o_glaq = o_rope + QK_ROPE
    o_gf = o_glaq + 2 * qk_w + W_GLA
    o_og = o_gf + 2 * GLA_GATE_RANK
    sw = _swap_halves(QK_ROPE)
    w = p["w_in"]
    pad = LANES - QK_ROPE - 2 * GLA_GATE_RANK
    w_in = jnp.concatenate([
        w[:, :o_glaq], w[:, o_gf:o_og], jnp.zeros((d_model, pad), w.dtype),
        w[:, o_glaq:o_gf], w[:, o_og:],
    ], axis=1).astype(BF16)

    wcat = (0.5 * jnp.concatenate([_block_diag(p[k]) for k in ("lru_wa_f", "lru_wx_f", "lru_wa_b", "lru_wx_b")],
                                  axis=1)).astype(BF16)
    bcat = 0.5 * jnp.concatenate([p[k] for k in ("lru_ba_f", "lru_bx_f", "lru_ba_b", "lru_bx_b")]).reshape(1, -1)
    lam = jnp.stack([p["lru_lam_f"], p["lru_lam_b"]])

    wq = p["mla_w_uq"].reshape(Q_RANK, MLA_HEADS, QK_HEAD)
    wq = jnp.concatenate([wq, wq[:, :, QK_NOPE + sw]], axis=2).reshape(Q_RANK, -1).astype(BF16)
    wkv = p["mla_w_ukv"].reshape(KV_RANK, MLA_HEADS, QK_NOPE + V_HEAD)
    wkv = jnp.concatenate([wkv[:, :, :QK_NOPE].reshape(KV_RANK, -1),
                           wkv[:, :, QK_NOPE:].reshape(KV_RANK, -1)], axis=1).astype(BF16)

    def qk_gains(g):
        pad = jnp.zeros((LANES - QK_ROPE,), F32)
        return jnp.stack([g[:QK_NOPE], jnp.concatenate([g[QK_NOPE:], pad]),
                          jnp.concatenate([g[QK_NOPE + sw], pad])]).astype(F32)

    wg = jnp.zeros((LANES, 2 * qk_w), F32)
    wg = wg.at[GATE_LANE0:GATE_LANE0 + GLA_GATE_RANK, :qk_w].set(p["gla_wa2_f"])
    wg = wg.at[GATE_LANE0 + GLA_GATE_RANK:GATE_LANE0 + 2 * GLA_GATE_RANK, qk_w:].set(p["gla_wa2_b"])
    bg = jnp.concatenate([p["gla_ba2_f"], p["gla_ba2_b"]]).reshape(1, -1)

    w_out = p["w_out"].astype(BF16)
    wgu = p["w_ffn_in"].astype(BF16)
    wdown = p["w_ffn_out"].astype(BF16)
    return dict(
        g_mix=row(p["g_mix"]), w_in=w_in,
        conv_w=p["conv_w"], conv_b=row(p["conv_b"]), wcat=wcat, bcat=bcat, lam=lam,
        gcq=row(p["mla_g_cq"]), wq=wq, gckv=row(p["mla_g_ckv"]), wkv=wkv,
        gq=qk_gains(p["mla_g_q"]), gk=qk_gains(p["mla_g_k"]),
        wg=wg.astype(BF16), bg=bg, go=row(jnp.tile(p["gla_g_o"], GLA_HEADS)),
        wol=w_out[:W_LRU], wom=w_out[W_LRU:W_LRU + W_MLA], wog=w_out[W_LRU + W_MLA:],
        g_ffn=row(p["g_ffn"]), wgu=wgu, wdown=wdown,
    )


def _rotary_tables(seq):
    half = QK_ROPE // 2
    inv = 1.0 / (ROPE_THETA ** (jnp.arange(half, dtype=F32) * 2.0 / QK_ROPE))
    ang = jnp.arange(seq, dtype=F32)[:, None] * inv[None, :]
    cos, sin = jnp.cos(ang), jnp.sin(ang)
    pad = jnp.zeros((seq, LANES - QK_ROPE), F32)
    return jnp.concatenate([cos, cos, pad], axis=1), jnp.concatenate([-sin, sin, pad], axis=1)


def _layer(x, cos, sin, lp):
    b, seq, d = x.shape
    x2 = x.reshape(b * seq, d)
    zl, zm, zg = _in_proj(x2, lp["g_mix"], lp["w_in"])
    o_lru = _lru_mixer(zl.reshape(b, seq, ZL_W), lp["conv_w"], lp["conv_b"], lp["wcat"], lp["bcat"], lp["lam"])
    o_mla = _mla_mixer(zm.reshape(b, seq, ZM_W), cos, sin, lp["gcq"], lp["wq"], lp["gckv"], lp["wkv"],
                       lp["gq"], lp["gk"])
    o_gla = _gla_mixer(zg.reshape(b, seq, ZG_W), zm.reshape(b, seq, ZM_W), lp["wg"], lp["bg"], lp["go"])
    y = _out_ffn(x2, o_lru.reshape(b * seq, W_LRU), o_mla.reshape(b * seq, W_MLA),
                 o_gla.reshape(b * seq, W_GLA), lp["wol"], lp["wom"], lp["wog"], lp["g_ffn"],
                 lp["wgu"], lp["wdown"])
    return y.reshape(b, seq, d)


_PARAM_NAMES = ("g_mix", "w_in", "conv_w", "conv_b",
                "lru_wa_f", "lru_ba_f", "lru_wx_f", "lru_bx_f", "lru_lam_f",
                "lru_wa_b", "lru_ba_b", "lru_wx_b", "lru_bx_b", "lru_lam_b",
                "mla_g_cq", "mla_w_uq", "mla_g_ckv", "mla_w_ukv", "mla_g_q", "mla_g_k",
                "gla_wa2_f", "gla_ba2_f", "gla_wa2_b", "gla_ba2_b", "gla_g_o",
                "w_out", "g_ffn", "w_ffn_in", "w_ffn_out")


def kernel(x_prompt, x_sample, g_mix, w_in, conv_w, conv_b, lru_wa_f, lru_ba_f, lru_wx_f, lru_bx_f, lru_lam_f, lru_wa_b, lru_ba_b, lru_wx_b, lru_bx_b, lru_lam_b, mla_g_cq, mla_w_uq, mla_g_ckv, mla_w_ukv, mla_g_q, mla_g_k, gla_wa2_f, gla_ba2_f, gla_wa2_b, gla_ba2_b, gla_g_o, w_out, g_ffn, w_ffn_in, w_ffn_out):
    stacked = dict(zip(_PARAM_NAMES, (
        g_mix, w_in, conv_w, conv_b, lru_wa_f, lru_ba_f, lru_wx_f, lru_bx_f, lru_lam_f,
        lru_wa_b, lru_ba_b, lru_wx_b, lru_bx_b, lru_lam_b,
        mla_g_cq, mla_w_uq, mla_g_ckv, mla_w_ukv, mla_g_q, mla_g_k,
        gla_wa2_f, gla_ba2_f, gla_wa2_b, gla_ba2_b, gla_g_o, w_out, g_ffn, w_ffn_in, w_ffn_out)))
    depth = g_mix.shape[0]
    layers = [_prep_layer({k: v[l] for k, v in stacked.items()}) for l in range(depth)]

    def trunk(x):
        cos, sin = _rotary_tables(x.shape[1])
        for lp in layers:
            x = _layer(x, cos, sin, lp)
        return x

    return trunk(x_prompt), trunk(x_sample)
```

```python
import jax
import jax.numpy as jnp
import numpy as np
from jax import lax
from jax.experimental import pallas as pl
from jax.experimental.pallas import tpu as pltpu

F32 = jnp.float32
BF16 = jnp.bfloat16

NORM_EPS = 1e-6
W_LRU = 256
W_MLA = 512
W_GLA = 256
LRU_BLOCKS = 8
LRU_C = 8.0
CONV_W = 4
CONV_LEFT = 2
MLA_HEADS = 4
QK_NOPE = 128
QK_ROPE = 64
QK_HEAD = QK_NOPE + QK_ROPE
V_HEAD = 128
Q_RANK = 256
KV_RANK = 128
ROPE_THETA = 10000.0
GLA_HEADS = 4
GLA_DK = 32
GLA_DV = 64
GLA_GATE_RANK = 16
GLA_TAU = 16.0
GLA_CHUNK = 64

LANES = 128
SUBLANES = 8
VMEM_LIMIT_BYTES = 56 * 1024 * 1024

ZL_W = 2 * W_LRU
ZM_W = Q_RANK + KV_RANK + LANES
ZG_W = 2 * GLA_HEADS * GLA_DK + W_GLA + W_GLA
GATE_LANE0 = QK_ROPE

IN_ROWS = 2048
IN_SUB_ROWS = 256
FFN_ROWS = 1024
FFN_COLS = 256
FFN_SUB_ROWS = 256
MLA_Q_ROWS = 512
MLA_KV_ROWS = 1024
MLA_KEY_BLOCK = 1024
MLA_SAFE_SHIFT = 60.0
GLA_ROWS = 256
LRU_SEGS = SUBLANES


def _dot(a, b):
    return jnp.dot(a, b, preferred_element_type=F32)


def _dot_nt(a, b):
    return lax.dot_general(a, b, (((1,), (1,)), ((), ())), preferred_element_type=F32)


def _cparams(sem):
    return pltpu.CompilerParams(dimension_semantics=sem, vmem_limit_bytes=VMEM_LIMIT_BYTES)


def _full(shape):
    return pl.BlockSpec(shape, lambda *_: (0,) * len(shape))


def _in_proj_kernel(x_ref, g_ref, w_ref, zl_ref, zm_ref, zg_ref):
    for r in range(x_ref.shape[0] // IN_SUB_ROWS):
        rows = slice(r * IN_SUB_ROWS, (r + 1) * IN_SUB_ROWS)
        h = _rms(x_ref[rows, :], g_ref[...]).astype(BF16)
        zl_ref[rows, :] = _dot(h, w_ref[:, 0:ZL_W]).astype(BF16)
        zm_ref[rows, :] = _dot(h, w_ref[:, ZL_W:ZL_W + ZM_W]).astype(BF16)
        zg_ref[rows, :] = _dot(h, w_ref[:, ZL_W + ZM_W:]).astype(BF16)


def _in_proj(x2, g, w):
    n, d = x2.shape
    tm = min(IN_ROWS, n)
    row = lambda i: (i, 0)
    return pl.pallas_call(
        _in_proj_kernel,
        grid=(n // tm,),
        in_specs=[pl.BlockSpec((tm, d), row), _full(g.shape), _full(w.shape)],
        out_specs=[pl.BlockSpec((tm, ZL_W), row), pl.BlockSpec((tm, ZM_W), row),
                   pl.BlockSpec((tm, ZG_W), row)],
        out_shape=[jax.ShapeDtypeStruct((n, ZL_W), BF16), jax.ShapeDtypeStruct((n, ZM_W), BF16),
                   jax.ShapeDtypeStruct((n, ZG_W), BF16)],
        compiler_params=_cparams(("arbitrary",)),
        name="in_proj",
    )(x2, g, w)


def _softplus(x):
    return jnp.maximum(x, 0.0) + jnp.log1p(jnp.exp(-jnp.abs(x)))


def _sigmoid(x):
    return 0.5 * jnp.tanh(0.5 * x) + 0.5


def _lru_kernel(z_ref, cw_ref, cb_ref, wcat_ref, bcat_ref, lam_ref, o_ref,
                u_ref, af_ref, vf_ref, ab_ref, vb_ref, c_ref):
    seq = z_ref.shape[1]
    seg = seq // LRU_SEGS
    pitch = seg + SUBLANES
    halves = W_LRU // LANES

    xin = z_ref[0, :, 0:W_LRU].astype(F32)
    row = lax.broadcasted_iota(jnp.int32, (seq, W_LRU), 0)
    cw = cw_ref[...]
    u = xin * cw[CONV_LEFT:CONV_LEFT + 1]
    for j in range(CONV_W):
        off = j - CONV_LEFT
        if off == 0:
            continue
        shifted = pltpu.roll(xin, (-off) % seq, 0)
        valid = (row >= -off) if off < 0 else (row < seq - off)
        u = u + jnp.where(valid, shifted, 0.0) * cw[j:j + 1]
    u_ref[...] = u + cb_ref[...]

    half_c_sp = (-0.5 * LRU_C) * _softplus(-lam_ref[...])

    def gates(s, _):
        r0 = pl.multiple_of(s * seg, seg)
        p0 = pl.multiple_of(s * pitch, SUBLANES)
        uc = u_ref[pl.ds(r0, seg), :]
        g = _dot(uc.astype(BF16), wcat_ref[...]) + bcat_ref[...]
        half_u = 0.5 * uc
        for d, (a_ref, v_ref) in enumerate(((af_ref, vf_ref), (ab_ref, vb_ref))):
            base = d * 2 * W_LRU
            t_r = jnp.tanh(g[:, base:base + W_LRU])
            t_i = jnp.tanh(g[:, base + W_LRU:base + 2 * W_LRU])
            log_a = t_r * half_c_sp[d:d + 1] + half_c_sp[d:d + 1]
            a = jnp.exp(log_a)
            v = jnp.sqrt(-jnp.tanh(log_a) * (a * a + 1.0)) * ((t_i + 1.0) * half_u)
            for hh in range(halves):
                a_ref[hh, pl.ds(p0, seg), :] = a[:, hh * LANES:(hh + 1) * LANES]
                v_ref[hh, pl.ds(p0, seg), :] = v[:, hh * LANES:(hh + 1) * LANES]
        return 0

    lax.fori_loop(0, LRU_SEGS, gates, 0, unroll=4)

    def rows_at(i):
        return pl.ds(i, LRU_SEGS, stride=pitch)

    def scan(i, carry):
        hf, pf, hb, pb = carry
        ib = seg - 1 - i
        hf_n, pf_n, hb_n, pb_n = [], [], [], []
        for hh in range(halves):
            a = af_ref[hh, rows_at(i), :]
            h = a * hf[hh] + vf_ref[hh, rows_at(i), :]
            p = a * pf[hh]
            vf_ref[hh, rows_at(i), :] = h
            af_ref[hh, rows_at(i), :] = p
            hf_n.append(h)
            pf_n.append(p)
            a = ab_ref[hh, rows_at(ib), :]
            h = a * hb[hh] + vb_ref[hh, rows_at(ib), :]
            p = a * pb[hh]
            vb_ref[hh, rows_at(ib), :] = h
            ab_ref[hh, rows_at(ib), :] = p
            hb_n.append(h)
            pb_n.append(p)
        return tuple(hf_n), tuple(pf_n), tuple(hb_n), tuple(pb_n)

    zeros = tuple(jnp.zeros((LRU_SEGS, LANES), F32) for _ in range(halves))
    ones = tuple(jnp.ones((LRU_SEGS, LANES), F32) for _ in range(halves))
    hf_e, pf_e, hb_e, pb_e = lax.fori_loop(0, seg, scan, (zeros, ones, zeros, ones), unroll=4)

    srow = lax.broadcasted_iota(jnp.int32, (LRU_SEGS, LANES), 0)
    for hh in range(halves):
        c = jnp.zeros((LRU_SEGS, LANES), F32)
        for _ in range(LRU_SEGS - 1):
            c = jnp.where(srow == 0, 0.0, pltpu.roll(hf_e[hh] + pf_e[hh] * c, 1, 0))
        c_ref[0, hh] = c
        c = jnp.zeros((LRU_SEGS, LANES), F32)
        for _ in range(LRU_SEGS - 1):
            c = jnp.where(srow == LRU_SEGS - 1, 0.0,
                          pltpu.roll(hb_e[hh] + pb_e[hh] * c, LRU_SEGS - 1, 0))
        c_ref[1, hh] = c

    def finish(s, _):
        r0 = pl.multiple_of(s * seg, seg)
        p0 = pl.multiple_of(s * pitch, SUBLANES)
        for hh in range(halves):
            h = (vf_ref[hh, pl.ds(p0, seg), :] + af_ref[hh, pl.ds(p0, seg), :] * c_ref[0, hh, pl.ds(s, 1), :]
                 + vb_ref[hh, pl.ds(p0, seg), :] + ab_ref[hh, pl.ds(p0, seg), :] * c_ref[1, hh, pl.ds(s, 1), :])
            gate = z_ref[0, pl.ds(r0, seg), W_LRU + hh * LANES:W_LRU + (hh + 1) * LANES].astype(F32)
            o_ref[0, pl.ds(r0, seg), hh * LANES:(hh + 1) * LANES] = (h * jax.nn.gelu(gate)).astype(BF16)
        return 0

    lax.fori_loop(0, LRU_SEGS, finish, 0, unroll=2)


def _lru_mixer(zl, cw, cb, wcat, bcat, lam):
    b, seq, _ = zl.shape
    pitch = seq // LRU_SEGS + SUBLANES
    scan = pltpu.VMEM((W_LRU // LANES, LRU_SEGS * pitch, LANES), F32)
    return pl.pallas_call(
        _lru_kernel,
        grid=(b,),
        in_specs=[pl.BlockSpec((1, seq, ZL_W), lambda i: (i, 0, 0)), _full(cw.shape), _full(cb.shape),
                  _full(wcat.shape), _full(bcat.shape), _full(lam.shape)],
        out_specs=pl.BlockSpec((1, seq, W_LRU), lambda i: (i, 0, 0)),
        out_shape=jax.ShapeDtypeStruct((b, seq, W_LRU), BF16),
        scratch_shapes=[pltpu.VMEM((seq, W_LRU), F32), scan, scan, scan, scan,
                        pltpu.VMEM((2, W_LRU // LANES, LRU_SEGS, LANES), F32)],
        compiler_params=_cparams(("arbitrary",)),
        name="lru_mixer",
    )(zl, cw, cb, wcat, bcat, lam)


def _rms(x, g):
    ms = jnp.mean(x * x, axis=-1, keepdims=True)
    return x * lax.rsqrt(ms + NORM_EPS) * g


def _split_bf16(x):
    hi = x.astype(BF16)
    lo = (x - hi.astype(F32)).astype(BF16)
    return hi, lo


def _group_sums(x, ones):
    return _dot(x.astype(BF16), ones)


def _block_ones(n, block):
    ri = lax.broadcasted_iota(jnp.int32, (n, n), 0) // block
    ci = lax.broadcasted_iota(jnp.int32, (n, n), 1) // block
    return jnp.where(ri == ci, 1.0, 0.0).astype(BF16)


def _mla_kernel(z_ref, cos_ref, sin_ref, gcq_ref, wq_ref, gckv_ref, wkv_ref, gq_ref, gk_ref,
                o_ref, k_scr, v_scr, q_scr):
    seq = z_ref.shape[1]
    c_kv0 = Q_RANK
    c_rope0 = Q_RANK + KV_RANK
    head_w = QK_NOPE + 2 * QK_ROPE
    ones_pair = _block_ones(2 * LANES, LANES)

    @pl.when(pl.program_id(0) == 0)
    def _():
        for h in range(MLA_HEADS):
            v_scr[h, :, V_HEAD:2 * V_HEAD] = jnp.ones((seq, V_HEAD), BF16)

    every_step = pl.program_id(0) >= 0

    @pl.when(every_step)
    def _():
        gk = gk_ref[...]
        for c in range(seq // MLA_KV_ROWS):
            rows = slice(c * MLA_KV_ROWS, (c + 1) * MLA_KV_ROWS)
            ckv = _rms(z_ref[0, rows, c_kv0:c_rope0].astype(F32), gckv_ref[...])
            kv = _dot(ckv.astype(BF16), wkv_ref[...])
            t = z_ref[0, rows, c_rope0:c_rope0 + LANES].astype(F32)
            lane = lax.broadcasted_iota(jnp.int32, t.shape, 1)
            t_sq = jnp.where(lane < QK_ROPE, t * t, 0.0)
            half = QK_ROPE // 2
            swapped = jnp.where(lane < half, pltpu.roll(t, LANES - half, 1), pltpu.roll(t, half, 1))
            rope = t * cos_ref[rows, :] * gk[1:2] + swapped * sin_ref[rows, :] * gk[2:3]
            for h in range(MLA_HEADS):
                kn = kv[:, h * QK_NOPE:(h + 1) * QK_NOPE]
                ss = jnp.sum(kn * kn + t_sq, axis=-1, keepdims=True)
                inv = lax.rsqrt(ss * (1.0 / QK_HEAD) + NORM_EPS)
                k_scr[h, rows, 0:QK_NOPE] = (kn * inv * gk[0:1]).astype(BF16)
                k_scr[h, rows, QK_NOPE:head_w] = (rope * inv).astype(BF16)
                v0 = MLA_HEADS * QK_NOPE + h * V_HEAD
                v_scr[h, rows, 0:V_HEAD] = kv[:, v0:v0 + V_HEAD].astype(BF16)

    gq = gq_ref[...]
    scale = QK_HEAD ** -0.5 * np.log2(np.e)

    def prep_queries(tile, slot):
        qrows = pl.ds(pl.multiple_of(tile * MLA_Q_ROWS, MLA_Q_ROWS), MLA_Q_ROWS)
        cq = z_ref[0, qrows, 0:Q_RANK].astype(F32)
        ms = _group_sums(cq * cq, jnp.ones((Q_RANK, Q_RANK), BF16)) * (1.0 / Q_RANK)
        cq = cq * lax.rsqrt(ms + NORM_EPS) * gcq_ref[...]
        qall = _dot(cq.astype(BF16), wq_ref[...])
        cos_g = cos_ref[qrows, :] * gq[1:2]
        sin_g = sin_ref[qrows, :] * gq[2:3]
        sq = lambda h: (qall[:, h * head_w:h * head_w + QK_NOPE] ** 2
                        + 0.5 * qall[:, h * head_w + QK_NOPE:(h + 1) * head_w] ** 2)
        for h in range(MLA_HEADS):
            qn = qall[:, h * head_w:h * head_w + QK_NOPE]
            t = qall[:, h * head_w + QK_NOPE:(h + 1) * head_w]
            if h % 2 == 0:
                ss2 = _group_sums(jnp.concatenate([sq(h), sq(h + 1)], axis=-1), ones_pair)
            ss = ss2[:, (h % 2) * LANES:(h % 2 + 1) * LANES]
            inv = lax.rsqrt(ss * (1.0 / QK_HEAD) + NORM_EPS) * scale
            rope = t * cos_g + pltpu.roll(t, QK_ROPE, 1) * sin_g
            q_scr[slot, h] = jnp.concatenate([qn * inv * gq[0:1], rope * inv], axis=-1).astype(BF16)

    g_max = lambda g: jnp.max(jnp.abs(g[0:2]))
    bound = scale * QK_HEAD * g_max(gq) * g_max(gk_ref[...])
    shift_is_safe = bound <= MLA_SAFE_SHIFT

    n_tiles = seq // MLA_Q_ROWS

    @pl.when(every_step)
    def _():
        prep_queries(0, 0)

    def attend_tile(qi, _):
        slot = qi % 2
        qrows = pl.ds(pl.multiple_of(qi * MLA_Q_ROWS, MLA_Q_ROWS), MLA_Q_ROWS)

        def prep_next():
            prep_queries(jnp.minimum(qi + 1, n_tiles - 1), 1 - slot)

        def write_head(h, ov):
            o_ref[0, qrows, h * V_HEAD:(h + 1) * V_HEAD] = (ov[:, 0:V_HEAD] / ov[:, V_HEAD:]).astype(BF16)

        @pl.when(shift_is_safe)
        def _():
            for h in range(MLA_HEADS):
                ov = jnp.zeros((MLA_Q_ROWS, 2 * V_HEAD), F32)
                for kb in range(seq // MLA_KEY_BLOCK):
                    keys = slice(kb * MLA_KEY_BLOCK, (kb + 1) * MLA_KEY_BLOCK)
                    s = _dot_nt(q_scr[slot, h], k_scr[h, keys, :])
                    ov = ov + _dot(jnp.exp2(s - bound).astype(BF16), v_scr[h, keys, :])
                write_head(h, ov)
            prep_next()

        @pl.when(jnp.logical_not(shift_is_safe))
        def _():
            for h in range(MLA_HEADS):
                s = _dot_nt(q_scr[slot, h], k_scr[h])
                m = jnp.max(s, axis=-1, keepdims=True)
                write_head(h, _dot(jnp.exp2(s - m).astype(BF16), v_scr[h]))
            prep_next()

        return 0

    lax.fori_loop(0, n_tiles, attend_tile, 0)


def _mla_mixer(zm, cos, sin, gcq, wq, gckv, wkv, gq, gk):
    b, seq, _ = zm.shape
    head_w = QK_NOPE + 2 * QK_ROPE
    return pl.pallas_call(
        _mla_kernel,
        grid=(b,),
        in_specs=[pl.BlockSpec((1, seq, ZM_W), lambda i: (i, 0, 0)), _full(cos.shape), _full(sin.shape),
                  _full(gcq.shape), _full(wq.shape), _full(gckv.shape), _full(wkv.shape),
                  _full(gq.shape), _full(gk.shape)],
        out_specs=pl.BlockSpec((1, seq, W_MLA), lambda i: (i, 0, 0)),
        out_shape=jax.ShapeDtypeStruct((b, seq, W_MLA), BF16),
        scratch_shapes=[pltpu.VMEM((MLA_HEADS, seq, head_w), BF16),
                        pltpu.VMEM((MLA_HEADS, seq, 2 * V_HEAD), BF16),
                        pltpu.VMEM((2, MLA_HEADS, MLA_Q_ROWS, head_w), BF16)],
        compiler_params=_cparams(("arbitrary",)),
        name="mla_mixer",
    )(zm, cos, sin, gcq, wq, gckv, wkv, gq, gk)


def _log_sigmoid(x):
    return jnp.minimum(x, 0.0) - jnp.log(1.0 + jnp.exp(-jnp.abs(x)))


def _gla_kernel(z_ref, zgate_ref, wg_ref, bg_ref, go_ref, o_ref,
                oi_scr, qd_scr, ki_scr, ke_scr, u_scr, dec_scr, s_scr):
    seq = z_ref.shape[1]
    n_groups = seq // GLA_ROWS
    chunks = GLA_ROWS // GLA_CHUNK
    qk_w = GLA_HEADS * GLA_DK
    k0, v0, og0 = qk_w, 2 * qk_w, 2 * qk_w + W_GLA

    ri = lax.broadcasted_iota(jnp.int32, (GLA_ROWS, GLA_ROWS), 0)
    ci = lax.broadcasted_iota(jnp.int32, (GLA_ROWS, GLA_ROWS), 1)
    same_chunk = (ri // GLA_CHUNK) == (ci // GLA_CHUNK)
    causal = (jnp.where(same_chunk, ci, GLA_ROWS) <= ri, jnp.where(same_chunk, ci, -1) >= ri)
    cum_mat = jnp.where(causal[0], 1.0, 0.0).astype(BF16)
    avg_mat = jnp.where((ri // GLA_DV) == (ci // GLA_DV), 1.0 / GLA_DV, 0.0).astype(BF16)
    qk_head = lax.broadcasted_iota(jnp.int32, (GLA_ROWS, qk_w), 1) // GLA_DK
    v_head = lax.broadcasted_iota(jnp.int32, (GLA_ROWS, W_GLA), 1) // GLA_DV
    st_mask = (lax.broadcasted_iota(jnp.int32, (W_GLA, qk_w), 0) // GLA_DV
               == lax.broadcasted_iota(jnp.int32, (W_GLA, qk_w), 1) // GLA_DK)
    col_chunk = ci // GLA_CHUNK

    def rows_of(g):
        return pl.ds(pl.multiple_of(g * GLA_ROWS, GLA_ROWS), GLA_ROWS)

    def decays(g, _):
        rows = rows_of(g)
        x = _dot(zgate_ref[0, rows, :], wg_ref[...]) + bg_ref[...]
        la = _log_sigmoid(x) * (np.log2(np.e) / GLA_TAU)
        la_hi, la_lo = _split_bf16(la)
        b_pre = _dot(cum_mat, la_hi) + _dot(cum_mat, la_lo)
        ends = [b_pre[(c + 1) * GLA_CHUNK - 1:(c + 1) * GLA_CHUNK] for c in range(chunks)]
        bt = jnp.concatenate([jnp.broadcast_to(e, (GLA_CHUNK, 2 * qk_w)) for e in ends], axis=0)
        b_dir = (b_pre[:, 0:qk_w], bt[:, qk_w:] - b_pre[:, qk_w:] + la[:, qk_w:])
        q = z_ref[0, rows, 0:qk_w].astype(F32) * (GLA_DK ** -0.5)
        k = z_ref[0, rows, k0:k0 + qk_w].astype(F32)
        for d in range(2):
            b = b_dir[d]
            cols = slice(d * qk_w, (d + 1) * qk_w)
            qd_scr[rows, cols] = (q * jnp.exp2(b)).astype(BF16)
            ki_scr[rows, cols] = (k * jnp.exp2(-b)).astype(BF16)
            ke_scr[rows, cols] = (k * jnp.exp2(bt[:, cols] - b)).astype(BF16)
        dec_scr[g] = jnp.exp2(jnp.concatenate(ends + ends, axis=0))
        return 0

    lax.fori_loop(0, n_groups, decays, 0, unroll=True)

    def local(g, _):
        rows = rows_of(g)
        v = z_ref[0, rows, v0:v0 + W_GLA]
        probs = None
        for d in range(2):
            cols = slice(d * qk_w, (d + 1) * qk_w)
            q_dec = qd_scr[rows, cols]
            q_stack = jnp.concatenate([jnp.where(qk_head == h, q_dec, jnp.zeros_like(q_dec))
                                       for h in range(GLA_HEADS)], axis=0)
            s = _dot_nt(q_stack, ki_scr[rows, cols])
            p = [jnp.where(causal[d], s[h * GLA_ROWS:(h + 1) * GLA_ROWS], 0.0) for h in range(GLA_HEADS)]
            probs = p if probs is None else [a + c for a, c in zip(probs, p)]
        v_stack = jnp.concatenate([jnp.where(v_head == h, v, jnp.zeros_like(v)) for h in range(GLA_HEADS)],
                                  axis=0)
        oi_scr[rows, :] = _dot(jnp.concatenate([p.astype(BF16) for p in probs], axis=1), v_stack)
        v_t = v.astype(F32).T
        vt_stack = jnp.concatenate([jnp.where(col_chunk == c, v_t, 0.0) for c in range(chunks)], axis=0)
        u_scr[g] = _dot(vt_stack.astype(BF16), ke_scr[rows, :])
        return 0

    lax.fori_loop(0, n_groups, local, 0, unroll=4)

    for d in range(2):
        def step(i, st, d=d):
            g = i if d == 0 else n_groups - 1 - i
            dec = dec_scr[g]
            for c in (range(chunks) if d == 0 else reversed(range(chunks))):
                blk = (c * 2 + d) * qk_w
                s_scr[g, :, blk:blk + qk_w] = st.astype(BF16)
                upd = u_scr[g, c * W_GLA:(c + 1) * W_GLA, d * qk_w:(d + 1) * qk_w]
                st = st * dec[c:c + 1, d * qk_w:(d + 1) * qk_w] + jnp.where(st_mask, upd, 0.0)
            return st

        lax.fori_loop(0, n_groups, step, jnp.zeros((W_GLA, qk_w), F32))

    def finish(g, _):
        rows = rows_of(g)
        inter = []
        for c in range(chunks):
            crow = pl.ds(pl.multiple_of(g * GLA_ROWS + c * GLA_CHUNK, GLA_CHUNK), GLA_CHUNK)
            inter.append(_dot_nt(qd_scr[crow, :], s_scr[g, :, c * 2 * qk_w:(c + 1) * 2 * qk_w]))
        o = oi_scr[rows, :] + jnp.concatenate(inter, axis=0)
        ms = _dot((o * o).astype(BF16), avg_mat)
        og = z_ref[0, rows, og0:og0 + W_GLA].astype(F32)
        y = (o * lax.rsqrt(ms + NORM_EPS) * go_ref[...]) * (og * _sigmoid(og))
        o_ref[0, rows, :] = y.astype(BF16)
        return 0

    lax.fori_loop(0, n_groups, finish, 0, unroll=True)


def _gla_mixer(zg, zm, wg, bg, go):
    b, seq, _ = zg.shape
    gate_block = (ZM_W - LANES) // LANES
    qk_w = GLA_HEADS * GLA_DK
    n_groups = seq // GLA_ROWS
    chunks = GLA_ROWS // GLA_CHUNK
    return pl.pallas_call(
        _gla_kernel,
        grid=(b,),
        in_specs=[pl.BlockSpec((1, seq, ZG_W), lambda i: (i, 0, 0)),
                  pl.BlockSpec((1, seq, LANES), lambda i: (i, 0, gate_block)),
                  _full(wg.shape), _full(bg.shape), _full(go.shape)],
        out_specs=pl.BlockSpec((1, seq, W_GLA), lambda i: (i, 0, 0)),
        out_shape=jax.ShapeDtypeStruct((b, seq, W_GLA), BF16),
        scratch_shapes=[pltpu.VMEM((seq, W_GLA), F32),
                        pltpu.VMEM((seq, 2 * qk_w), BF16),
                        pltpu.VMEM((seq, 2 * qk_w), BF16),
                        pltpu.VMEM((seq, 2 * qk_w), BF16),
                        pltpu.VMEM((n_groups, chunks * W_GLA, 2 * qk_w), F32),
                        pltpu.VMEM((n_groups, 2 * chunks, 2 * qk_w), F32),
                        pltpu.VMEM((n_groups, W_GLA, 2 * chunks * qk_w), BF16)],
        compiler_params=_cparams(("arbitrary",)),
        name="gla_mixer",
    )(zg, zm, wg, bg, go)


def _out_ffn_kernel(x_ref, ol_ref, om_ref, og_ref, wo_ref, g_ref, wgu_ref, wdown_ref, o_ref, h_ref):
    m0, g0 = W_LRU, W_LRU + W_MLA
    for r in range(x_ref.shape[0] // FFN_SUB_ROWS):
        rows = slice(r * FFN_SUB_ROWS, (r + 1) * FFN_SUB_ROWS)
        x1 = (x_ref[rows, :] + _dot(ol_ref[rows, :], wo_ref[0:m0, :]) + _dot(om_ref[rows, :], wo_ref[m0:g0, :])
              + _dot(og_ref[rows, :], wo_ref[g0:, :]))
        o_ref[rows, :] = x1
        h_ref[rows, :] = _rms(x1, g_ref[...]).astype(BF16)

    d_ff = wdown_ref.shape[0]
    for j in range(d_ff // FFN_COLS):
        cols = slice(j * FFN_COLS, (j + 1) * FFN_COLS)
        up_cols = slice(d_ff + j * FFN_COLS, d_ff + (j + 1) * FFN_COLS)
        gate = _dot(h_ref[...], wgu_ref[:, cols])
        up = _dot(h_ref[...], wgu_ref[:, up_cols])
        act = (gate * jax.nn.sigmoid(gate) * up).astype(BF16)
        o_ref[...] += _dot(act, wdown_ref[cols, :])


def _out_ffn(x2, ol, om, og, wo, g, wgu, wdown):
    n, d = x2.shape
    tm = min(FFN_ROWS, n)
    row = lambda i: (i, 0)
    resident = lambda a: pl.BlockSpec(a.shape, lambda i: (0,) * a.ndim, pipeline_mode=pl.Buffered(1))
    return pl.pallas_call(
        _out_ffn_kernel,
        grid=(n // tm,),
        in_specs=[pl.BlockSpec((tm, d), row), pl.BlockSpec((tm, W_LRU), row),
                  pl.BlockSpec((tm, W_MLA), row), pl.BlockSpec((tm, W_GLA), row),
                  resident(wo), resident(g),
                  resident(wgu), resident(wdown)],
        out_specs=pl.BlockSpec((tm, d), row),
        out_shape=jax.ShapeDtypeStruct((n, d), F32),
        scratch_shapes=[pltpu.VMEM((tm, d), BF16)],
        compiler_params=_cparams(("arbitrary",)),
        name="out_ffn",
    )(x2, ol, om, og, wo, g, wgu, wdown)


def _block_diag(w):
    nb, bi, bo = w.shape
    eye = jnp.eye(nb, dtype=w.dtype)
    return (w[:, :, None, :] * eye[:, None, :, None]).reshape(nb * bi, nb * bo)


def _swap_halves(n):
    return np.concatenate([np.arange(n // 2, n), np.arange(0, n // 2)])


def _prep_layer(p):
    row = lambda v: v.reshape(1, -1).astype(F32)
    d_model = p["w_in"].shape[0]
    qk_w = GLA_HEADS * GLA_DK
    o_cq = 2 * W_LRU
    o_ckv = o_cq + Q_RANK
    o_rope = o_ckv + KV_RANK
    o_glaq = o_rope + QK_ROPE
    o_gf = o_glaq + 2 * qk_w + W_GLA
    o_og = o_gf + 2 * GLA_GATE_RANK
    sw = _swap_halves(QK_ROPE)
    w = p["w_in"]
    pad = LANES - QK_ROPE - 2 * GLA_GATE_RANK
    w_in = jnp.concatenate([
        w[:, :o_glaq], w[:, o_gf:o_og], jnp.zeros((d_model, pad), w.dtype),
        w[:, o_glaq:o_gf], w[:, o_og:],
    ], axis=1).astype(BF16)

    wcat = (0.5 * jnp.concatenate([_block_diag(p[k]) for k in ("lru_wa_f", "lru_wx_f", "lru_wa_b", "lru_wx_b")],
                                  axis=1)).astype(BF16)
    bcat = 0.5 * jnp.concatenate([p[k] for k in ("lru_ba_f", "lru_bx_f", "lru_ba_b", "lru_bx_b")]).reshape(1, -1)
    lam = jnp.stack([p["lru_lam_f"], p["lru_lam_b"]])

    wq = p["mla_w_uq"].reshape(Q_RANK, MLA_HEADS, QK_HEAD)
    wq = jnp.concatenate([wq, wq[:, :, QK_NOPE + sw]], axis=2).reshape(Q_RANK, -1).astype(BF16)
    wkv = p["mla_w_ukv"].reshape(KV_RANK, MLA_HEADS, QK_NOPE + V_HEAD)
    wkv = jnp.concatenate([wkv[:, :, :QK_NOPE].reshape(KV_RANK, -1),
                           wkv[:, :, QK_NOPE:].reshape(KV_RANK, -1)], axis=1).astype(BF16)

    def qk_gains(g):
        pad = jnp.zeros((LANES - QK_ROPE,), F32)
        return jnp.stack([g[:QK_NOPE], jnp.concatenate([g[QK_NOPE:], pad]),
                          jnp.concatenate([g[QK_NOPE + sw], pad])]).astype(F32)

    wg = jnp.zeros((LANES, 2 * qk_w), F32)
    wg = wg.at[GATE_LANE0:GATE_LANE0 + GLA_GATE_RANK, :qk_w].set(p["gla_wa2_f"])
    wg = wg.at[GATE_LANE0 + GLA_GATE_RANK:GATE_LANE0 + 2 * GLA_GATE_RANK, qk_w:].set(p["gla_wa2_b"])
    bg = jnp.concatenate([p["gla_ba2_f"], p["gla_ba2_b"]]).reshape(1, -1)

    w_out = p["w_out"].astype(BF16)
    wgu = p["w_ffn_in"].astype(BF16)
    wdown = p["w_ffn_out"].astype(BF16)
    return dict(
        g_mix=row(p["g_mix"]), w_in=w_in,
        conv_w=p["conv_w"], conv_b=row(p["conv_b"]), wcat=wcat, bcat=bcat, lam=lam,
        gcq=row(p["mla_g_cq"]), wq=wq, gckv=row(p["mla_g_ckv"]), wkv=wkv,
        gq=qk_gains(p["mla_g_q"]), gk=qk_gains(p["mla_g_k"]),
        wg=wg.astype(BF16), bg=bg, go=row(jnp.tile(p["gla_g_o"], GLA_HEADS)),
        w_out=w_out,
        g_ffn=row(p["g_ffn"]), wgu=wgu, wdown=wdown,
    )


def _rotary_tables(seq):
    half = QK_ROPE // 2
    inv = 1.0 / (ROPE_THETA ** (jnp.arange(half, dtype=F32) * 2.0 / QK_ROPE))
    ang = jnp.arange(seq, dtype=F32)[:, None] * inv[None, :]
    cos, sin = jnp.cos(ang), jnp.sin(ang)
    pad = jnp.zeros((seq, LANES - QK_ROPE), F32)
    return jnp.concatenate([cos, cos, pad], axis=1), jnp.concatenate([-sin, sin, pad], axis=1)


def _layer(x, cos, sin, lp):
    b, seq, d = x.shape
    x2 = x.reshape(b * seq, d)
    zl, zm, zg = _in_proj(x2, lp["g_mix"], lp["w_in"])
    o_lru = _lru_mixer(zl.reshape(b, seq, ZL_W), lp["conv_w"], lp["conv_b"], lp["wcat"], lp["bcat"], lp["lam"])
    o_mla = _mla_mixer(zm.reshape(b, seq, ZM_W), cos, sin, lp["gcq"], lp["wq"], lp["gckv"], lp["wkv"],
                       lp["gq"], lp["gk"])
    o_gla = _gla_mixer(zg.reshape(b, seq, ZG_W), zm.reshape(b, seq, ZM_W), lp["wg"], lp["bg"], lp["go"])
    y = _out_ffn(x2, o_lru.reshape(b * seq, W_LRU), o_mla.reshape(b * seq, W_MLA),
                 o_gla.reshape(b * seq, W_GLA), lp["w_out"], lp["g_ffn"],
                 lp["wgu"], lp["wdown"])
    return y.reshape(b, seq, d)


_PARAM_NAMES = ("g_mix", "w_in", "conv_w", "conv_b",
                "lru_wa_f", "lru_ba_f", "lru_wx_f", "lru_bx_f", "lru_lam_f",
                "lru_wa_b", "lru_ba_b", "lru_wx_b", "lru_bx_b", "lru_lam_b",
                "mla_g_cq", "mla_w_uq", "mla_g_ckv", "mla_w_ukv", "mla_g_q", "mla_g_k",
                "gla_wa2_f", "gla_ba2_f", "gla_wa2_b", "gla_ba2_b", "gla_g_o",
                "w_out", "g_ffn", "w_ffn_in", "w_ffn_out")


def kernel(x_prompt, x_sample, g_mix, w_in, conv_w, conv_b, lru_wa_f, lru_ba_f, lru_wx_f, lru_bx_f, lru_lam_f, lru_wa_b, lru_ba_b, lru_wx_b, lru_bx_b, lru_lam_b, mla_g_cq, mla_w_uq, mla_g_ckv, mla_w_ukv, mla_g_q, mla_g_k, gla_wa2_f, gla_ba2_f, gla_wa2_b, gla_ba2_b, gla_g_o, w_out, g_ffn, w_ffn_in, w_ffn_out):
    stacked = dict(zip(_PARAM_NAMES, (
        g_mix, w_in, conv_w, conv_b, lru_wa_f, lru_ba_f, lru_wx_f, lru_bx_f, lru_lam_f,
        lru_wa_b, lru_ba_b, lru_wx_b, lru_bx_b, lru_lam_b,
        mla_g_cq, mla_w_uq, mla_g_ckv, mla_w_ukv, mla_g_q, mla_g_k,
        gla_wa2_f, gla_ba2_f, gla_wa2_b, gla_ba2_b, gla_g_o, w_out, g_ffn, w_ffn_in, w_ffn_out)))
    depth = g_mix.shape[0]
    layers = [_prep_layer({k: v[l] for k, v in stacked.items()}) for l in range(depth)]

    def trunk(x):
        cos, sin = _rotary_tables(x.shape[1])
        for lp in layers:
            x = _layer(x, cos, sin, lp)
        return x

    return trunk(x_prompt), trunk(x_sample)
```

```python
import jax
import jax.numpy as jnp
import numpy as np
from jax import lax
from jax.experimental import pallas as pl
from jax.experimental.pallas import tpu as pltpu

F32 = jnp.float32
BF16 = jnp.bfloat16

NORM_EPS = 1e-6
W_LRU = 256
W_MLA = 512
W_GLA = 256
LRU_BLOCKS = 8
LRU_C = 8.0
CONV_W = 4
CONV_LEFT = 2
MLA_HEADS = 4
QK_NOPE = 128
QK_ROPE = 64
QK_HEAD = QK_NOPE + QK_ROPE
V_HEAD = 128
Q_RANK = 256
KV_RANK = 128
ROPE_THETA = 10000.0
GLA_HEADS = 4
GLA_DK = 32
GLA_DV = 64
GLA_GATE_RANK = 16
GLA_TAU = 16.0
GLA_CHUNK = 64

LANES = 128
SUBLANES = 8
VMEM_LIMIT_BYTES = 56 * 1024 * 1024

ZL_W = 2 * W_LRU
ZM_W = Q_RANK + KV_RANK + LANES
ZG_W = 2 * GLA_HEADS * GLA_DK + W_GLA + W_GLA
GATE_LANE0 = QK_ROPE

IN_ROWS = 2048
IN_SUB_ROWS = 256
FFN_ROWS = 1024
FFN_COLS = 256
FFN_SUB_ROWS = 256
MLA_Q_ROWS = 512
MLA_KV_ROWS = 1024
MLA_KEY_BLOCK = 1024
MLA_SAFE_SHIFT = 60.0
GLA_ROWS = 256
LRU_SEGS = SUBLANES


def _dot(a, b):
    return jnp.dot(a, b, preferred_element_type=F32)


def _dot_nt(a, b):
    return lax.dot_general(a, b, (((1,), (1,)), ((), ())), preferred_element_type=F32)


def _cparams(sem):
    return pltpu.CompilerParams(dimension_semantics=sem, vmem_limit_bytes=VMEM_LIMIT_BYTES)


def _full(shape):
    return pl.BlockSpec(shape, lambda *_: (0,) * len(shape))


def _in_proj_kernel(x_ref, g_ref, w_ref, zl_ref, zm_ref, zg_ref):
    for r in range(x_ref.shape[0] // IN_SUB_ROWS):
        rows = slice(r * IN_SUB_ROWS, (r + 1) * IN_SUB_ROWS)
        h = _rms(x_ref[rows, :], g_ref[...]).astype(BF16)
        zl_ref[rows, :] = _dot(h, w_ref[:, 0:ZL_W]).astype(BF16)
        zm_ref[rows, :] = _dot(h, w_ref[:, ZL_W:ZL_W + ZM_W]).astype(BF16)
        zg_ref[rows, :] = _dot(h, w_ref[:, ZL_W + ZM_W:]).astype(BF16)


def _in_proj(x2, g, w):
    n, d = x2.shape
    tm = min(IN_ROWS, n)
    assert n % tm == 0 and tm % IN_SUB_ROWS == 0
    row = lambda i: (i, 0)
    return pl.pallas_call(
        _in_proj_kernel,
        grid=(n // tm,),
        in_specs=[pl.BlockSpec((tm, d), row), _full(g.shape), _full(w.shape)],
        out_specs=[pl.BlockSpec((tm, ZL_W), row), pl.BlockSpec((tm, ZM_W), row),
                   pl.BlockSpec((tm, ZG_W), row)],
        out_shape=[jax.ShapeDtypeStruct((n, ZL_W), BF16), jax.ShapeDtypeStruct((n, ZM_W), BF16),
                   jax.ShapeDtypeStruct((n, ZG_W), BF16)],
        compiler_params=_cparams(("arbitrary",)),
        name="in_proj",
    )(x2, g, w)


def _softplus(x):
    return jnp.maximum(x, 0.0) + jnp.log1p(jnp.exp(-jnp.abs(x)))


def _sigmoid(x):
    return 0.5 * jnp.tanh(0.5 * x) + 0.5


def _lru_kernel(z_ref, cw_ref, cb_ref, wcat_ref, bcat_ref, lam_ref, o_ref,
                u_ref, af_ref, vf_ref, ab_ref, vb_ref, c_ref):
    seq = z_ref.shape[1]
    seg = seq // LRU_SEGS
    pitch = seg + SUBLANES
    halves = W_LRU // LANES

    xin = z_ref[0, :, 0:W_LRU].astype(F32)
    row = lax.broadcasted_iota(jnp.int32, (seq, W_LRU), 0)
    cw = cw_ref[...]
    u = xin * cw[CONV_LEFT:CONV_LEFT + 1]
    for j in range(CONV_W):
        off = j - CONV_LEFT
        if off == 0:
            continue
        shifted = pltpu.roll(xin, (-off) % seq, 0)
        valid = (row >= -off) if off < 0 else (row < seq - off)
        u = u + jnp.where(valid, shifted, 0.0) * cw[j:j + 1]
    u_ref[...] = u + cb_ref[...]

    half_c_sp = (-0.5 * LRU_C) * _softplus(-lam_ref[...])

    def gates(s, _):
        r0 = pl.multiple_of(s * seg, seg)
        p0 = pl.multiple_of(s * pitch, SUBLANES)
        uc = u_ref[pl.ds(r0, seg), :]
        g = _dot(uc.astype(BF16), wcat_ref[...]) + bcat_ref[...]
        half_u = 0.5 * uc
        for d, (a_ref, v_ref) in enumerate(((af_ref, vf_ref), (ab_ref, vb_ref))):
            base = d * 2 * W_LRU
            t_r = jnp.tanh(g[:, base:base + W_LRU])
            t_i = jnp.tanh(g[:, base + W_LRU:base + 2 * W_LRU])
            log_a = t_r * half_c_sp[d:d + 1] + half_c_sp[d:d + 1]
            a = jnp.exp(log_a)
            v = jnp.sqrt(-jnp.tanh(log_a) * (a * a + 1.0)) * ((t_i + 1.0) * half_u)
            for hh in range(halves):
                a_ref[hh, pl.ds(p0, seg), :] = a[:, hh * LANES:(hh + 1) * LANES]
                v_ref[hh, pl.ds(p0, seg), :] = v[:, hh * LANES:(hh + 1) * LANES]
        return 0

    lax.fori_loop(0, LRU_SEGS, gates, 0, unroll=4)

    def rows_at(i):
        return pl.ds(i, LRU_SEGS, stride=pitch)

    def scan(i, carry):
        hf, pf, hb, pb = carry
        ib = seg - 1 - i
        hf_n, pf_n, hb_n, pb_n = [], [], [], []
        for hh in range(halves):
            a = af_ref[hh, rows_at(i), :]
            h = a * hf[hh] + vf_ref[hh, rows_at(i), :]
            p = a * pf[hh]
            vf_ref[hh, rows_at(i), :] = h
            af_ref[hh, rows_at(i), :] = p
            hf_n.append(h)
            pf_n.append(p)
            a = ab_ref[hh, rows_at(ib), :]
            h = a * hb[hh] + vb_ref[hh, rows_at(ib), :]
            p = a * pb[hh]
            vb_ref[hh, rows_at(ib), :] = h
            ab_ref[hh, rows_at(ib), :] = p
            hb_n.append(h)
            pb_n.append(p)
        return tuple(hf_n), tuple(pf_n), tuple(hb_n), tuple(pb_n)

    zeros = tuple(jnp.zeros((LRU_SEGS, LANES), F32) for _ in range(halves))
    ones = tuple(jnp.ones((LRU_SEGS, LANES), F32) for _ in range(halves))
    hf_e, pf_e, hb_e, pb_e = lax.fori_loop(0, seg, scan, (zeros, ones, zeros, ones), unroll=4)

    srow = lax.broadcasted_iota(jnp.int32, (LRU_SEGS, LANES), 0)
    for hh in range(halves):
        c = jnp.zeros((LRU_SEGS, LANES), F32)
        for _ in range(LRU_SEGS - 1):
            c = jnp.where(srow == 0, 0.0, pltpu.roll(hf_e[hh] + pf_e[hh] * c, 1, 0))
        c_ref[0, hh] = c
        c = jnp.zeros((LRU_SEGS, LANES), F32)
        for _ in range(LRU_SEGS - 1):
            c = jnp.where(srow == LRU_SEGS - 1, 0.0,
                          pltpu.roll(hb_e[hh] + pb_e[hh] * c, LRU_SEGS - 1, 0))
        c_ref[1, hh] = c

    def finish(s, _):
        r0 = pl.multiple_of(s * seg, seg)
        p0 = pl.multiple_of(s * pitch, SUBLANES)
        for hh in range(halves):
            h = (vf_ref[hh, pl.ds(p0, seg), :] + af_ref[hh, pl.ds(p0, seg), :] * c_ref[0, hh, pl.ds(s, 1), :]
                 + vb_ref[hh, pl.ds(p0, seg), :] + ab_ref[hh, pl.ds(p0, seg), :] * c_ref[1, hh, pl.ds(s, 1), :])
            gate = z_ref[0, pl.ds(r0, seg), W_LRU + hh * LANES:W_LRU + (hh + 1) * LANES].astype(F32)
            o_ref[0, pl.ds(r0, seg), hh * LANES:(hh + 1) * LANES] = (h * jax.nn.gelu(gate)).astype(BF16)
        return 0

    lax.fori_loop(0, LRU_SEGS, finish, 0, unroll=2)


def _lru_mixer(zl, cw, cb, wcat, bcat, lam):
    b, seq, _ = zl.shape
    pitch = seq // LRU_SEGS + SUBLANES
    scan = pltpu.VMEM((W_LRU // LANES, LRU_SEGS * pitch, LANES), F32)
    return pl.pallas_call(
        _lru_kernel,
        grid=(b,),
        in_specs=[pl.BlockSpec((1, seq, ZL_W), lambda i: (i, 0, 0)), _full(cw.shape), _full(cb.shape),
                  _full(wcat.shape), _full(bcat.shape), _full(lam.shape)],
        out_specs=pl.BlockSpec((1, seq, W_LRU), lambda i: (i, 0, 0)),
        out_shape=jax.ShapeDtypeStruct((b, seq, W_LRU), BF16),
        scratch_shapes=[pltpu.VMEM((seq, W_LRU), F32), scan, scan, scan, scan,
                        pltpu.VMEM((2, W_LRU // LANES, LRU_SEGS, LANES), F32)],
        compiler_params=_cparams(("arbitrary",)),
        name="lru_mixer",
    )(zl, cw, cb, wcat, bcat, lam)


def _rms(x, g):
    ms = jnp.mean(x * x, axis=-1, keepdims=True)
    return x * lax.rsqrt(ms + NORM_EPS) * g


def _split_bf16(x):
    hi = x.astype(BF16)
    lo = (x - hi.astype(F32)).astype(BF16)
    return hi, lo


def _group_sums(x, ones):
    return _dot(x.astype(BF16), ones)


def _block_ones(n, block):
    ri = lax.broadcasted_iota(jnp.int32, (n, n), 0) // block
    ci = lax.broadcasted_iota(jnp.int32, (n, n), 1) // block
    return jnp.where(ri == ci, 1.0, 0.0).astype(BF16)


def _mla_kernel(z_ref, zn_ref, cos_ref, sin_ref, gcq_ref, wq_ref, gckv_ref, wkv_ref, gq_ref, gk_ref,
                o_ref, k_scr, v_scr, q_scr):
    seq = z_ref.shape[1]
    c_kv0 = Q_RANK
    c_rope0 = Q_RANK + KV_RANK
    head_w = QK_NOPE + 2 * QK_ROPE
    ones_pair = _block_ones(2 * LANES, LANES)

    @pl.when(pl.program_id(0) == 0)
    def _():
        for h in range(MLA_HEADS):
            v_scr[h, :, V_HEAD:2 * V_HEAD] = jnp.ones((seq, V_HEAD), BF16)

    every_step = pl.program_id(0) >= 0

    @pl.when(every_step)
    def _():
        gk = gk_ref[...]
        for c in range(seq // MLA_KV_ROWS):
            rows = slice(c * MLA_KV_ROWS, (c + 1) * MLA_KV_ROWS)
            ckv = _rms(z_ref[0, rows, c_kv0:c_rope0].astype(F32), gckv_ref[...])
            kv = _dot(ckv.astype(BF16), wkv_ref[...])
            t = z_ref[0, rows, c_rope0:c_rope0 + LANES].astype(F32)
            lane = lax.broadcasted_iota(jnp.int32, t.shape, 1)
            t_sq = jnp.where(lane < QK_ROPE, t * t, 0.0)
            half = QK_ROPE // 2
            swapped = jnp.where(lane < half, pltpu.roll(t, LANES - half, 1), pltpu.roll(t, half, 1))
            rope = t * cos_ref[rows, :] * gk[1:2] + swapped * sin_ref[rows, :] * gk[2:3]
            for h in range(MLA_HEADS):
                kn = kv[:, h * QK_NOPE:(h + 1) * QK_NOPE]
                ss = jnp.sum(kn * kn + t_sq, axis=-1, keepdims=True)
                inv = lax.rsqrt(ss * (1.0 / QK_HEAD) + NORM_EPS)
                k_scr[h, rows, 0:QK_NOPE] = (kn * inv * gk[0:1]).astype(BF16)
                k_scr[h, rows, QK_NOPE:head_w] = (rope * inv).astype(BF16)
                v0 = MLA_HEADS * QK_NOPE + h * V_HEAD
                v_scr[h, rows, 0:V_HEAD] = kv[:, v0:v0 + V_HEAD].astype(BF16)

    gq = gq_ref[...]
    scale = QK_HEAD ** -0.5 * np.log2(np.e)

    def tile_cq(ref, tile):
        return ref[0, pl.ds(pl.multiple_of(tile * MLA_Q_ROWS, MLA_Q_ROWS), MLA_Q_ROWS), 0:Q_RANK]

    def prep_queries(cq, tile, slot):
        qrows = pl.ds(pl.multiple_of(tile * MLA_Q_ROWS, MLA_Q_ROWS), MLA_Q_ROWS)
        cq = cq.astype(F32)
        ms = _group_sums(cq * cq, jnp.ones((Q_RANK, Q_RANK), BF16)) * (1.0 / Q_RANK)
        cq = cq * lax.rsqrt(ms + NORM_EPS) * gcq_ref[...]
        qall = _dot(cq.astype(BF16), wq_ref[...])
        cos_g = cos_ref[qrows, :] * gq[1:2]
        sin_g = sin_ref[qrows, :] * gq[2:3]
        sq = lambda h: (qall[:, h * head_w:h * head_w + QK_NOPE] ** 2
                        + 0.5 * qall[:, h * head_w + QK_NOPE:(h + 1) * head_w] ** 2)
        for h in range(MLA_HEADS):
            qn = qall[:, h * head_w:h * head_w + QK_NOPE]
            t = qall[:, h * head_w + QK_NOPE:(h + 1) * head_w]
            if h % 2 == 0:
                ss2 = _group_sums(jnp.concatenate([sq(h), sq(h + 1)], axis=-1), ones_pair)
            ss = ss2[:, (h % 2) * LANES:(h % 2 + 1) * LANES]
            inv = lax.rsqrt(ss * (1.0 / QK_HEAD) + NORM_EPS) * scale
            rope = t * cos_g + pltpu.roll(t, QK_ROPE, 1) * sin_g
            q_scr[slot, h] = jnp.concatenate([qn * inv * gq[0:1], rope * inv], axis=-1).astype(BF16)

    g_max = lambda g: jnp.max(jnp.abs(g[0:2]))
    bound = scale * QK_HEAD * g_max(gq) * g_max(gk_ref[...])
    shift_is_safe = bound <= MLA_SAFE_SHIFT

    n_tiles = seq // MLA_Q_ROWS
    assert n_tiles % 2 == 0

    @pl.when(pl.program_id(0) == 0)
    def _():
        prep_queries(tile_cq(z_ref, 0), 0, 0)

    def attend_tile(qi, _):
        slot = qi % 2
        qrows = pl.ds(pl.multiple_of(qi * MLA_Q_ROWS, MLA_Q_ROWS), MLA_Q_ROWS)

        def prep_next():
            is_last = qi == n_tiles - 1
            nxt = jnp.where(is_last, 0, qi + 1)
            prep_queries(jnp.where(is_last, tile_cq(zn_ref, 0), tile_cq(z_ref, nxt)), nxt, 1 - slot)

        def write_head(h, ov):
            o_ref[0, qrows, h * V_HEAD:(h + 1) * V_HEAD] = (ov[:, 0:V_HEAD] / ov[:, V_HEAD:]).astype(BF16)

        @pl.when(shift_is_safe)
        def _():
            for h in range(MLA_HEADS):
                ov = jnp.zeros((MLA_Q_ROWS, 2 * V_HEAD), F32)
                for kb in range(seq // MLA_KEY_BLOCK):
                    keys = slice(kb * MLA_KEY_BLOCK, (kb + 1) * MLA_KEY_BLOCK)
                    s = _dot_nt(q_scr[slot, h], k_scr[h, keys, :])
                    ov = ov + _dot(jnp.exp2(s - bound).astype(BF16), v_scr[h, keys, :])
                write_head(h, ov)
            prep_next()

        @pl.when(jnp.logical_not(shift_is_safe))
        def _():
            for h in range(MLA_HEADS):
                s = _dot_nt(q_scr[slot, h], k_scr[h])
                m = jnp.max(s, axis=-1, keepdims=True)
                write_head(h, _dot(jnp.exp2(s - m).astype(BF16), v_scr[h]))
            prep_next()

        return 0

    lax.fori_loop(0, n_tiles, attend_tile, 0)


def _mla_mixer(zm, cos, sin, gcq, wq, gckv, wkv, gq, gk):
    b, seq, _ = zm.shape
    head_w = QK_NOPE + 2 * QK_ROPE
    return pl.pallas_call(
        _mla_kernel,
        grid=(b,),
        in_specs=[pl.BlockSpec((1, seq, ZM_W), lambda i: (i, 0, 0)),
                  pl.BlockSpec((1, seq, ZM_W), lambda i: (jnp.minimum(i + 1, b - 1), 0, 0)),
                  _full(cos.shape), _full(sin.shape),
                  _full(gcq.shape), _full(wq.shape), _full(gckv.shape), _full(wkv.shape),
                  _full(gq.shape), _full(gk.shape)],
        out_specs=pl.BlockSpec((1, seq, W_MLA), lambda i: (i, 0, 0)),
        out_shape=jax.ShapeDtypeStruct((b, seq, W_MLA), BF16),
        scratch_shapes=[pltpu.VMEM((MLA_HEADS, seq, head_w), BF16),
                        pltpu.VMEM((MLA_HEADS, seq, 2 * V_HEAD), BF16),
                        pltpu.VMEM((2, MLA_HEADS, MLA_Q_ROWS, head_w), BF16)],
        compiler_params=_cparams(("arbitrary",)),
        name="mla_mixer",
    )(zm, zm, cos, sin, gcq, wq, gckv, wkv, gq, gk)


def _log_sigmoid(x):
    return jnp.minimum(x, 0.0) - jnp.log(1.0 + jnp.exp(-jnp.abs(x)))


def _gla_kernel(z_ref, zgate_ref, wg_ref, bg_ref, go_ref, o_ref,
                oi_scr, qd_scr, ki_scr, ke_scr, u_scr, dec_scr, s_scr):
    seq = z_ref.shape[1]
    n_groups = seq // GLA_ROWS
    chunks = GLA_ROWS // GLA_CHUNK
    qk_w = GLA_HEADS * GLA_DK
    k0, v0, og0 = qk_w, 2 * qk_w, 2 * qk_w + W_GLA

    ri = lax.broadcasted_iota(jnp.int32, (GLA_ROWS, GLA_ROWS), 0)
    ci = lax.broadcasted_iota(jnp.int32, (GLA_ROWS, GLA_ROWS), 1)
    same_chunk = (ri // GLA_CHUNK) == (ci // GLA_CHUNK)
    causal = (jnp.where(same_chunk, ci, GLA_ROWS) <= ri, jnp.where(same_chunk, ci, -1) >= ri)
    cum_mat = jnp.where(causal[0], 1.0, 0.0).astype(BF16)
    avg_mat = jnp.where((ri // GLA_DV) == (ci // GLA_DV), 1.0 / GLA_DV, 0.0).astype(BF16)
    qk_head = lax.broadcasted_iota(jnp.int32, (GLA_ROWS, qk_w), 1) // GLA_DK
    v_head = lax.broadcasted_iota(jnp.int32, (GLA_ROWS, W_GLA), 1) // GLA_DV
    st_mask = (lax.broadcasted_iota(jnp.int32, (W_GLA, qk_w), 0) // GLA_DV
               == lax.broadcasted_iota(jnp.int32, (W_GLA, qk_w), 1) // GLA_DK)
    col_chunk = ci // GLA_CHUNK

    def rows_of(g):
        return pl.ds(pl.multiple_of(g * GLA_ROWS, GLA_ROWS), GLA_ROWS)

    def decays(g, _):
        rows = rows_of(g)
        x = _dot(zgate_ref[0, rows, :], wg_ref[...]) + bg_ref[...]
        la = _log_sigmoid(x) * (np.log2(np.e) / GLA_TAU)
        la_hi, la_lo = _split_bf16(la)
        b_pre = _dot(cum_mat, la_hi) + _dot(cum_mat, la_lo)
        ends = [b_pre[(c + 1) * GLA_CHUNK - 1:(c + 1) * GLA_CHUNK] for c in range(chunks)]
        bt = jnp.concatenate([jnp.broadcast_to(e, (GLA_CHUNK, 2 * qk_w)) for e in ends], axis=0)
        b_dir = (b_pre[:, 0:qk_w], bt[:, qk_w:] - b_pre[:, qk_w:] + la[:, qk_w:])
        q = z_ref[0, rows, 0:qk_w].astype(F32) * (GLA_DK ** -0.5)
        k = z_ref[0, rows, k0:k0 + qk_w].astype(F32)
        for d in range(2):
            b = b_dir[d]
            cols = slice(d * qk_w, (d + 1) * qk_w)
            qd_scr[rows, cols] = (q * jnp.exp2(b)).astype(BF16)
            ki_scr[rows, cols] = (k * jnp.exp2(-b)).astype(BF16)
            ke_scr[rows, cols] = (k * jnp.exp2(bt[:, cols] - b)).astype(BF16)
        dec_scr[g] = jnp.exp2(jnp.concatenate(ends + ends, axis=0))
        return 0

    lax.fori_loop(0, n_groups, decays, 0, unroll=True)

    def local(g, _):
        rows = rows_of(g)
        v = z_ref[0, rows, v0:v0 + W_GLA]
        probs = None
        for d in range(2):
            cols = slice(d * qk_w, (d + 1) * qk_w)
            q_dec = qd_scr[rows, cols]
            q_stack = jnp.concatenate([jnp.where(qk_head == h, q_dec, jnp.zeros_like(q_dec))
                                       for h in range(GLA_HEADS)], axis=0)
            s = _dot_nt(q_stack, ki_scr[rows, cols])
            p = [jnp.where(causal[d], s[h * GLA_ROWS:(h + 1) * GLA_ROWS], 0.0) for h in range(GLA_HEADS)]
            probs = p if probs is None else [a + c for a, c in zip(probs, p)]
        v_stack = jnp.concatenate([jnp.where(v_head == h, v, jnp.zeros_like(v)) for h in range(GLA_HEADS)],
                                  axis=0)
        oi_scr[rows, :] = _dot(jnp.concatenate([p.astype(BF16) for p in probs], axis=1), v_stack)
        v_t = v.astype(F32).T
        vt_stack = jnp.concatenate([jnp.where(col_chunk == c, v_t, 0.0) for c in range(chunks)], axis=0)
        u_scr[g] = _dot(vt_stack.astype(BF16), ke_scr[rows, :])
        return 0

    lax.fori_loop(0, n_groups, local, 0, unroll=4)

    for d in range(2):
        def step(i, st, d=d):
            g = i if d == 0 else n_groups - 1 - i
            dec = dec_scr[g]
            for c in (range(chunks) if d == 0 else reversed(range(chunks))):
                blk = (c * 2 + d) * qk_w
                s_scr[g, :, blk:blk + qk_w] = st.astype(BF16)
                upd = u_scr[g, c * W_GLA:(c + 1) * W_GLA, d * qk_w:(d + 1) * qk_w]
                st = st * dec[c:c + 1, d * qk_w:(d + 1) * qk_w] + jnp.where(st_mask, upd, 0.0)
            return st

        lax.fori_loop(0, n_groups, step, jnp.zeros((W_GLA, qk_w), F32))

    def finish(g, _):
        rows = rows_of(g)
        inter = []
        for c in range(chunks):
            crow = pl.ds(pl.multiple_of(g * GLA_ROWS + c * GLA_CHUNK, GLA_CHUNK), GLA_CHUNK)
            inter.append(_dot_nt(qd_scr[crow, :], s_scr[g, :, c * 2 * qk_w:(c + 1) * 2 * qk_w]))
        o = oi_scr[rows, :] + jnp.concatenate(inter, axis=0)
        ms = _dot((o * o).astype(BF16), avg_mat)
        og = z_ref[0, rows, og0:og0 + W_GLA].astype(F32)
        y = (o * lax.rsqrt(ms + NORM_EPS) * go_ref[...]) * (og * _sigmoid(og))
        o_ref[0, rows, :] = y.astype(BF16)
        return 0

    lax.fori_loop(0, n_groups, finish, 0, unroll=True)


def _gla_mixer(zg, zm, wg, bg, go):
    b, seq, _ = zg.shape
    gate_block = (ZM_W - LANES) // LANES
    qk_w = GLA_HEADS * GLA_DK
    n_groups = seq // GLA_ROWS
    chunks = GLA_ROWS // GLA_CHUNK
    return pl.pallas_call(
        _gla_kernel,
        grid=(b,),
        in_specs=[pl.BlockSpec((1, seq, ZG_W), lambda i: (i, 0, 0)),
                  pl.BlockSpec((1, seq, LANES), lambda i: (i, 0, gate_block)),
                  _full(wg.shape), _full(bg.shape), _full(go.shape)],
        out_specs=pl.BlockSpec((1, seq, W_GLA), lambda i: (i, 0, 0)),
        out_shape=jax.ShapeDtypeStruct((b, seq, W_GLA), BF16),
        scratch_shapes=[pltpu.VMEM((seq, W_GLA), F32),
                        pltpu.VMEM((seq, 2 * qk_w), BF16),
                        pltpu.VMEM((seq, 2 * qk_w), BF16),
                        pltpu.VMEM((seq, 2 * qk_w), BF16),
                        pltpu.VMEM((n_groups, chunks * W_GLA, 2 * qk_w), F32),
                        pltpu.VMEM((n_groups, 2 * chunks, 2 * qk_w), F32),
                        pltpu.VMEM((n_groups, W_GLA, 2 * chunks * qk_w), BF16)],
        compiler_params=_cparams(("arbitrary",)),
        name="gla_mixer",
    )(zg, zm, wg, bg, go)


def _out_ffn_kernel(x_ref, ol_ref, om_ref, og_ref, wo_ref, g_ref, wgu_ref, wdown_ref, o_ref, h_ref):
    m0, g0 = W_LRU, W_LRU + W_MLA
    for r in range(x_ref.shape[0] // FFN_SUB_ROWS):
        rows = slice(r * FFN_SUB_ROWS, (r + 1) * FFN_SUB_ROWS)
        x1 = (x_ref[rows, :] + _dot(ol_ref[rows, :], wo_ref[0:m0, :]) + _dot(om_ref[rows, :], wo_ref[m0:g0, :])
              + _dot(og_ref[rows, :], wo_ref[g0:, :]))
        o_ref[rows, :] = x1
        h_ref[rows, :] = _rms(x1, g_ref[...]).astype(BF16)

    d_ff = wdown_ref.shape[0]
    for j in range(d_ff // FFN_COLS):
        cols = slice(j * FFN_COLS, (j + 1) * FFN_COLS)
        up_cols = slice(d_ff + j * FFN_COLS, d_ff + (j + 1) * FFN_COLS)
        gate = _dot(h_ref[...], wgu_ref[:, cols])
        up = _dot(h_ref[...], wgu_ref[:, up_cols])
        act = (gate * jax.nn.sigmoid(gate) * up).astype(BF16)
        o_ref[...] += _dot(act, wdown_ref[cols, :])


def _out_ffn(x2, ol, om, og, wo, g, wgu, wdown):
    n, d = x2.shape
    tm = min(FFN_ROWS, n)
    assert n % tm == 0 and tm % FFN_SUB_ROWS == 0
    row = lambda i: (i, 0)
    resident = lambda a: pl.BlockSpec(a.shape, lambda i: (0,) * a.ndim, pipeline_mode=pl.Buffered(1))
    return pl.pallas_call(
        _out_ffn_kernel,
        grid=(n // tm,),
        in_specs=[pl.BlockSpec((tm, d), row), pl.BlockSpec((tm, W_LRU), row),
                  pl.BlockSpec((tm, W_MLA), row), pl.BlockSpec((tm, W_GLA), row),
                  resident(wo), resident(g),
                  resident(wgu), resident(wdown)],
        out_specs=pl.BlockSpec((tm, d), row),
        out_shape=jax.ShapeDtypeStruct((n, d), F32),
        scratch_shapes=[pltpu.VMEM((tm, d), BF16)],
        compiler_params=_cparams(("arbitrary",)),
        name="out_ffn",
    )(x2, ol, om, og, wo, g, wgu, wdown)


def _block_diag(w):
    nb, bi, bo = w.shape
    eye = jnp.eye(nb, dtype=w.dtype)
    return (w[:, :, None, :] * eye[:, None, :, None]).reshape(nb * bi, nb * bo)


def _swap_halves(n):
    return np.concatenate([np.arange(n // 2, n), np.arange(0, n // 2)])


def _prep_layer(p):
    row = lambda v: v.reshape(1, -1).astype(F32)
    d_model = p["w_in"].shape[0]
    qk_w = GLA_HEADS * GLA_DK
    o_cq = 2 * W_LRU
    o_ckv = o_cq + Q_RANK
    o_rope = o_ckv + KV_RANK
    o_glaq = o_rope + QK_ROPE
    o_gf = o_glaq + 2 * qk_w + W_GLA
    o_og = o_gf + 2 * GLA_GATE_RANK
    sw = _swap_halves(QK_ROPE)
    w = p["w_in"]
    pad = LANES - QK_ROPE - 2 * GLA_GATE_RANK
    w_in = jnp.concatenate([
        w[:, :o_glaq], w[:, o_gf:o_og], jnp.zeros((d_model, pad), w.dtype),
        w[:, o_glaq:o_gf], w[:, o_og:],
    ], axis=1).astype(BF16)

    wcat = (0.5 * jnp.concatenate([_block_diag(p[k]) for k in ("lru_wa_f", "lru_wx_f", "lru_wa_b", "lru_wx_b")],
                                  axis=1)).astype(BF16)
    bcat = 0.5 * jnp.concatenate([p[k] for k in ("lru_ba_f", "lru_bx_f", "lru_ba_b", "lru_bx_b")]).reshape(1, -1)
    lam = jnp.stack([p["lru_lam_f"], p["lru_lam_b"]])

    wq = p["mla_w_uq"].reshape(Q_RANK, MLA_HEADS, QK_HEAD)
    wq = jnp.concatenate([wq, wq[:, :, QK_NOPE + sw]], axis=2).reshape(Q_RANK, -1).astype(BF16)
    wkv = p["mla_w_ukv"].reshape(KV_RANK, MLA_HEADS, QK_NOPE + V_HEAD)
    wkv = jnp.concatenate([wkv[:, :, :QK_NOPE].reshape(KV_RANK, -1),
                           wkv[:, :, QK_NOPE:].reshape(KV_RANK, -1)], axis=1).astype(BF16)

    def qk_gains(g):
        pad = jnp.zeros((LANES - QK_ROPE,), F32)
        return jnp.stack([g[:QK_NOPE], jnp.concatenate([g[QK_NOPE:], pad]),
                          jnp.concatenate([g[QK_NOPE + sw], pad])]).astype(F32)

    wg = jnp.zeros((LANES, 2 * qk_w), F32)
    wg = wg.at[GATE_LANE0:GATE_LANE0 + GLA_GATE_RANK, :qk_w].set(p["gla_wa2_f"])
    wg = wg.at[GATE_LANE0 + GLA_GATE_RANK:GATE_LANE0 + 2 * GLA_GATE_RANK, qk_w:].set(p["gla_wa2_b"])
    bg = jnp.concatenate([p["gla_ba2_f"], p["gla_ba2_b"]]).reshape(1, -1)

    w_out = p["w_out"].astype(BF16)
    wgu = p["w_ffn_in"].astype(BF16)
    wdown = p["w_ffn_out"].astype(BF16)
    return dict(
        g_mix=row(p["g_mix"]), w_in=w_in,
        conv_w=p["conv_w"], conv_b=row(p["conv_b"]), wcat=wcat, bcat=bcat, lam=lam,
        gcq=row(p["mla_g_cq"]), wq=wq, gckv=row(p["mla_g_ckv"]), wkv=wkv,
        gq=qk_gains(p["mla_g_q"]), gk=qk_gains(p["mla_g_k"]),
        wg=wg.astype(BF16), bg=bg, go=row(jnp.tile(p["gla_g_o"], GLA_HEADS)),
        w_out=w_out,
        g_ffn=row(p["g_ffn"]), wgu=wgu, wdown=wdown,
    )


def _rotary_tables(seq):
    half = QK_ROPE // 2
    inv = 1.0 / (ROPE_THETA ** (jnp.arange(half, dtype=F32) * 2.0 / QK_ROPE))
    ang = jnp.arange(seq, dtype=F32)[:, None] * inv[None, :]
    cos, sin = jnp.cos(ang), jnp.sin(ang)
    pad = jnp.zeros((seq, LANES - QK_ROPE), F32)
    return jnp.concatenate([cos, cos, pad], axis=1), jnp.concatenate([-sin, sin, pad], axis=1)


def _layer(x, cos, sin, lp):
    b, seq, d = x.shape
    x2 = x.reshape(b * seq, d)
    zl, zm, zg = _in_proj(x2, lp["g_mix"], lp["w_in"])
    o_lru = _lru_mixer(zl.reshape(b, seq, ZL_W), lp["conv_w"], lp["conv_b"], lp["wcat"], lp["bcat"], lp["lam"])
    o_mla = _mla_mixer(zm.reshape(b, seq, ZM_W), cos, sin, lp["gcq"], lp["wq"], lp["gckv"], lp["wkv"],
                       lp["gq"], lp["gk"])
    o_gla = _gla_mixer(zg.reshape(b, seq, ZG_W), zm.reshape(b, seq, ZM_W), lp["wg"], lp["bg"], lp["go"])
    y = _out_ffn(x2, o_lru.reshape(b * seq, W_LRU), o_mla.reshape(b * seq, W_MLA),
                 o_gla.reshape(b * seq, W_GLA), lp["w_out"], lp["g_ffn"],
                 lp["wgu"], lp["wdown"])
    return y.reshape(b, seq, d)


_PARAM_NAMES = ("g_mix", "w_in", "conv_w", "conv_b",
                "lru_wa_f", "lru_ba_f", "lru_wx_f", "lru_bx_f", "lru_lam_f",
                "lru_wa_b", "lru_ba_b", "lru_wx_b", "lru_bx_b", "lru_lam_b",
                "mla_g_cq", "mla_w_uq", "mla_g_ckv", "mla_w_ukv", "mla_g_q", "mla_g_k",
                "gla_wa2_f", "gla_ba2_f", "gla_wa2_b", "gla_ba2_b", "gla_g_o",
                "w_out", "g_ffn", "w_ffn_in", "w_ffn_out")


def kernel(x_prompt, x_sample, g_mix, w_in, conv_w, conv_b, lru_wa_f, lru_ba_f, lru_wx_f, lru_bx_f, lru_lam_f, lru_wa_b, lru_ba_b, lru_wx_b, lru_bx_b, lru_lam_b, mla_g_cq, mla_w_uq, mla_g_ckv, mla_w_ukv, mla_g_q, mla_g_k, gla_wa2_f, gla_ba2_f, gla_wa2_b, gla_ba2_b, gla_g_o, w_out, g_ffn, w_ffn_in, w_ffn_out):
    stacked = dict(zip(_PARAM_NAMES, (
        g_mix, w_in, conv_w, conv_b, lru_wa_f, lru_ba_f, lru_wx_f, lru_bx_f, lru_lam_f,
        lru_wa_b, lru_ba_b, lru_wx_b, lru_bx_b, lru_lam_b,
        mla_g_cq, mla_w_uq, mla_g_ckv, mla_w_ukv, mla_g_q, mla_g_k,
        gla_wa2_f, gla_ba2_f, gla_wa2_b, gla_ba2_b, gla_g_o, w_out, g_ffn, w_ffn_in, w_ffn_out)))
    depth = g_mix.shape[0]
    layers = [_prep_layer({k: v[l] for k, v in stacked.items()}) for l in range(depth)]

    def trunk(x):
        cos, sin = _rotary_tables(x.shape[1])
        for lp in layers:
            x = _layer(x, cos, sin, lp)
        return x

    return trunk(x_prompt), trunk(x_sample)
```

```python
import jax
import jax.numpy as jnp
import numpy as np
from jax import lax
from jax.experimental import pallas as pl
from jax.experimental.pallas import tpu as pltpu

F32 = jnp.float32
BF16 = jnp.bfloat16

NORM_EPS = 1e-6
W_LRU = 256
W_MLA = 512
W_GLA = 256
LRU_BLOCKS = 8
LRU_C = 8.0
CONV_W = 4
CONV_LEFT = 2
MLA_HEADS = 4
QK_NOPE = 128
QK_ROPE = 64
QK_HEAD = QK_NOPE + QK_ROPE
V_HEAD = 128
Q_RANK = 256
KV_RANK = 128
ROPE_THETA = 10000.0
GLA_HEADS = 4
GLA_DK = 32
GLA_DV = 64
GLA_GATE_RANK = 16
GLA_TAU = 16.0
GLA_CHUNK = 64

LANES = 128
SUBLANES = 8
VMEM_LIMIT_BYTES = 56 * 1024 * 1024

ZL_W = 2 * W_LRU
ZM_W = Q_RANK + KV_RANK + LANES
ZG_W = 2 * GLA_HEADS * GLA_DK + W_GLA + W_GLA
GATE_LANE0 = QK_ROPE

IN_ROWS = 2048
IN_SUB_ROWS = 256
FFN_ROWS = 1024
FFN_COLS = 256
FFN_SUB_ROWS = 256
MLA_Q_ROWS = 512
MLA_KV_ROWS = 1024
MLA_KEY_BLOCK = 1024
MLA_SAFE_SHIFT = 60.0
GLA_ROWS = 256
LRU_SEGS = SUBLANES


def _dot(a, b):
    return jnp.dot(a, b, preferred_element_type=F32)


def _dot_nt(a, b):
    return lax.dot_general(a, b, (((1,), (1,)), ((), ())), preferred_element_type=F32)


def _cparams(sem):
    return pltpu.CompilerParams(dimension_semantics=sem, vmem_limit_bytes=VMEM_LIMIT_BYTES)


def _full(shape):
    return pl.BlockSpec(shape, lambda *_: (0,) * len(shape))


def _in_proj_kernel(x_ref, g_ref, w_ref, zl_ref, zm_ref, zg_ref):
    for r in range(x_ref.shape[0] // IN_SUB_ROWS):
        rows = slice(r * IN_SUB_ROWS, (r + 1) * IN_SUB_ROWS)
        h = _rms(x_ref[rows, :], g_ref[...]).astype(BF16)
        zl_ref[rows, :] = _dot(h, w_ref[:, 0:ZL_W]).astype(BF16)
        zm_ref[rows, :] = _dot(h, w_ref[:, ZL_W:ZL_W + ZM_W]).astype(BF16)
        zg_ref[rows, :] = _dot(h, w_ref[:, ZL_W + ZM_W:]).astype(BF16)


def _in_proj(x2, g, w):
    n, d = x2.shape
    tm = min(IN_ROWS, n)
    assert n % tm == 0 and tm % IN_SUB_ROWS == 0
    row = lambda i: (i, 0)
    return pl.pallas_call(
        _in_proj_kernel,
        grid=(n // tm,),
        in_specs=[pl.BlockSpec((tm, d), row), _full(g.shape), _full(w.shape)],
        out_specs=[pl.BlockSpec((tm, ZL_W), row), pl.BlockSpec((tm, ZM_W), row),
                   pl.BlockSpec((tm, ZG_W), row)],
        out_shape=[jax.ShapeDtypeStruct((n, ZL_W), BF16), jax.ShapeDtypeStruct((n, ZM_W), BF16),
                   jax.ShapeDtypeStruct((n, ZG_W), BF16)],
        compiler_params=_cparams(("arbitrary",)),
        name="in_proj",
    )(x2, g, w)


def _softplus(x):
    return jnp.maximum(x, 0.0) + jnp.log1p(jnp.exp(-jnp.abs(x)))


def _sigmoid(x):
    return 0.5 * jnp.tanh(0.5 * x) + 0.5


def _lru_kernel(z_ref, cw_ref, cb_ref, wcat_ref, bcat_ref, lam_ref, o_ref,
                u_ref, af_ref, vf_ref, ab_ref, vb_ref, c_ref):
    seq = z_ref.shape[1]
    seg = seq // LRU_SEGS
    pitch = seg + SUBLANES
    halves = W_LRU // LANES

    xin = z_ref[0, :, 0:W_LRU].astype(F32)
    row = lax.broadcasted_iota(jnp.int32, (seq, W_LRU), 0)
    cw = cw_ref[...]
    u = xin * cw[CONV_LEFT:CONV_LEFT + 1]
    for j in range(CONV_W):
        off = j - CONV_LEFT
        if off == 0:
            continue
        shifted = pltpu.roll(xin, (-off) % seq, 0)
        valid = (row >= -off) if off < 0 else (row < seq - off)
        u = u + jnp.where(valid, shifted, 0.0) * cw[j:j + 1]
    u_ref[...] = u + cb_ref[...]

    half_c_sp = (-0.5 * LRU_C) * _softplus(-lam_ref[...])

    def gates(s, _):
        r0 = pl.multiple_of(s * seg, seg)
        p0 = pl.multiple_of(s * pitch, SUBLANES)
        uc = u_ref[pl.ds(r0, seg), :]
        g = _dot(uc.astype(BF16), wcat_ref[...]) + bcat_ref[...]
        half_u = 0.5 * uc
        for d, (a_ref, v_ref) in enumerate(((af_ref, vf_ref), (ab_ref, vb_ref))):
            base = d * 2 * W_LRU
            t_r = jnp.tanh(g[:, base:base + W_LRU])
            t_i = jnp.tanh(g[:, base + W_LRU:base + 2 * W_LRU])
            log_a = t_r * half_c_sp[d:d + 1] + half_c_sp[d:d + 1]
            a = jnp.exp(log_a)
            v = jnp.sqrt(-jnp.tanh(log_a) * (a * a + 1.0)) * ((t_i + 1.0) * half_u)
            for hh in range(halves):
                a_ref[hh, pl.ds(p0, seg), :] = a[:, hh * LANES:(hh + 1) * LANES]
                v_ref[hh, pl.ds(p0, seg), :] = v[:, hh * LANES:(hh + 1) * LANES]
        return 0

    lax.fori_loop(0, LRU_SEGS, gates, 0, unroll=4)

    def rows_at(i):
        return pl.ds(i, LRU_SEGS, stride=pitch)

    def scan(i, carry):
        hf, pf, hb, pb = carry
        ib = seg - 1 - i
        hf_n, pf_n, hb_n, pb_n = [], [], [], []
        for hh in range(halves):
            a = af_ref[hh, rows_at(i), :]
            h = a * hf[hh] + vf_ref[hh, rows_at(i), :]
            p = a * pf[hh]
            vf_ref[hh, rows_at(i), :] = h
            af_ref[hh, rows_at(i), :] = p
            hf_n.append(h)
            pf_n.append(p)
            a = ab_ref[hh, rows_at(ib), :]
            h = a * hb[hh] + vb_ref[hh, rows_at(ib), :]
            p = a * pb[hh]
            vb_ref[hh, rows_at(ib), :] = h
            ab_ref[hh, rows_at(ib), :] = p
            hb_n.append(h)
            pb_n.append(p)
        return tuple(hf_n), tuple(pf_n), tuple(hb_n), tuple(pb_n)

    zeros = tuple(jnp.zeros((LRU_SEGS, LANES), F32) for _ in range(halves))
    ones = tuple(jnp.ones((LRU_SEGS, LANES), F32) for _ in range(halves))
    hf_e, pf_e, hb_e, pb_e = lax.fori_loop(0, seg, scan, (zeros, ones, zeros, ones), unroll=4)

    srow = lax.broadcasted_iota(jnp.int32, (LRU_SEGS, LANES), 0)
    for hh in range(halves):
        c = jnp.zeros((LRU_SEGS, LANES), F32)
        for _ in range(LRU_SEGS - 1):
            c = jnp.where(srow == 0, 0.0, pltpu.roll(hf_e[hh] + pf_e[hh] * c, 1, 0))
        c_ref[0, hh] = c
        c = jnp.zeros((LRU_SEGS, LANES), F32)
        for _ in range(LRU_SEGS - 1):
            c = jnp.where(srow == LRU_SEGS - 1, 0.0,
                          pltpu.roll(hb_e[hh] + pb_e[hh] * c, LRU_SEGS - 1, 0))
        c_ref[1, hh] = c

    def finish(s, _):
        r0 = pl.multiple_of(s * seg, seg)
        p0 = pl.multiple_of(s * pitch, SUBLANES)
        for hh in range(halves):
            h = (vf_ref[hh, pl.ds(p0, seg), :] + af_ref[hh, pl.ds(p0, seg), :] * c_ref[0, hh, pl.ds(s, 1), :]
                 + vb_ref[hh, pl.ds(p0, seg), :] + ab_ref[hh, pl.ds(p0, seg), :] * c_ref[1, hh, pl.ds(s, 1), :])
            gate = z_ref[0, pl.ds(r0, seg), W_LRU + hh * LANES:W_LRU + (hh + 1) * LANES].astype(F32)
            o_ref[0, pl.ds(r0, seg), hh * LANES:(hh + 1) * LANES] = (h * jax.nn.gelu(gate)).astype(BF16)
        return 0

    lax.fori_loop(0, LRU_SEGS, finish, 0, unroll=2)


def _lru_mixer(zl, cw, cb, wcat, bcat, lam):
    b, seq, _ = zl.shape
    pitch = seq // LRU_SEGS + SUBLANES
    scan = pltpu.VMEM((W_LRU // LANES, LRU_SEGS * pitch, LANES), F32)
    return pl.pallas_call(
        _lru_kernel,
        grid=(b,),
        in_specs=[pl.BlockSpec((1, seq, ZL_W), lambda i: (i, 0, 0)), _full(cw.shape), _full(cb.shape),
                  _full(wcat.shape), _full(bcat.shape), _full(lam.shape)],
        out_specs=pl.BlockSpec((1, seq, W_LRU), lambda i: (i, 0, 0)),
        out_shape=jax.ShapeDtypeStruct((b, seq, W_LRU), BF16),
        scratch_shapes=[pltpu.VMEM((seq, W_LRU), F32), scan, scan, scan, scan,
                        pltpu.VMEM((2, W_LRU // LANES, LRU_SEGS, LANES), F32)],
        compiler_params=_cparams(("arbitrary",)),
        name="lru_mixer",
    )(zl, cw, cb, wcat, bcat, lam)


def _rms(x, g):
    ms = jnp.mean(x * x, axis=-1, keepdims=True)
    return x * lax.rsqrt(ms + NORM_EPS) * g


def _split_bf16(x):
    hi = x.astype(BF16)
    lo = (x - hi.astype(F32)).astype(BF16)
    return hi, lo


def _group_sums(x, ones):
    return _dot(x.astype(BF16), ones)


def _block_ones(n, block):
    ri = lax.broadcasted_iota(jnp.int32, (n, n), 0) // block
    ci = lax.broadcasted_iota(jnp.int32, (n, n), 1) // block
    return jnp.where(ri == ci, 1.0, 0.0).astype(BF16)


def _mla_kernel(z_ref, zn_ref, cos_ref, sin_ref, gcq_ref, wq_ref, gckv_ref, wkv_ref, gq_ref, gk_ref,
                o_ref, k_scr, v_scr, q_scr):
    seq = z_ref.shape[1]
    c_kv0 = Q_RANK
    c_rope0 = Q_RANK + KV_RANK
    head_w = QK_NOPE + 2 * QK_ROPE
    ones_pair = _block_ones(2 * LANES, LANES)

    @pl.when(pl.program_id(0) == 0)
    def _():
        for h in range(MLA_HEADS):
            v_scr[h, :, V_HEAD:2 * V_HEAD] = jnp.ones((seq, V_HEAD), BF16)

    every_step = pl.program_id(0) >= 0

    @pl.when(every_step)
    def _():
        gk = gk_ref[...]
        for c in range(seq // MLA_KV_ROWS):
            rows = slice(c * MLA_KV_ROWS, (c + 1) * MLA_KV_ROWS)
            ckv = _rms(z_ref[0, rows, c_kv0:c_rope0].astype(F32), gckv_ref[...]).astype(BF16)
            t = z_ref[0, rows, c_rope0:c_rope0 + LANES].astype(F32)
            lane = lax.broadcasted_iota(jnp.int32, t.shape, 1)
            t_sq = jnp.where(lane < QK_ROPE, t * t, 0.0)
            half = QK_ROPE // 2
            swapped = jnp.where(lane < half, pltpu.roll(t, LANES - half, 1), pltpu.roll(t, half, 1))
            rope = t * cos_ref[rows, :] * gk[1:2] + swapped * sin_ref[rows, :] * gk[2:3]
            kv_w = QK_NOPE + V_HEAD
            for h in range(MLA_HEADS):
                kv = _dot(ckv, wkv_ref[:, h * kv_w:(h + 1) * kv_w])
                kn = kv[:, 0:QK_NOPE]
                ss = jnp.sum(kn * kn + t_sq, axis=-1, keepdims=True)
                inv = lax.rsqrt(ss * (1.0 / QK_HEAD) + NORM_EPS)
                k_scr[h, rows, 0:QK_NOPE] = (kn * inv * gk[0:1]).astype(BF16)
                k_scr[h, rows, QK_NOPE:head_w] = (rope * inv).astype(BF16)
                v_scr[h, rows, 0:V_HEAD] = kv[:, QK_NOPE:kv_w].astype(BF16)

    gq = gq_ref[...]
    scale = QK_HEAD ** -0.5 * np.log2(np.e)

    def tile_cq(ref, tile):
        return ref[0, pl.ds(pl.multiple_of(tile * MLA_Q_ROWS, MLA_Q_ROWS), MLA_Q_ROWS), 0:Q_RANK]

    def prep_queries(cq, tile, slot):
        qrows = pl.ds(pl.multiple_of(tile * MLA_Q_ROWS, MLA_Q_ROWS), MLA_Q_ROWS)
        cq = cq.astype(F32)
        ms = _group_sums(cq * cq, jnp.ones((Q_RANK, Q_RANK), BF16)) * (1.0 / Q_RANK)
        cq = cq * lax.rsqrt(ms + NORM_EPS) * gcq_ref[...]
        qall = _dot(cq.astype(BF16), wq_ref[...])
        cos_g = cos_ref[qrows, :] * gq[1:2]
        sin_g = sin_ref[qrows, :] * gq[2:3]
        sq = lambda h: (qall[:, h * head_w:h * head_w + QK_NOPE] ** 2
                        + 0.5 * qall[:, h * head_w + QK_NOPE:(h + 1) * head_w] ** 2)
        for h in range(MLA_HEADS):
            qn = qall[:, h * head_w:h * head_w + QK_NOPE]
            t = qall[:, h * head_w + QK_NOPE:(h + 1) * head_w]
            if h % 2 == 0:
                ss2 = _group_sums(jnp.concatenate([sq(h), sq(h + 1)], axis=-1), ones_pair)
            ss = ss2[:, (h % 2) * LANES:(h % 2 + 1) * LANES]
            inv = lax.rsqrt(ss * (1.0 / QK_HEAD) + NORM_EPS) * scale
            rope = t * cos_g + pltpu.roll(t, QK_ROPE, 1) * sin_g
            q_scr[slot, h] = jnp.concatenate([qn * inv * gq[0:1], rope * inv], axis=-1).astype(BF16)

    g_max = lambda g: jnp.max(jnp.abs(g[0:2]))
    bound = scale * QK_HEAD * g_max(gq) * g_max(gk_ref[...])
    shift_is_safe = bound <= MLA_SAFE_SHIFT

    n_tiles = seq // MLA_Q_ROWS
    assert n_tiles % 2 == 0

    @pl.when(pl.program_id(0) == 0)
    def _():
        prep_queries(tile_cq(z_ref, 0), 0, 0)

    def attend_tile(qi, _):
        slot = qi % 2
        qrows = pl.ds(pl.multiple_of(qi * MLA_Q_ROWS, MLA_Q_ROWS), MLA_Q_ROWS)

        def prep_next():
            is_last = qi == n_tiles - 1
            nxt = jnp.where(is_last, 0, qi + 1)
            prep_queries(jnp.where(is_last, tile_cq(zn_ref, 0), tile_cq(z_ref, nxt)), nxt, 1 - slot)

        def write_head(h, ov):
            o_ref[0, qrows, h * V_HEAD:(h + 1) * V_HEAD] = (ov[:, 0:V_HEAD] / ov[:, V_HEAD:]).astype(BF16)

        @pl.when(shift_is_safe)
        def _():
            for h in range(MLA_HEADS):
                ov = jnp.zeros((MLA_Q_ROWS, 2 * V_HEAD), F32)
                for kb in range(seq // MLA_KEY_BLOCK):
                    keys = slice(kb * MLA_KEY_BLOCK, (kb + 1) * MLA_KEY_BLOCK)
                    s = _dot_nt(q_scr[slot, h], k_scr[h, keys, :])
                    ov = ov + _dot(jnp.exp2(s - bound).astype(BF16), v_scr[h, keys, :])
                write_head(h, ov)
            prep_next()

        @pl.when(jnp.logical_not(shift_is_safe))
        def _():
            for h in range(MLA_HEADS):
                s = _dot_nt(q_scr[slot, h], k_scr[h])
                m = jnp.max(s, axis=-1, keepdims=True)
                write_head(h, _dot(jnp.exp2(s - m).astype(BF16), v_scr[h]))
            prep_next()

        return 0

    lax.fori_loop(0, n_tiles, attend_tile, 0)


def _mla_mixer(zm, cos, sin, gcq, wq, gckv, wkv, gq, gk):
    b, seq, _ = zm.shape
    head_w = QK_NOPE + 2 * QK_ROPE
    return pl.pallas_call(
        _mla_kernel,
        grid=(b,),
        in_specs=[pl.BlockSpec((1, seq, ZM_W), lambda i: (i, 0, 0)),
                  pl.BlockSpec((1, seq, ZM_W), lambda i: (jnp.minimum(i + 1, b - 1), 0, 0)),
                  _full(cos.shape), _full(sin.shape),
                  _full(gcq.shape), _full(wq.shape), _full(gckv.shape), _full(wkv.shape),
                  _full(gq.shape), _full(gk.shape)],
        out_specs=pl.BlockSpec((1, seq, W_MLA), lambda i: (i, 0, 0)),
        out_shape=jax.ShapeDtypeStruct((b, seq, W_MLA), BF16),
        scratch_shapes=[pltpu.VMEM((MLA_HEADS, seq, head_w), BF16),
                        pltpu.VMEM((MLA_HEADS, seq, 2 * V_HEAD), BF16),
                        pltpu.VMEM((2, MLA_HEADS, MLA_Q_ROWS, head_w), BF16)],
        compiler_params=_cparams(("arbitrary",)),
        name="mla_mixer",
    )(zm, zm, cos, sin, gcq, wq, gckv, wkv, gq, gk)


def _log_sigmoid(x):
    return jnp.minimum(x, 0.0) - jnp.log(1.0 + jnp.exp(-jnp.abs(x)))


def _gla_kernel(z_ref, zgate_ref, wg_ref, bg_ref, go_ref, o_ref,
                oi_scr, qd_scr, ki_scr, ke_scr, u_scr, dec_scr, s_scr):
    seq = z_ref.shape[1]
    n_groups = seq // GLA_ROWS
    chunks = GLA_ROWS // GLA_CHUNK
    qk_w = GLA_HEADS * GLA_DK
    k0, v0, og0 = qk_w, 2 * qk_w, 2 * qk_w + W_GLA

    ri = lax.broadcasted_iota(jnp.int32, (GLA_ROWS, GLA_ROWS), 0)
    ci = lax.broadcasted_iota(jnp.int32, (GLA_ROWS, GLA_ROWS), 1)
    same_chunk = (ri // GLA_CHUNK) == (ci // GLA_CHUNK)
    causal = (jnp.where(same_chunk, ci, GLA_ROWS) <= ri, jnp.where(same_chunk, ci, -1) >= ri)
    cum_mat = jnp.where(causal[0], 1.0, 0.0).astype(BF16)
    avg_mat = jnp.where((ri // GLA_DV) == (ci // GLA_DV), 1.0 / GLA_DV, 0.0).astype(BF16)
    qk_head = lax.broadcasted_iota(jnp.int32, (GLA_ROWS, qk_w), 1) // GLA_DK
    v_head = lax.broadcasted_iota(jnp.int32, (GLA_ROWS, W_GLA), 1) // GLA_DV
    st_mask = (lax.broadcasted_iota(jnp.int32, (W_GLA, qk_w), 0) // GLA_DV
               == lax.broadcasted_iota(jnp.int32, (W_GLA, qk_w), 1) // GLA_DK)
    col_chunk = ci // GLA_CHUNK

    def rows_of(g):
        return pl.ds(pl.multiple_of(g * GLA_ROWS, GLA_ROWS), GLA_ROWS)

    def decays(g, _):
        rows = rows_of(g)
        x = _dot(zgate_ref[0, rows, :], wg_ref[...]) + bg_ref[...]
        la = _log_sigmoid(x) * (np.log2(np.e) / GLA_TAU)
        la_hi, la_lo = _split_bf16(la)
        b_pre = _dot(cum_mat, la_hi) + _dot(cum_mat, la_lo)
        ends = [b_pre[(c + 1) * GLA_CHUNK - 1:(c + 1) * GLA_CHUNK] for c in range(chunks)]
        bt = jnp.concatenate([jnp.broadcast_to(e, (GLA_CHUNK, 2 * qk_w)) for e in ends], axis=0)
        b_dir = (b_pre[:, 0:qk_w], bt[:, qk_w:] - b_pre[:, qk_w:] + la[:, qk_w:])
        q = z_ref[0, rows, 0:qk_w].astype(F32) * (GLA_DK ** -0.5)
        k = z_ref[0, rows, k0:k0 + qk_w].astype(F32)
        for d in range(2):
            b = b_dir[d]
            cols = slice(d * qk_w, (d + 1) * qk_w)
            qd_scr[rows, cols] = (q * jnp.exp2(b)).astype(BF16)
            ki_scr[rows, cols] = (k * jnp.exp2(-b)).astype(BF16)
            ke_scr[rows, cols] = (k * jnp.exp2(bt[:, cols] - b)).astype(BF16)
        dec_scr[g] = jnp.exp2(jnp.concatenate(ends + ends, axis=0))
        return 0

    lax.fori_loop(0, n_groups, decays, 0, unroll=True)

    def local(g, _):
        rows = rows_of(g)
        v = z_ref[0, rows, v0:v0 + W_GLA]
        probs = None
        for d in range(2):
            cols = slice(d * qk_w, (d + 1) * qk_w)
            q_dec = qd_scr[rows, cols]
            q_stack = jnp.concatenate([jnp.where(qk_head == h, q_dec, jnp.zeros_like(q_dec))
                                       for h in range(GLA_HEADS)], axis=0)
            s = _dot_nt(q_stack, ki_scr[rows, cols])
            p = [jnp.where(causal[d], s[h * GLA_ROWS:(h + 1) * GLA_ROWS], 0.0) for h in range(GLA_HEADS)]
            probs = p if probs is None else [a + c for a, c in zip(probs, p)]
        v_stack = jnp.concatenate([jnp.where(v_head == h, v, jnp.zeros_like(v)) for h in range(GLA_HEADS)],
                                  axis=0)
        oi_scr[rows, :] = _dot(jnp.concatenate([p.astype(BF16) for p in probs], axis=1), v_stack)
        v_t = v.astype(F32).T
        vt_stack = jnp.concatenate([jnp.where(col_chunk == c, v_t, 0.0) for c in range(chunks)], axis=0)
        u_scr[g] = _dot(vt_stack.astype(BF16), ke_scr[rows, :])
        return 0

    lax.fori_loop(0, n_groups, local, 0, unroll=4)

    for d in range(2):
        def step(i, st, d=d):
            g = i if d == 0 else n_groups - 1 - i
            dec = dec_scr[g]
            for c in (range(chunks) if d == 0 else reversed(range(chunks))):
                blk = (c * 2 + d) * qk_w
                s_scr[g, :, blk:blk + qk_w] = st.astype(BF16)
                upd = u_scr[g, c * W_GLA:(c + 1) * W_GLA, d * qk_w:(d + 1) * qk_w]
                st = st * dec[c:c + 1, d * qk_w:(d + 1) * qk_w] + jnp.where(st_mask, upd, 0.0)
            return st

        lax.fori_loop(0, n_groups, step, jnp.zeros((W_GLA, qk_w), F32))

    def finish(g, _):
        rows = rows_of(g)
        inter = []
        for c in range(chunks):
            crow = pl.ds(pl.multiple_of(g * GLA_ROWS + c * GLA_CHUNK, GLA_CHUNK), GLA_CHUNK)
            inter.append(_dot_nt(qd_scr[crow, :], s_scr[g, :, c * 2 * qk_w:(c + 1) * 2 * qk_w]))
        o = oi_scr[rows, :] + jnp.concatenate(inter, axis=0)
        ms = _dot((o * o).astype(BF16), avg_mat)
        og = z_ref[0, rows, og0:og0 + W_GLA].astype(F32)
        y = (o * lax.rsqrt(ms + NORM_EPS) * go_ref[...]) * (og * _sigmoid(og))
        o_ref[0, rows, :] = y.astype(BF16)
        return 0

    lax.fori_loop(0, n_groups, finish, 0, unroll=True)


def _gla_mixer(zg, zm, wg, bg, go):
    b, seq, _ = zg.shape
    gate_block = (ZM_W - LANES) // LANES
    qk_w = GLA_HEADS * GLA_DK
    n_groups = seq // GLA_ROWS
    chunks = GLA_ROWS // GLA_CHUNK
    return pl.pallas_call(
        _gla_kernel,
        grid=(b,),
        in_specs=[pl.BlockSpec((1, seq, ZG_W), lambda i: (i, 0, 0)),
                  pl.BlockSpec((1, seq, LANES), lambda i: (i, 0, gate_block)),
                  _full(wg.shape), _full(bg.shape), _full(go.shape)],
        out_specs=pl.BlockSpec((1, seq, W_GLA), lambda i: (i, 0, 0)),
        out_shape=jax.ShapeDtypeStruct((b, seq, W_GLA), BF16),
        scratch_shapes=[pltpu.VMEM((seq, W_GLA), F32),
                        pltpu.VMEM((seq, 2 * qk_w), BF16),
                        pltpu.VMEM((seq, 2 * qk_w), BF16),
                        pltpu.VMEM((seq, 2 * qk_w), BF16),
                        pltpu.VMEM((n_groups, chunks * W_GLA, 2 * qk_w), F32),
                        pltpu.VMEM((n_groups, 2 * chunks, 2 * qk_w), F32),
                        pltpu.VMEM((n_groups, W_GLA, 2 * chunks * qk_w), BF16)],
        compiler_params=_cparams(("arbitrary",)),
        name="gla_mixer",
    )(zg, zm, wg, bg, go)


def _out_ffn_kernel(x_ref, ol_ref, om_ref, og_ref, wo_ref, g_ref, wgu_ref, wdown_ref, o_ref, h_ref):
    m0, g0 = W_LRU, W_LRU + W_MLA
    for r in range(x_ref.shape[0] // FFN_SUB_ROWS):
        rows = slice(r * FFN_SUB_ROWS, (r + 1) * FFN_SUB_ROWS)
        x1 = (x_ref[rows, :] + _dot(ol_ref[rows, :], wo_ref[0:m0, :]) + _dot(om_ref[rows, :], wo_ref[m0:g0, :])
              + _dot(og_ref[rows, :], wo_ref[g0:, :]))
        o_ref[rows, :] = x1
        h_ref[rows, :] = _rms(x1, g_ref[...]).astype(BF16)

    d_ff = wdown_ref.shape[0]
    for j in range(d_ff // FFN_COLS):
        cols = slice(j * FFN_COLS, (j + 1) * FFN_COLS)
        up_cols = slice(d_ff + j * FFN_COLS, d_ff + (j + 1) * FFN_COLS)
        gate = _dot(h_ref[...], wgu_ref[:, cols])
        up = _dot(h_ref[...], wgu_ref[:, up_cols])
        act = (gate * jax.nn.sigmoid(gate) * up).astype(BF16)
        o_ref[...] += _dot(act, wdown_ref[cols, :])


def _out_ffn(x2, ol, om, og, wo, g, wgu, wdown):
    n, d = x2.shape
    tm = min(FFN_ROWS, n)
    assert n % tm == 0 and tm % FFN_SUB_ROWS == 0
    row = lambda i: (i, 0)
    resident = lambda a: pl.BlockSpec(a.shape, lambda i: (0,) * a.ndim, pipeline_mode=pl.Buffered(1))
    return pl.pallas_call(
        _out_ffn_kernel,
        grid=(n // tm,),
        in_specs=[pl.BlockSpec((tm, d), row), pl.BlockSpec((tm, W_LRU), row),
                  pl.BlockSpec((tm, W_MLA), row), pl.BlockSpec((tm, W_GLA), row),
                  resident(wo), resident(g),
                  resident(wgu), resident(wdown)],
        out_specs=pl.BlockSpec((tm, d), row),
        out_shape=jax.ShapeDtypeStruct((n, d), F32),
        scratch_shapes=[pltpu.VMEM((tm, d), BF16)],
        compiler_params=_cparams(("arbitrary",)),
        name="out_ffn",
    )(x2, ol, om, og, wo, g, wgu, wdown)


def _block_diag(w):
    nb, bi, bo = w.shape
    eye = jnp.eye(nb, dtype=w.dtype)
    return (w[:, :, None, :] * eye[:, None, :, None]).reshape(nb * bi, nb * bo)


def _swap_halves(n):
    return np.concatenate([np.arange(n // 2, n), np.arange(0, n // 2)])


def _prep_layer(p):
    row = lambda v: v.reshape(1, -1).astype(F32)
    d_model = p["w_in"].shape[0]
    qk_w = GLA_HEADS * GLA_DK
    o_cq = 2 * W_LRU
    o_ckv = o_cq + Q_RANK
    o_rope = o_ckv + KV_RANK
    o_glaq = o_rope + QK_ROPE
    o_gf = o_glaq + 2 * qk_w + W_GLA
    o_og = o_gf + 2 * GLA_GATE_RANK
    sw = _swap_halves(QK_ROPE)
    w = p["w_in"]
    pad = LANES - QK_ROPE - 2 * GLA_GATE_RANK
    w_in = jnp.concatenate([
        w[:, :o_glaq], w[:, o_gf:o_og], jnp.zeros((d_model, pad), w.dtype),
        w[:, o_glaq:o_gf], w[:, o_og:],
    ], axis=1).astype(BF16)

    wcat = (0.5 * jnp.concatenate([_block_diag(p[k]) for k in ("lru_wa_f", "lru_wx_f", "lru_wa_b", "lru_wx_b")],
                                  axis=1)).astype(BF16)
    bcat = 0.5 * jnp.concatenate([p[k] for k in ("lru_ba_f", "lru_bx_f", "lru_ba_b", "lru_bx_b")]).reshape(1, -1)
    lam = jnp.stack([p["lru_lam_f"], p["lru_lam_b"]])

    wq = p["mla_w_uq"].reshape(Q_RANK, MLA_HEADS, QK_HEAD)
    wq = jnp.concatenate([wq, wq[:, :, QK_NOPE + sw]], axis=2).reshape(Q_RANK, -1).astype(BF16)
    wkv = p["mla_w_ukv"].astype(BF16)

    def qk_gains(g):
        pad = jnp.zeros((LANES - QK_ROPE,), F32)
        return jnp.stack([g[:QK_NOPE], jnp.concatenate([g[QK_NOPE:], pad]),
                          jnp.concatenate([g[QK_NOPE + sw], pad])]).astype(F32)

    wg = jnp.zeros((LANES, 2 * qk_w), F32)
    wg = wg.at[GATE_LANE0:GATE_LANE0 + GLA_GATE_RANK, :qk_w].set(p["gla_wa2_f"])
    wg = wg.at[GATE_LANE0 + GLA_GATE_RANK:GATE_LANE0 + 2 * GLA_GATE_RANK, qk_w:].set(p["gla_wa2_b"])
    bg = jnp.concatenate([p["gla_ba2_f"], p["gla_ba2_b"]]).reshape(1, -1)

    w_out = p["w_out"].astype(BF16)
    wgu = p["w_ffn_in"].astype(BF16)
    wdown = p["w_ffn_out"].astype(BF16)
    return dict(
        g_mix=row(p["g_mix"]), w_in=w_in,
        conv_w=p["conv_w"], conv_b=row(p["conv_b"]), wcat=wcat, bcat=bcat, lam=lam,
        gcq=row(p["mla_g_cq"]), wq=wq, gckv=row(p["mla_g_ckv"]), wkv=wkv,
        gq=qk_gains(p["mla_g_q"]), gk=qk_gains(p["mla_g_k"]),
        wg=wg.astype(BF16), bg=bg, go=row(jnp.tile(p["gla_g_o"], GLA_HEADS)),
        w_out=w_out,
        g_ffn=row(p["g_ffn"]), wgu=wgu, wdown=wdown,
    )


def _rotary_tables(seq):
    half = QK_ROPE // 2
    inv = 1.0 / (ROPE_THETA ** (jnp.arange(half, dtype=F32) * 2.0 / QK_ROPE))
    ang = jnp.arange(seq, dtype=F32)[:, None] * inv[None, :]
    cos, sin = jnp.cos(ang), jnp.sin(ang)
    pad = jnp.zeros((seq, LANES - QK_ROPE), F32)
    return jnp.concatenate([cos, cos, pad], axis=1), jnp.concatenate([-sin, sin, pad], axis=1)


def _layer(x, cos, sin, lp):
    b, seq, d = x.shape
    x2 = x.reshape(b * seq, d)
    zl, zm, zg = _in_proj(x2, lp["g_mix"], lp["w_in"])
    o_lru = _lru_mixer(zl.reshape(b, seq, ZL_W), lp["conv_w"], lp["conv_b"], lp["wcat"], lp["bcat"], lp["lam"])
    o_mla = _mla_mixer(zm.reshape(b, seq, ZM_W), cos, sin, lp["gcq"], lp["wq"], lp["gckv"], lp["wkv"],
                       lp["gq"], lp["gk"])
    o_gla = _gla_mixer(zg.reshape(b, seq, ZG_W), zm.reshape(b, seq, ZM_W), lp["wg"], lp["bg"], lp["go"])
    y = _out_ffn(x2, o_lru.reshape(b * seq, W_LRU), o_mla.reshape(b * seq, W_MLA),
                 o_gla.reshape(b * seq, W_GLA), lp["w_out"], lp["g_ffn"],
                 lp["wgu"], lp["wdown"])
    return y.reshape(b, seq, d)


_PARAM_NAMES = ("g_mix", "w_in", "conv_w", "conv_b",
                "lru_wa_f", "lru_ba_f", "lru_wx_f", "lru_bx_f", "lru_lam_f",
                "lru_wa_b", "lru_ba_b", "lru_wx_b", "lru_bx_b", "lru_lam_b",
                "mla_g_cq", "mla_w_uq", "mla_g_ckv", "mla_w_ukv", "mla_g_q", "mla_g_k",
                "gla_wa2_f", "gla_ba2_f", "gla_wa2_b", "gla_ba2_b", "gla_g_o",
                "w_out", "g_ffn", "w_ffn_in", "w_ffn_out")


def kernel(x_prompt, x_sample, g_mix, w_in, conv_w, conv_b, lru_wa_f, lru_ba_f, lru_wx_f, lru_bx_f, lru_lam_f, lru_wa_b, lru_ba_b, lru_wx_b, lru_bx_b, lru_lam_b, mla_g_cq, mla_w_uq, mla_g_ckv, mla_w_ukv, mla_g_q, mla_g_k, gla_wa2_f, gla_ba2_f, gla_wa2_b, gla_ba2_b, gla_g_o, w_out, g_ffn, w_ffn_in, w_ffn_out):
    stacked = dict(zip(_PARAM_NAMES, (
        g_mix, w_in, conv_w, conv_b, lru_wa_f, lru_ba_f, lru_wx_f, lru_bx_f, lru_lam_f,
        lru_wa_b, lru_ba_b, lru_wx_b, lru_bx_b, lru_lam_b,
        mla_g_cq, mla_w_uq, mla_g_ckv, mla_w_ukv, mla_g_q, mla_g_k,
        gla_wa2_f, gla_ba2_f, gla_wa2_b, gla_ba2_b, gla_g_o, w_out, g_ffn, w_ffn_in, w_ffn_out)))
    depth = g_mix.shape[0]
    layers = [_prep_layer({k: v[l] for k, v in stacked.items()}) for l in range(depth)]

    def trunk(x):
        cos, sin = _rotary_tables(x.shape[1])
        for lp in layers:
            x = _layer(x, cos, sin, lp)
        return x

    return trunk(x_prompt), trunk(x_sample)
```
